```python
import math
import jax, jax.numpy as jnp
from jax import lax
import numpy as np

D_MODEL = 2048
BATCH = 2
SEQ = 4096
DEPTH = 4

HEAD_DIM = 128
N_HEADS = D_MODEL // HEAD_DIM
N_SB_HEADS = N_HEADS // 2
N_DIL_HEADS = N_HEADS - N_SB_HEADS
N_FOX_HEADS = N_HEADS
DILATED_CONFIGS = ((128, 1), (512, 4), (2048, 16))
ROPE_THETA = 500000.0
ROPE_DIM = HEAD_DIM // 4
D_FF = 256 * ((8 * D_MODEL // 3 + 255) // 256)
CONV_WIDTH = 3
PLE_DIM = 256
Q_BLOCK = 128
RMS_EPS = 1e-6

kernel_name = "hybrid_stickbreak_dilated_fox_convffn"


def rmsnorm(x, g):
    x32 = x.astype(jnp.float32)
    y = x32 * lax.rsqrt(jnp.mean(x32 * x32, axis=-1, keepdims=True) + RMS_EPS)
    return (y * g.astype(jnp.float32)).astype(x.dtype)


def partial_rope(x, positions):
    half = ROPE_DIM // 2
    inv_freq = ROPE_THETA ** (-jnp.arange(0, ROPE_DIM, 2, dtype=jnp.float32) / ROPE_DIM)
    ang = positions.astype(jnp.float32)[..., None] * inv_freq
    cos = jnp.cos(ang)[:, :, None, :].astype(x.dtype)
    sin = jnp.sin(ang)[:, :, None, :].astype(x.dtype)
    x1 = x[..., :half]
    x2 = x[..., half:ROPE_DIM]
    return jnp.concatenate([x1 * cos - x2 * sin, x2 * cos + x1 * sin, x[..., ROPE_DIM:]], axis=-1)


def stick_breaking_attention(q, k, v):
    B, S, H, dh = q.shape
    nb = S // Q_BLOCK
    scale = dh ** -0.5
    qb = q.reshape(B, nb, Q_BLOCK, H, dh).transpose(1, 0, 2, 3, 4)
    kpos = jnp.arange(S)

    def one_block(args):
        qblk, b_idx = args
        z = jnp.einsum('bqhd,bkhd->bhqk', qblk, k).astype(jnp.float32) * scale
        qpos = b_idx * Q_BLOCK + jnp.arange(Q_BLOCK)
        mask = kpos[None, :] < qpos[:, None]
        log_1mb = jnp.where(mask, jax.nn.log_sigmoid(-z), 0.0)
        suffix = lax.cumsum(log_1mb, axis=3, reverse=True) - log_1mb
        w = jnp.where(mask, jnp.exp(jax.nn.log_sigmoid(z) + suffix), 0.0)
        return jnp.einsum('bhqk,bkhd->bqhd', w.astype(v.dtype), v)

    out = lax.map(one_block, (qb, jnp.arange(nb)))
    return out.transpose(1, 0, 2, 3, 4).reshape(B, S, H, dh)


def banded_window_attention(q, k, v, n_back):
    B, G, L, H, dh = q.shape
    Lp = -(-L // Q_BLOCK) * Q_BLOCK
    padw = ((0, 0), (0, 0), (0, Lp - L), (0, 0), (0, 0))
    q, k, v = jnp.pad(q, padw), jnp.pad(k, padw), jnp.pad(v, padw)
    nb = Lp // Q_BLOCK
    qb = q.reshape(B, G, nb, Q_BLOCK, H, dh)
    kb = k.reshape(B, G, nb, Q_BLOCK, H, dh)
    vb = v.reshape(B, G, nb, Q_BLOCK, H, dh)
    blk_pad = ((0, 0), (0, 0), (1, 0), (0, 0), (0, 0), (0, 0))
    kk = jnp.concatenate([jnp.pad(kb, blk_pad)[:, :, :nb], kb], axis=3)
    vv = jnp.concatenate([jnp.pad(vb, blk_pad)[:, :, :nb], vb], axis=3)
    s = jnp.einsum('bgnqhd,bgnkhd->bgnhqk', qb, kk).astype(jnp.float32) * (dh ** -0.5)
    qi = jnp.arange(Q_BLOCK)[:, None] + Q_BLOCK
    kj = jnp.arange(2 * Q_BLOCK)[None, :]
    dist = qi - kj
    band = (dist >= 0) & (dist <= n_back)
    blk = jnp.arange(nb)
    valid = band[None] & ((blk[:, None, None] > 0) | (kj[None] >= Q_BLOCK))
    s = jnp.where(valid[:, None], s, -jnp.inf)
    lse = jax.nn.logsumexp(s, axis=-1)
    pr = jnp.exp(s - lse[..., None])
    out = jnp.einsum('bgnhqk,bgnkhd->bgnqhd', pr.astype(v.dtype), vv)
    out = out.reshape(B, G, Lp, H, dh)[:, :, :L]
    lse = lse.transpose(0, 1, 2, 4, 3).reshape(B, G, Lp, H)[:, :, :L]
    return out, lse


def to_stride_groups(x, dil):
    B, S, H, dh = x.shape
    return x.reshape(B, S // dil, dil, H, dh).transpose(0, 2, 1, 3, 4)


def dilated_attention(q, k, v):
    B, S, H, dh = q.shape
    outs, lses = [], []
    for window, dil in DILATED_CONFIGS:
        o, l = banded_window_attention(to_stride_groups(q, dil), to_stride_groups(k, dil),
                                       to_stride_groups(v, dil), window // dil)
        outs.append(o.transpose(0, 2, 1, 3, 4).reshape(B, S, H, dh))
        lses.append(l.transpose(0, 2, 1, 3).reshape(B, S, H))
    w = jax.nn.softmax(jnp.stack(lses, axis=0), axis=0)
    out = jnp.einsum('cbsh,cbshd->bshd', w, jnp.stack(outs, axis=0).astype(jnp.float32))
    return out.astype(q.dtype)


def forgetting_attention(q, k, v, log_f):
    B, S, H, dh = q.shape
    nb = S // Q_BLOCK
    scale = dh ** -0.5
    c = jnp.cumsum(log_f, axis=1).transpose(0, 2, 1)
    qb = q.reshape(B, nb, Q_BLOCK, H, dh).transpose(1, 0, 2, 3, 4)
    cb = c.reshape(B, H, nb, Q_BLOCK).transpose(2, 0, 1, 3)
    kpos = jnp.arange(S)

    def one_block(args):
        qblk, cq, b_idx = args
        s = jnp.einsum('bqhd,bkhd->bhqk', qblk, k).astype(jnp.float32) * scale
        s = s + cq[..., None] - c[:, :, None, :]
        qpos = b_idx * Q_BLOCK + jnp.arange(Q_BLOCK)
        s = jnp.where(kpos[None, :] <= qpos[:, None], s, -jnp.inf)
        pr = jax.nn.softmax(s, axis=-1)
        return jnp.einsum('bhqk,bkhd->bqhd', pr.astype(v.dtype), v)

    out = lax.map(one_block, (qb, cb, jnp.arange(nb)))
    return out.transpose(1, 0, 2, 3, 4).reshape(B, S, H, dh)


def sb_dilated_mixer(h, positions, w_in, w_out):
    B, S, D = h.shape
    qkv = (h @ w_in).reshape(B, S, 3, N_HEADS, HEAD_DIM)
    q, k, v = qkv[:, :, 0], qkv[:, :, 1], qkv[:, :, 2]
    sb = stick_breaking_attention(q[:, :, :N_SB_HEADS], k[:, :, :N_SB_HEADS], v[:, :, :N_SB_HEADS])
    qd = partial_rope(q[:, :, N_SB_HEADS:], positions)
    kd = partial_rope(k[:, :, N_SB_HEADS:], positions)
    dl = dilated_attention(qd, kd, v[:, :, N_SB_HEADS:])
    o = jnp.concatenate([sb, dl], axis=2).reshape(B, S, D)
    return o @ w_out


def fox_mixer(h, w_in, b_forget, w_out):
    B, S, D = h.shape
    proj = h @ w_in
    qkv = proj[..., :3 * D].reshape(B, S, 3, N_FOX_HEADS, HEAD_DIM)
    log_f = jax.nn.log_sigmoid(proj[..., 3 * D:].astype(jnp.float32) + b_forget.astype(jnp.float32))
    o = forgetting_attention(qkv[:, :, 0], qkv[:, :, 1], qkv[:, :, 2], log_f)
    return o.reshape(B, S, D) @ w_out


def conv_ffn(h, w_up, conv_w, conv_b, w_down):
    u = h @ w_up
    C = u.shape[-1]
    u = lax.conv_general_dilated(u, conv_w[:, None, :], window_strides=(1,),
                                 padding=((CONV_WIDTH - 1, 0),),
                                 dimension_numbers=('NWC', 'WIO', 'NWC'),
                                 feature_group_count=C) + conv_b
    gate, val = jnp.split(u, 2, axis=-1)
    return (jax.nn.silu(gate) * val) @ w_down


def setup_inputs(seed: int = 0) -> dict:
    key = jax.random.key(seed)
    ks = jax.random.split(key, 20)
    D, F = D_MODEL, D_FF
    n_even = (DEPTH + 1) // 2
    n_odd = DEPTH // 2
    f32 = jnp.float32

    def normal(k, shape, fan_in):
        return jax.random.normal(k, shape, f32) * (fan_in ** -0.5)

    def gain(k, shape):
        return 1.0 + 0.02 * jax.random.normal(k, shape, f32)

    x = jax.random.normal(ks[0], (BATCH, SEQ, D), f32)
    p = jax.random.normal(ks[1], (DEPTH, BATCH, SEQ, PLE_DIM), f32)
    offsets = jax.random.randint(ks[2], (BATCH, 1), 0, 4096, dtype=jnp.int32)
    positions = offsets + jnp.arange(SEQ, dtype=jnp.int32)[None, :]
    return {
        "x": x,
        "p": p,
        "positions": positions,
        "attn_norm": gain(ks[3], (DEPTH, D)),
        "w_in_even": normal(ks[4], (n_even, D, 3 * D), D),
        "w_out_even": normal(ks[5], (n_even, D, D), D),
        "w_in_odd": normal(ks[6], (n_odd, D, 3 * D + N_FOX_HEADS), D),
        "b_forget": jax.random.uniform(ks[7], (n_odd, N_FOX_HEADS), f32, 1.0, 6.0),
        "w_out_odd": normal(ks[8], (n_odd, D, D), D),
        "ffn_norm": gain(ks[9], (DEPTH, D)),
        "w_up": normal(ks[10], (DEPTH, D, 2 * F), D),
        "conv_w": normal(ks[11], (DEPTH, CONV_WIDTH, 2 * F), CONV_WIDTH),
        "conv_b": 0.02 * jax.random.normal(ks[12], (DEPTH, 2 * F), f32),
        "w_down": normal(ks[13], (DEPTH, F, D), F),
        "ple_norm": gain(ks[14], (DEPTH, D)),
        "w_ple_gate": normal(ks[15], (DEPTH, D, D), D),
        "w_ple": normal(ks[16], (DEPTH, PLE_DIM, D), PLE_DIM),
        "final_norm": gain(ks[17], (D,)),
    }


def reference(x, p, positions, attn_norm, w_in_even, w_out_even, w_in_odd, b_forget,
              w_out_odd, ffn_norm, w_up, conv_w, conv_b, w_down, ple_norm, w_ple_gate,
              w_ple, final_norm):
    h = x
    for i in range(DEPTH):
        a = rmsnorm(h, attn_norm[i])
        if i % 2 == 0:
            h = h + sb_dilated_mixer(a, positions, w_in_even[i // 2], w_out_even[i // 2])
        else:
            h = h + fox_mixer(a, w_in_odd[i // 2], b_forget[i // 2], w_out_odd[i // 2])
        h = h + conv_ffn(rmsnorm(h, ffn_norm[i]), w_up[i], conv_w[i], conv_b[i], w_down[i])
        gate = jax.nn.sigmoid(rmsnorm(h, ple_norm[i]) @ w_ple_gate[i])
        h = h + gate * (p[i] @ w_ple[i])
    return rmsnorm(h, final_norm)
```

```python
import functools

import jax
import jax.numpy as jnp
from jax import lax
from jax.experimental import pallas as pl
from jax.experimental.pallas import tpu as pltpu

HEAD_DIM = 128
ROPE_DIM = HEAD_DIM // 4
ROPE_THETA = 500000.0
DILATED_CONFIGS = ((128, 1), (512, 4), (2048, 16))
BAND_BLOCK = 128
RMS_EPS = 1e-6
CONV_WIDTH = 3
MASK_VALUE = -1e30
SUBLANES = 8
VMEM_CAP_BYTES = 60 * 1024 * 1024

F32 = jnp.float32
BF16 = jnp.bfloat16


def _params(semantics, vmem_bytes):
    return pltpu.CompilerParams(dimension_semantics=semantics,
                                vmem_limit_bytes=int(min(vmem_bytes, VMEM_CAP_BYTES)))


def _tile(dim, pref):
    t = min(pref, dim)
    while dim % t:
        t //= 2
    return t


def _nt_dot(a, b):
    return lax.dot_general(a, b, (((1,), (1,)), ((), ())), preferred_element_type=F32)


def _softplus_neg_abs(z):
    return jnp.log1p(jnp.exp(-jnp.abs(z)))


def _split_bf16(x, parts):
    out = []
    for _ in range(parts - 1):
        hi = x.astype(BF16)
        out.append(hi)
        x = x - hi.astype(F32)
    out.append(x.astype(BF16))
    return out


def _rmsnorm_body(h_ref, g_ref, o_ref):
    x = h_ref[...]
    ms = jnp.mean(x * x, axis=-1, keepdims=True)
    o_ref[...] = (x * lax.rsqrt(ms + RMS_EPS) * g_ref[...]).astype(o_ref.dtype)


def _rmsnorm(h, g, out_dtype):
    T, D = h.shape
    tm = _tile(T, 512)
    blk = tm * D * 4
    return pl.pallas_call(
        _rmsnorm_body,
        out_shape=jax.ShapeDtypeStruct((T, D), out_dtype),
        grid=(T // tm,),
        in_specs=[pl.BlockSpec((tm, D), lambda m: (m, 0)),
                  pl.BlockSpec((1, D), lambda m: (0, 0))],
        out_specs=pl.BlockSpec((tm, D), lambda m: (m, 0)),
        compiler_params=_params(("parallel",), 6 * blk),
        name="rmsnorm",
    )(h, g.reshape(1, D))


def _rope_table_body(pos_ref, invf_ref, cos_ref, sa_ref, sb_ref):
    ang = pos_ref[...].astype(F32) * invf_ref[...]
    lane = lax.broadcasted_iota(jnp.int32, ang.shape, 1)
    half = ROPE_DIM // 2
    s = jnp.sin(ang)
    cos_ref[...] = jnp.cos(ang)
    sa_ref[...] = jnp.where((lane >= half) & (lane < ROPE_DIM), s, 0.0)
    sb_ref[...] = jnp.where(lane < half, -s, 0.0)


def _rope_tables(positions):
    T = positions.size
    tm = _tile(T, 512)
    half = ROPE_DIM // 2
    inv_freq = ROPE_THETA ** (-jnp.arange(0, ROPE_DIM, 2, dtype=F32) / ROPE_DIM)
    invf = jnp.concatenate([inv_freq, inv_freq, jnp.zeros((HEAD_DIM - 2 * half,), F32)]).reshape(1, HEAD_DIM)
    tab = jax.ShapeDtypeStruct((T, HEAD_DIM), F32)
    spec = pl.BlockSpec((tm, HEAD_DIM), lambda m: (m, 0))
    return pl.pallas_call(
        _rope_table_body,
        out_shape=(tab, tab, tab),
        grid=(T // tm,),
        in_specs=[pl.BlockSpec((tm, 1), lambda m: (m, 0)),
                  pl.BlockSpec((1, HEAD_DIM), lambda m: (0, 0))],
        out_specs=(spec, spec, spec),
        compiler_params=_params(("parallel",), 32 * tm * HEAD_DIM * 4),
        name="rope_tables",
    )(positions.reshape(T, 1), invf)


def _qkv_body(*refs, rope, scale, tiles_per_section, heads_per_tile, first_rope_head):
    if rope:
        a_ref, w_ref, cos_ref, sa_ref, sb_ref, o_ref, wbf_ref = refs
    else:
        a_ref, w_ref, o_ref, wbf_ref = refs
    n = pl.program_id(0)

    @pl.when(pl.program_id(1) == 0)
    def _():
        wbf_ref[...] = w_ref[...].astype(BF16)

    acc = jnp.dot(a_ref[...], wbf_ref[...], preferred_element_type=F32)
    acc = acc * jnp.where(n < tiles_per_section, scale, 1.0).astype(F32)
    if not rope:
        o_ref[...] = acc.astype(o_ref.dtype)
        return

    section = n // tiles_per_section
    first_head = (n % tiles_per_section) * heads_per_tile
    is_rope = jnp.logical_and(section < 2, first_head >= first_rope_head)

    @pl.when(is_rope)
    def _():
        c, sa, sb = cos_ref[...], sa_ref[...], sb_ref[...]
        half = ROPE_DIM // 2
        for j in range(heads_per_tile):
            x = acc[:, j * HEAD_DIM:(j + 1) * HEAD_DIM]
            y = x * c + pltpu.roll(x, half, 1) * sa + pltpu.roll(x, HEAD_DIM - half, 1) * sb
            o_ref[:, j * HEAD_DIM:(j + 1) * HEAD_DIM] = y.astype(o_ref.dtype)

    @pl.when(jnp.logical_not(is_rope))
    def _():
        o_ref[...] = acc.astype(o_ref.dtype)


def _qkv_proj(a, w, layer, rope_tabs, first_rope_head):
    T, D = a.shape
    rope = rope_tabs is not None
    tm = _tile(T, 512)
    tn = _tile(D, 512)
    if rope:
        tn = min(tn, first_rope_head * HEAD_DIM)
    body = functools.partial(
        _qkv_body, rope=rope, scale=HEAD_DIM ** -0.5, tiles_per_section=D // tn,
        heads_per_tile=tn // HEAD_DIM, first_rope_head=first_rope_head)
    in_specs = [pl.BlockSpec((tm, D), lambda n, m: (m, 0)),
                pl.BlockSpec((None, D, tn), lambda n, m: (layer, 0, n))]
    args = [a, w]
    if rope:
        in_specs += [pl.BlockSpec((tm, HEAD_DIM), lambda n, m: (m, 0))] * 3
        args += list(rope_tabs)
    vmem = 2 * tm * D * 2 + 2 * D * tn * 4 + D * tn * 2 + 6 * tm * tn * 4 + 6 * tm * HEAD_DIM * 4
    return pl.pallas_call(
        body,
        out_shape=jax.ShapeDtypeStruct((T, 3 * D), BF16),
        grid=(3 * D // tn, T // tm),
        in_specs=in_specs,
        out_specs=pl.BlockSpec((tm, tn), lambda n, m: (m, n)),
        scratch_shapes=[pltpu.VMEM((D, tn), BF16)],
        compiler_params=_params(("parallel", "arbitrary"), vmem),
        name="qkv_proj",
    )(*args)


def _sb_body(q_ref, k_ref, v_ref, o_ref, acc_ref, carry_ref, *, tq, tk):
    qi = pl.program_id(2)
    ratio = tq // tk
    q = q_ref[...]
    acc_ref[...] = jnp.zeros_like(acc_ref)
    carry_ref[...] = jnp.zeros_like(carry_ref)
    strict_upper = (lax.broadcasted_iota(jnp.int32, (tk, tk), 0)
                    > lax.broadcasted_iota(jnp.int32, (tk, tk), 1)).astype(BF16)

    def block(j, masked):
        start = pl.multiple_of(j * tk, tk)
        k = k_ref[pl.ds(start, tk), :]
        v = v_ref[pl.ds(start, tk), :]
        z = _nt_dot(q, k)
        tail = _softplus_neg_abs(z)
        log_1mb = -jnp.maximum(z, 0.0) - tail
        if masked:
            qpos = qi * tq + lax.broadcasted_iota(jnp.int32, (tq, tk), 0)
            kpos = j * tk + lax.broadcasted_iota(jnp.int32, (tq, tk), 1)
            mask = kpos < qpos
            log_1mb = jnp.where(mask, log_1mb, 0.0)
        hi, lo = _split_bf16(log_1mb, 2)
        suffix = (jnp.dot(hi, strict_upper, preferred_element_type=F32)
                  + jnp.dot(lo, strict_upper, preferred_element_type=F32))
        carry = carry_ref[...]
        w = jnp.exp(jnp.minimum(z, 0.0) - tail + suffix + carry)
        if masked:
            w = jnp.where(mask, w, 0.0)
        acc_ref[...] += jnp.dot(w.astype(BF16), v, preferred_element_type=F32)
        carry_ref[...] = carry + suffix[:, 0:1] + log_1mb[:, 0:1]

    for jj in reversed(range(ratio)):
        block(qi * ratio + jj, True)

    def body(i, c):
        block(qi * ratio - 1 - i, False)
        return c

    lax.fori_loop(0, qi * ratio, body, 0)
    o_ref[...] = acc_ref[...].astype(o_ref.dtype)


def _sb_attention(qkv, B, S, D, n_heads):
    T = B * S
    tq = _tile(S, 512)
    tk = _tile(S, 256)
    nq = S // tq
    hb = D // HEAD_DIM
    body = functools.partial(_sb_body, tq=tq, tk=tk)
    vmem = 4 * S * HEAD_DIM * 2 + 4 * tq * HEAD_DIM * 2 + tq * HEAD_DIM * 4 + tq * 128 * 4 + 12 * tq * tk * 4
    return pl.pallas_call(
        body,
        out_shape=jax.ShapeDtypeStruct((T, n_heads * HEAD_DIM), BF16),
        grid=(B, n_heads, nq),
        in_specs=[pl.BlockSpec((tq, HEAD_DIM), lambda b, h, i: (b * nq + i, h)),
                  pl.BlockSpec((S, HEAD_DIM), lambda b, h, i: (b, hb + h)),
                  pl.BlockSpec((S, HEAD_DIM), lambda b, h, i: (b, 2 * hb + h))],
        out_specs=pl.BlockSpec((tq, HEAD_DIM), lambda b, h, i: (b * nq + i, h)),
        scratch_shapes=[pltpu.VMEM((tq, HEAD_DIM), F32), pltpu.VMEM((tq, 1), F32)],
        compiler_params=_params(("parallel", "parallel", "parallel"), vmem),
        name="stick_breaking_attention",
    )(qkv, qkv, qkv)


def _dilated_body(q_ref, kp_ref, kc_ref, vp_ref, vc_ref, o_ref, lse_ref, *, n_heads, n_back, blk):
    mb = pl.program_id(2)
    row = lax.broadcasted_iota(jnp.int32, (blk, 2 * blk), 0)
    col = lax.broadcasted_iota(jnp.int32, (blk, 2 * blk), 1)
    dist = row + blk - col
    valid = (dist >= 0) & (dist <= n_back) & ((mb > 0) | (col >= blk))
    for hd in range(n_heads):
        sl = slice(hd * HEAD_DIM, (hd + 1) * HEAD_DIM)
        kk = jnp.concatenate([kp_ref[:, sl], kc_ref[:, sl]], axis=0)
        vv = jnp.concatenate([vp_ref[:, sl], vc_ref[:, sl]], axis=0)
        s = jnp.where(valid, _nt_dot(q_ref[:, sl], kk), MASK_VALUE)
        m = jnp.max(s, axis=1, keepdims=True)
        p = jnp.exp(s - m)
        l = jnp.sum(p, axis=1, keepdims=True)
        o_ref[:, sl] = jnp.dot(p.astype(BF16), vv, preferred_element_type=F32) / l
        lse_ref[:, sl] = jnp.broadcast_to(m + jnp.log(l), (blk, HEAD_DIM))


def _dilated_window(qkv, B, S, D, n_heads, window, dil):
    T = B * S
    blk = BAND_BLOCK
    L = S // dil
    width = n_heads * HEAD_DIM
    per = D // width
    first = (D - width) // width
    grouped = qkv.reshape(B, L, dil * 3 * D)

    def spec(section, prev):
        def index(b, r, mb):
            row = jnp.maximum(mb - 1, 0) if prev else mb
            return (b, row, r * 3 * per + section * per + first)
        return pl.BlockSpec((None, blk, width), index)

    out = jax.ShapeDtypeStruct((B, L, dil * width), F32)
    out_spec = pl.BlockSpec((None, blk, width), lambda b, r, mb: (b, mb, r))
    body = functools.partial(_dilated_body, n_heads=n_heads, n_back=window // dil, blk=blk)
    vmem = 10 * blk * width * 2 + 4 * blk * width * 4 + 16 * blk * 2 * blk * 4
    o, lse = pl.pallas_call(
        body,
        out_shape=(out, out),
        grid=(B, dil, L // blk),
        in_specs=[spec(0, False), spec(1, True), spec(1, False), spec(2, True), spec(2, False)],
        out_specs=(out_spec, out_spec),
        compiler_params=_params(("parallel", "parallel", "parallel"), vmem),
        name=f"dilated_window_{dil}",
    )(grouped, grouped, grouped, grouped, grouped)
    return o.reshape(T, width), lse.reshape(T, width)


def _mix_windows_body(*refs):
    o_ref = refs[-1]
    outs, lses = refs[0:-1:2], refs[1:-1:2]
    top = lses[0][...]
    for l_ref in lses[1:]:
        top = jnp.maximum(top, l_ref[...])
    num = jnp.zeros(o_ref.shape, F32)
    den = jnp.zeros(o_ref.shape, F32)
    for out_ref, l_ref in zip(outs, lses):
        e = jnp.exp(l_ref[...] - top)
        num = num + e * out_ref[...]
        den = den + e
    o_ref[...] = (num / den).astype(o_ref.dtype)


def _mix_windows(parts):
    T, width = parts[0][0].shape
    tm = _tile(T, 256)
    flat = [x for pair in parts for x in pair]
    spec = pl.BlockSpec((tm, width), lambda m: (m, 0))
    return pl.pallas_call(
        _mix_windows_body,
        out_shape=jax.ShapeDtypeStruct((T, width), BF16),
        grid=(T // tm,),
        in_specs=[spec] * len(flat),
        out_specs=spec,
        compiler_params=_params(("parallel",), (2 * len(flat) + 8) * tm * width * 4),
        name="mix_windows",
    )(*flat)


def _forget_body(a_ref, w_ref, wt_ref, b_ref, bt_ref, col_ref, row_ref, carry_col_ref, carry_row_ref, *, ts):
    @pl.when(pl.program_id(1) == 0)
    def _():
        carry_col_ref[...] = jnp.zeros_like(carry_col_ref)
        carry_row_ref[...] = jnp.zeros_like(carry_row_ref)

    a = a_ref[...]
    r = lax.broadcasted_iota(jnp.int32, (ts, ts), 0)
    c = lax.broadcasted_iota(jnp.int32, (ts, ts), 1)

    def log_sigmoid(x):
        return jnp.minimum(x, 0.0) - _softplus_neg_abs(x)

    logf_col = log_sigmoid(jnp.dot(a, w_ref[...].astype(BF16), preferred_element_type=F32) + b_ref[...])
    lower = (r >= c).astype(BF16)
    cum_col = carry_col_ref[...]
    for part in _split_bf16(logf_col, 3):
        cum_col = cum_col + jnp.dot(lower, part, preferred_element_type=F32)
    col_ref[...] = cum_col
    carry_col_ref[...] = cum_col[ts - 1:ts, :]

    logf_row = log_sigmoid(_nt_dot(wt_ref[...].astype(BF16), a) + bt_ref[...])
    upper = (r <= c).astype(BF16)
    cum_row = carry_row_ref[...]
    for part in _split_bf16(logf_row, 3):
        cum_row = cum_row + jnp.dot(part, upper, preferred_element_type=F32)
    row_ref[...] = cum_row
    carry_row_ref[...] = cum_row[:, ts - 1:ts]


def _forget_cumsum(a, w_f, b_f, B, S):
    T, D = a.shape
    H = w_f.shape[1]
    ts = _tile(S, 512)
    ns = S // ts
    body = functools.partial(_forget_body, ts=ts)
    vmem = 4 * ts * D * 2 + 8 * D * 128 * 4 + 12 * ts * ts * 4
    col, row = pl.pallas_call(
        body,
        out_shape=(jax.ShapeDtypeStruct((T, H), F32), jax.ShapeDtypeStruct((B, H, S), F32)),
        grid=(B, ns),
        in_specs=[pl.BlockSpec((ts, D), lambda b, s: (b * ns + s, 0)),
                  pl.BlockSpec((D, H), lambda b, s: (0, 0)),
                  pl.BlockSpec((H, D), lambda b, s: (0, 0)),
                  pl.BlockSpec((1, H), lambda b, s: (0, 0)),
                  pl.BlockSpec((H, 1), lambda b, s: (0, 0))],
        out_specs=(pl.BlockSpec((ts, H), lambda b, s: (b * ns + s, 0)),
                   pl.BlockSpec((None, H, ts), lambda b, s: (b, 0, s))),
        scratch_shapes=[pltpu.VMEM((1, H), F32), pltpu.VMEM((H, 1), F32)],
        compiler_params=_params(("parallel", "arbitrary"), vmem),
        name="forget_cumsum",
    )(a, w_f, w_f.T, b_f.reshape(1, H), b_f.reshape(H, 1))
    return col, row.reshape(B * H, 1, S)


def _fox_body(q_ref, k_ref, v_ref, ccol_ref, crow_ref, o_ref, acc_ref, m_ref, l_ref, *, tq, tk):
    h = pl.program_id(1)
    qi = pl.program_id(2)
    ratio = tq // tk
    q = q_ref[...]
    ccol = ccol_ref[...]
    head = lax.broadcasted_iota(jnp.int32, ccol.shape, 1)
    cq = jnp.sum(jnp.where(head == h, ccol, 0.0), axis=1, keepdims=True)
    acc_ref[...] = jnp.zeros_like(acc_ref)
    l_ref[...] = jnp.zeros_like(l_ref)
    m_ref[...] = jnp.full_like(m_ref, MASK_VALUE)

    def block(j, masked):
        start = pl.multiple_of(j * tk, tk)
        k = k_ref[pl.ds(start, tk), :]
        v = v_ref[pl.ds(start, tk), :]
        sk = _nt_dot(q, k) - crow_ref[:, pl.ds(start, tk)]
        if masked:
            qpos = qi * tq + lax.broadcasted_iota(jnp.int32, (tq, tk), 0)
            kpos = j * tk + lax.broadcasted_iota(jnp.int32, (tq, tk), 1)
            sk = jnp.where(kpos <= qpos, sk, MASK_VALUE)
        m_old = m_ref[...]
        m_new = jnp.maximum(m_old, jnp.max(sk, axis=1, keepdims=True) + cq)
        p = jnp.exp(sk + (cq - m_new))
        alpha = jnp.exp(m_old - m_new)
        l_ref[...] = alpha * l_ref[...] + jnp.sum(p, axis=1, keepdims=True)
        acc_ref[...] = alpha * acc_ref[...] + jnp.dot(p.astype(BF16), v, preferred_element_type=F32)
        m_ref[...] = m_new

    def body(j, c):
        block(j, False)
        return c

    lax.fori_loop(0, qi * ratio, body, 0)
    for jj in range(ratio):
        block(qi * ratio + jj, True)
    o_ref[...] = (acc_ref[...] / l_ref[...]).astype(o_ref.dtype)


def _fox_attention(qkv, c_col, c_row, B, S, D):
    T = B * S
    H = D // HEAD_DIM
    tq = _tile(S, 512)
    tk = _tile(S, 512)
    nq = S // tq
    body = functools.partial(_fox_body, tq=tq, tk=tk)
    vmem = (4 * S * HEAD_DIM * 2 + 4 * tq * HEAD_DIM * 2 + 2 * tq * 128 * 4 + 2 * SUBLANES * S * 4
            + tq * HEAD_DIM * 4 + 2 * tq * 128 * 4 + 10 * tq * tk * 4)
    return pl.pallas_call(
        body,
        out_shape=jax.ShapeDtypeStruct((T, D), BF16),
        grid=(B, H, nq),
        in_specs=[pl.BlockSpec((tq, HEAD_DIM), lambda b, h, i: (b * nq + i, h)),
                  pl.BlockSpec((S, HEAD_DIM), lambda b, h, i: (b, H + h)),
                  pl.BlockSpec((S, HEAD_DIM), lambda b, h, i: (b, 2 * H + h)),
                  pl.BlockSpec((tq, H), lambda b, h, i: (b * nq + i, 0)),
                  pl.BlockSpec((None, 1, S), lambda b, h, i: (b * H + h, 0, 0))],
        out_specs=pl.BlockSpec((tq, HEAD_DIM), lambda b, h, i: (b * nq + i, h)),
        scratch_shapes=[pltpu.VMEM((tq, HEAD_DIM), F32), pltpu.VMEM((tq, 1), F32), pltpu.VMEM((tq, 1), F32)],
        compiler_params=_params(("parallel", "parallel", "parallel"), vmem),
        name="forgetting_attention",
    )(qkv, qkv, qkv, c_col, c_row)


def _residual_mm_body(*refs, n_parts):
    a_refs = refs[:n_parts]
    w_refs = refs[n_parts:2 * n_parts]
    h_ref, o_ref = refs[2 * n_parts], refs[2 * n_parts + 1]
    wbf_refs = refs[2 * n_parts + 2:]

    @pl.when(pl.program_id(1) == 0)
    def _():
        for w_ref, wbf_ref in zip(w_refs, wbf_refs):
            wbf_ref[...] = w_ref[...].astype(BF16)

    acc = h_ref[...]
    for a_ref, wbf_ref in zip(a_refs, wbf_refs):
        acc = acc + jnp.dot(a_ref[...], wbf_ref[...], preferred_element_type=F32)
    o_ref[...] = acc


def _residual_matmul(h, parts, w, layer, tm_pref=512):
    T, N = h.shape
    n_parts = len(parts)
    kp = parts[0].shape[1]
    tm = _tile(T, tm_pref)
    tn = _tile(N, 512)
    body = functools.partial(_residual_mm_body, n_parts=n_parts)
    a_specs = [pl.BlockSpec((tm, kp), lambda n, m: (m, 0)) for _ in parts]
    w_specs = [pl.BlockSpec((None, kp, tn), lambda n, m, i=i: (layer, i, n), pipeline_mode=pl.Buffered(1))
               for i in range(n_parts)]
    hspec = pl.BlockSpec((tm, tn), lambda n, m: (m, n))
    vmem = n_parts * (2 * tm * kp * 2 + kp * tn * 4 + kp * tn * 2) + 8 * tm * tn * 4
    return pl.pallas_call(
        body,
        out_shape=jax.ShapeDtypeStruct((T, N), F32),
        grid=(N // tn, T // tm),
        in_specs=a_specs + w_specs + [hspec],
        out_specs=hspec,
        scratch_shapes=[pltpu.VMEM((kp, tn), BF16) for _ in parts],
        compiler_params=_params(("parallel", "arbitrary"), vmem),
        name="residual_matmul",
    )(*parts, *([w] * n_parts), h)


def _ffn_up_body(a_ref, wg_ref, wv_ref, cwg_ref, cwv_ref, cbg_ref, cbv_ref, o_ref,
                 wg_bf_ref, wv_bf_ref, tail_g_ref, tail_v_ref, *, tm, tiles_per_seq):
    m = pl.program_id(1)

    @pl.when(m == 0)
    def _():
        wg_bf_ref[...] = wg_ref[...].astype(BF16)
        wv_bf_ref[...] = wv_ref[...].astype(BF16)

    @pl.when(m % tiles_per_seq == 0)
    def _():
        tail_g_ref[...] = jnp.zeros_like(tail_g_ref)
        tail_v_ref[...] = jnp.zeros_like(tail_v_ref)

    a = a_ref[...]

    def conv(wbf_ref, cw_ref, cb_ref, tail_ref):
        u = jnp.dot(a, wbf_ref[...], preferred_element_type=F32)
        row = lax.broadcasted_iota(jnp.int32, u.shape, 0)
        cw = cw_ref[...]
        out = u * cw[CONV_WIDTH - 1:CONV_WIDTH, :] + cb_ref[...]
        for back in range(1, CONV_WIDTH):
            shifted = pltpu.roll(u, back, 0)
            for i in range(back):
                prev_row = tail_ref[SUBLANES - back + i:SUBLANES - back + i + 1, :]
                shifted = jnp.where(row == i, prev_row, shifted)
            out = out + shifted * cw[CONV_WIDTH - 1 - back:CONV_WIDTH - back, :]
        tail_ref[...] = u[tm - SUBLANES:, :]
        return out

    gate = conv(wg_bf_ref, cwg_ref, cbg_ref, tail_g_ref)
    val = conv(wv_bf_ref, cwv_ref, cbv_ref, tail_v_ref)
    o_ref[...] = (gate / (1.0 + jnp.exp(-gate)) * val).astype(o_ref.dtype)


def _ffn_up(a, w_up, conv_w, conv_b, layer, S):
    T, D = a.shape
    F = w_up.shape[2] // 2
    tm = _tile(S, 512)
    tn = _tile(F, 512)
    nf = F // tn
    body = functools.partial(_ffn_up_body, tm=tm, tiles_per_seq=S // tm)
    conv_b = conv_b.reshape(conv_b.shape[0], 1, 2 * F)
    vmem = 2 * tm * D * 2 + 2 * (D * tn * 4 + D * tn * 2) + 16 * tm * tn * 4
    return pl.pallas_call(
        body,
        out_shape=jax.ShapeDtypeStruct((T, F), BF16),
        grid=(nf, T // tm),
        in_specs=[pl.BlockSpec((tm, D), lambda n, m: (m, 0)),
                  pl.BlockSpec((None, D, tn), lambda n, m: (layer, 0, n), pipeline_mode=pl.Buffered(1)),
                  pl.BlockSpec((None, D, tn), lambda n, m: (layer, 0, nf + n), pipeline_mode=pl.Buffered(1)),
                  pl.BlockSpec((None, CONV_WIDTH, tn), lambda n, m: (layer, 0, n)),
                  pl.BlockSpec((None, CONV_WIDTH, tn), lambda n, m: (layer, 0, nf + n)),
                  pl.BlockSpec((None, 1, tn), lambda n, m: (layer, 0, n)),
                  pl.BlockSpec((None, 1, tn), lambda n, m: (layer, 0, nf + n))],
        out_specs=pl.BlockSpec((tm, tn), lambda n, m: (m, n)),
        scratch_shapes=[pltpu.VMEM((D, tn), BF16), pltpu.VMEM((D, tn), BF16),
                        pltpu.VMEM((SUBLANES, tn), F32), pltpu.VMEM((SUBLANES, tn), F32)],
        compiler_params=_params(("parallel", "arbitrary"), vmem),
        name="ffn_up_conv_gate",
    )(a, w_up, w_up, conv_w, conv_w, conv_b, conv_b)


def _ple_body(a_ref, wg_ref, p_ref, wp_ref, h_ref, o_ref, wg_bf_ref, wp_bf_ref):
    @pl.when(pl.program_id(1) == 0)
    def _():
        wg_bf_ref[...] = wg_ref[...].astype(BF16)
        wp_bf_ref[...] = wp_ref[...].astype(BF16)

    logits = jnp.dot(a_ref[...], wg_bf_ref[...], preferred_element_type=F32)
    emb = jnp.dot(p_ref[...].astype(BF16), wp_bf_ref[...], preferred_element_type=F32)
    o_ref[...] = h_ref[...] + emb / (1.0 + jnp.exp(-logits))


def _ple(h, a, w_gate, p, w_ple, layer):
    T, D = h.shape
    P = p.shape[2]
    tm = _tile(T, 512)
    tn = _tile(D, 512)
    hspec = pl.BlockSpec((tm, tn), lambda n, m: (m, n))
    vmem = 2 * tm * D * 2 + D * tn * 6 + 2 * tm * P * 4 + P * tn * 6 + 10 * tm * tn * 4
    return pl.pallas_call(
        _ple_body,
        out_shape=jax.ShapeDtypeStruct((T, D), F32),
        grid=(D // tn, T // tm),
        in_specs=[pl.BlockSpec((tm, D), lambda n, m: (m, 0)),
                  pl.BlockSpec((None, D, tn), lambda n, m: (layer, 0, n), pipeline_mode=pl.Buffered(1)),
                  pl.BlockSpec((None, tm, P), lambda n, m: (layer, m, 0)),
                  pl.BlockSpec((None, P, tn), lambda n, m: (layer, 0, n), pipeline_mode=pl.Buffered(1)),
                  hspec],
        out_specs=hspec,
        scratch_shapes=[pltpu.VMEM((D, tn), BF16), pltpu.VMEM((P, tn), BF16)],
        compiler_params=_params(("parallel", "arbitrary"), vmem),
        name="gated_layer_embedding",
    )(a, w_gate, p, w_ple, h)


def kernel(x, p, positions, attn_norm, w_in_even, w_out_even, w_in_odd, b_forget, w_out_odd, ffn_norm,
           w_up, conv_w, conv_b, w_down, ple_norm, w_ple_gate, w_ple, final_norm):
    B, S, D = x.shape
    T = B * S
    depth = p.shape[0]
    H = D // HEAD_DIM
    n_sb = H // 2
    n_dil = H - n_sb
    h = x.reshape(T, D)
    p = p.reshape(depth, T, p.shape[-1])
    rope_tabs = _rope_tables(positions)
    for i in range(depth):
        a = _rmsnorm(h, attn_norm[i], BF16)
        if i % 2 == 0:
            qkv = _qkv_proj(a, w_in_even, i // 2, rope_tabs, n_sb)
            sb = _sb_attention(qkv, B, S, D, n_sb)
            windows = [_dilated_window(qkv, B, S, D, n_dil, window, dil) for window, dil in DILATED_CONFIGS]
            h = _residual_matmul(h, [sb, _mix_windows(windows)], w_out_even, i // 2)
        else:
            qkv = _qkv_proj(a, w_in_odd, i // 2, None, H)
            c_col, c_row = _forget_cumsum(a, w_in_odd[i // 2, :, 3 * D:], b_forget[i // 2], B, S)
            o = _fox_attention(qkv, c_col, c_row, B, S, D)
            h = _residual_matmul(h, [o], w_out_odd, i // 2)
        act = _ffn_up(_rmsnorm(h, ffn_norm[i], BF16), w_up, conv_w, conv_b, i, S)
        h = _residual_matmul(h, [act], w_down, i, tm_pref=256)
        h = _ple(h, _rmsnorm(h, ple_norm[i], BF16), w_ple_gate, p, w_ple, i)
    return _rmsnorm(h, final_norm, F32).reshape(B, S, D)
```

```python
import functools

import jax
import jax.numpy as jnp
from jax import lax
from jax.experimental import pallas as pl
from jax.experimental.pallas import tpu as pltpu

HEAD_DIM = 128
ROPE_DIM = HEAD_DIM // 4
ROPE_THETA = 500000.0
DILATED_CONFIGS = ((128, 1), (512, 4), (2048, 16))
BAND_BLOCK = 128
RMS_EPS = 1e-6
CONV_WIDTH = 3
MASK_VALUE = -1e30
SUBLANES = 8
VMEM_CAP_BYTES = 60 * 1024 * 1024

F32 = jnp.float32
BF16 = jnp.bfloat16


def _params(semantics, vmem_bytes):
    return pltpu.CompilerParams(dimension_semantics=semantics,
                                vmem_limit_bytes=int(min(vmem_bytes, VMEM_CAP_BYTES)))


def _tile(dim, pref):
    t = min(pref, dim)
    while dim % t:
        t //= 2
    return t


def _nt_dot(a, b):
    return lax.dot_general(a, b, (((1,), (1,)), ((), ())), preferred_element_type=F32)


def _tn_dot(a, b):
    return lax.dot_general(a, b, (((0,), (0,)), ((), ())), preferred_element_type=F32)


def _softplus_neg_abs(z):
    return jnp.log1p(jnp.exp(-jnp.abs(z)))


def _split_bf16(x, parts):
    out = []
    for _ in range(parts - 1):
        hi = x.astype(BF16)
        out.append(hi)
        x = x - hi.astype(F32)
    out.append(x.astype(BF16))
    return out


def _rmsnorm_body(h_ref, g_ref, o_ref):
    x = h_ref[...]
    ms = jnp.mean(x * x, axis=-1, keepdims=True)
    o_ref[...] = (x * lax.rsqrt(ms + RMS_EPS) * g_ref[...]).astype(o_ref.dtype)


def _rmsnorm(h, g, out_dtype):
    T, D = h.shape
    tm = _tile(T, 512)
    blk = tm * D * 4
    return pl.pallas_call(
        _rmsnorm_body,
        out_shape=jax.ShapeDtypeStruct((T, D), out_dtype),
        grid=(T // tm,),
        in_specs=[pl.BlockSpec((tm, D), lambda m: (m, 0)),
                  pl.BlockSpec((1, D), lambda m: (0, 0))],
        out_specs=pl.BlockSpec((tm, D), lambda m: (m, 0)),
        compiler_params=_params(("parallel",), 6 * blk),
        name="rmsnorm",
    )(h, g.reshape(1, D))


def _rope_table_body(pos_ref, invf_ref, cos_ref, sa_ref, sb_ref):
    ang = pos_ref[...].astype(F32) * invf_ref[...]
    lane = lax.broadcasted_iota(jnp.int32, ang.shape, 1)
    half = ROPE_DIM // 2
    s = jnp.sin(ang)
    cos_ref[...] = jnp.cos(ang)
    sa_ref[...] = jnp.where((lane >= half) & (lane < ROPE_DIM), s, 0.0)
    sb_ref[...] = jnp.where(lane < half, -s, 0.0)


def _rope_tables(positions):
    T = positions.size
    tm = _tile(T, 512)
    half = ROPE_DIM // 2
    inv_freq = ROPE_THETA ** (-jnp.arange(0, ROPE_DIM, 2, dtype=F32) / ROPE_DIM)
    invf = jnp.concatenate([inv_freq, inv_freq, jnp.zeros((HEAD_DIM - 2 * half,), F32)]).reshape(1, HEAD_DIM)
    tab = jax.ShapeDtypeStruct((T, HEAD_DIM), F32)
    spec = pl.BlockSpec((tm, HEAD_DIM), lambda m: (m, 0))
    return pl.pallas_call(
        _rope_table_body,
        out_shape=(tab, tab, tab),
        grid=(T // tm,),
        in_specs=[pl.BlockSpec((tm, 1), lambda m: (m, 0)),
                  pl.BlockSpec((1, HEAD_DIM), lambda m: (0, 0))],
        out_specs=(spec, spec, spec),
        compiler_params=_params(("parallel",), 32 * tm * HEAD_DIM * 4),
        name="rope_tables",
    )(positions.reshape(T, 1), invf)


def _qkv_body(*refs, rope, scale, tiles_per_section, heads_per_tile, first_rope_head):
    if rope:
        a_ref, w_ref, cos_ref, sa_ref, sb_ref, o_ref, wbf_ref = refs
    else:
        a_ref, w_ref, o_ref, wbf_ref = refs
    n = pl.program_id(0)

    @pl.when(pl.program_id(1) == 0)
    def _():
        wbf_ref[...] = w_ref[...].astype(BF16)

    acc = jnp.dot(a_ref[...], wbf_ref[...], preferred_element_type=F32)
    acc = acc * jnp.where(n < tiles_per_section, scale, 1.0).astype(F32)
    if not rope:
        o_ref[...] = acc.astype(o_ref.dtype)
        return

    section = n // tiles_per_section
    first_head = (n % tiles_per_section) * heads_per_tile
    is_rope = jnp.logical_and(section < 2, first_head >= first_rope_head)

    @pl.when(is_rope)
    def _():
        c, sa, sb = cos_ref[...], sa_ref[...], sb_ref[...]
        half = ROPE_DIM // 2
        for j in range(heads_per_tile):
            x = acc[:, j * HEAD_DIM:(j + 1) * HEAD_DIM]
            y = x * c + pltpu.roll(x, half, 1) * sa + pltpu.roll(x, HEAD_DIM - half, 1) * sb
            o_ref[:, j * HEAD_DIM:(j + 1) * HEAD_DIM] = y.astype(o_ref.dtype)

    @pl.when(jnp.logical_not(is_rope))
    def _():
        o_ref[...] = acc.astype(o_ref.dtype)


def _qkv_proj(a, w, layer, rope_tabs, first_rope_head):
    T, D = a.shape
    rope = rope_tabs is not None
    tm = _tile(T, 512)
    tn = _tile(D, 512)
    if rope:
        tn = min(tn, first_rope_head * HEAD_DIM)
    body = functools.partial(
        _qkv_body, rope=rope, scale=HEAD_DIM ** -0.5, tiles_per_section=D // tn,
        heads_per_tile=tn // HEAD_DIM, first_rope_head=first_rope_head)
    in_specs = [pl.BlockSpec((tm, D), lambda n, m: (m, 0)),
                pl.BlockSpec((None, D, tn), lambda n, m: (layer, 0, n))]
    args = [a, w]
    if rope:
        in_specs += [pl.BlockSpec((tm, HEAD_DIM), lambda n, m: (m, 0))] * 3
        args += list(rope_tabs)
    vmem = 2 * tm * D * 2 + 2 * D * tn * 4 + D * tn * 2 + 6 * tm * tn * 4 + 6 * tm * HEAD_DIM * 4
    return pl.pallas_call(
        body,
        out_shape=jax.ShapeDtypeStruct((T, 3 * D), BF16),
        grid=(3 * D // tn, T // tm),
        in_specs=in_specs,
        out_specs=pl.BlockSpec((tm, tn), lambda n, m: (m, n)),
        scratch_shapes=[pltpu.VMEM((D, tn), BF16)],
        compiler_params=_params(("parallel", "arbitrary"), vmem),
        name="qkv_proj",
    )(*args)


def _sb_body(q_ref, k_ref, v_ref, o_ref, acc_ref, carry_ref, *, tq, tk):
    qi = pl.program_id(2)
    ratio = tq // tk
    q = q_ref[...]
    acc_ref[...] = jnp.zeros_like(acc_ref)
    carry_ref[...] = jnp.zeros_like(carry_ref)
    strict_upper = (lax.broadcasted_iota(jnp.int32, (tk, tk), 1)
                    > lax.broadcasted_iota(jnp.int32, (tk, tk), 0)).astype(BF16)

    def block(j, masked):
        start = pl.multiple_of(j * tk, tk)
        k = k_ref[pl.ds(start, tk), :]
        v = v_ref[pl.ds(start, tk), :]
        z = _nt_dot(k, q)
        tail = _softplus_neg_abs(z)
        log_1mb = -jnp.maximum(z, 0.0) - tail
        if masked:
            kpos = j * tk + lax.broadcasted_iota(jnp.int32, (tk, tq), 0)
            qpos = qi * tq + lax.broadcasted_iota(jnp.int32, (tk, tq), 1)
            mask = kpos < qpos
            log_1mb = jnp.where(mask, log_1mb, 0.0)
        hi, lo = _split_bf16(log_1mb, 2)
        suffix = (jnp.dot(strict_upper, hi, preferred_element_type=F32)
                  + jnp.dot(strict_upper, lo, preferred_element_type=F32))
        carry = carry_ref[...]
        w = jnp.exp(jnp.minimum(z, 0.0) - tail + suffix + carry)
        if masked:
            w = jnp.where(mask, w, 0.0)
        acc_ref[...] += _tn_dot(v, w.astype(BF16))
        carry_ref[...] = carry + suffix[0:1, :] + log_1mb[0:1, :]

    for jj in reversed(range(ratio)):
        block(qi * ratio + jj, True)

    def body(i, c):
        block(qi * ratio - 1 - i, False)
        return c

    lax.fori_loop(0, qi * ratio, body, 0)
    o_ref[...] = acc_ref[...].T.astype(o_ref.dtype)


def _sb_attention(qkv, B, S, D, n_heads):
    T = B * S
    tq = _tile(S, 512)
    tk = _tile(S, 256)
    nq = S // tq
    hb = D // HEAD_DIM
    body = functools.partial(_sb_body, tq=tq, tk=tk)
    vmem = 4 * S * HEAD_DIM * 2 + 4 * tq * HEAD_DIM * 2 + tq * HEAD_DIM * 4 + tq * 128 * 4 + 12 * tq * tk * 4
    return pl.pallas_call(
        body,
        out_shape=jax.ShapeDtypeStruct((T, n_heads * HEAD_DIM), BF16),
        grid=(B, n_heads, nq),
        in_specs=[pl.BlockSpec((tq, HEAD_DIM), lambda b, h, i: (b * nq + i, h)),
                  pl.BlockSpec((S, HEAD_DIM), lambda b, h, i: (b, hb + h)),
                  pl.BlockSpec((S, HEAD_DIM), lambda b, h, i: (b, 2 * hb + h))],
        out_specs=pl.BlockSpec((tq, HEAD_DIM), lambda b, h, i: (b * nq + i, h)),
        scratch_shapes=[pltpu.VMEM((HEAD_DIM, tq), F32), pltpu.VMEM((1, tq), F32)],
        compiler_params=_params(("parallel", "parallel", "parallel"), vmem),
        name="stick_breaking_attention",
    )(qkv, qkv, qkv)


def _dilated_body(q_ref, kp_ref, kc_ref, vp_ref, vc_ref, o_ref, lse_ref, *, n_heads, n_back, blk):
    mb = pl.program_id(2)
    row = lax.broadcasted_iota(jnp.int32, (blk, 2 * blk), 0)
    col = lax.broadcasted_iota(jnp.int32, (blk, 2 * blk), 1)
    dist = row + blk - col
    valid = (dist >= 0) & (dist <= n_back) & ((mb > 0) | (col >= blk))
    for hd in range(n_heads):
        sl = slice(hd * HEAD_DIM, (hd + 1) * HEAD_DIM)
        kk = jnp.concatenate([kp_ref[:, sl], kc_ref[:, sl]], axis=0)
        vv = jnp.concatenate([vp_ref[:, sl], vc_ref[:, sl]], axis=0)
        s = jnp.where(valid, _nt_dot(q_ref[:, sl], kk), MASK_VALUE)
        m = jnp.max(s, axis=1, keepdims=True)
        p = jnp.exp(s - m)
        l = jnp.sum(p, axis=1, keepdims=True)
        o_ref[:, sl] = jnp.dot(p.astype(BF16), vv, preferred_element_type=F32) / l
        lse_ref[:, sl] = jnp.broadcast_to(m + jnp.log(l), (blk, HEAD_DIM))


def _dilated_window(qkv, B, S, D, n_heads, window, dil):
    T = B * S
    blk = BAND_BLOCK
    L = S // dil
    width = n_heads * HEAD_DIM
    per = D // width
    first = (D - width) // width
    grouped = qkv.reshape(B, L, dil * 3 * D)

    def spec(section, prev):
        def index(b, r, mb):
            row = jnp.maximum(mb - 1, 0) if prev else mb
            return (b, row, r * 3 * per + section * per + first)
        return pl.BlockSpec((None, blk, width), index)

    out = jax.ShapeDtypeStruct((B, L, dil * width), F32)
    out_spec = pl.BlockSpec((None, blk, width), lambda b, r, mb: (b, mb, r))
    body = functools.partial(_dilated_body, n_heads=n_heads, n_back=window // dil, blk=blk)
    vmem = 10 * blk * width * 2 + 4 * blk * width * 4 + 16 * blk * 2 * blk * 4
    o, lse = pl.pallas_call(
        body,
        out_shape=(out, out),
        grid=(B, dil, L // blk),
        in_specs=[spec(0, False), spec(1, True), spec(1, False), spec(2, True), spec(2, False)],
        out_specs=(out_spec, out_spec),
        compiler_params=_params(("parallel", "parallel", "parallel"), vmem),
        name=f"dilated_window_{dil}",
    )(grouped, grouped, grouped, grouped, grouped)
    return o.reshape(T, width), lse.reshape(T, width)


def _mix_windows_body(*refs):
    o_ref = refs[-1]
    outs, lses = refs[0:-1:2], refs[1:-1:2]
    top = lses[0][...]
    for l_ref in lses[1:]:
        top = jnp.maximum(top, l_ref[...])
    num = jnp.zeros(o_ref.shape, F32)
    den = jnp.zeros(o_ref.shape, F32)
    for out_ref, l_ref in zip(outs, lses):
        e = jnp.exp(l_ref[...] - top)
        num = num + e * out_ref[...]
        den = den + e
    o_ref[...] = (num / den).astype(o_ref.dtype)


def _mix_windows(parts):
    T, width = parts[0][0].shape
    tm = _tile(T, 256)
    flat = [x for pair in parts for x in pair]
    spec = pl.BlockSpec((tm, width), lambda m: (m, 0))
    return pl.pallas_call(
        _mix_windows_body,
        out_shape=jax.ShapeDtypeStruct((T, width), BF16),
        grid=(T // tm,),
        in_specs=[spec] * len(flat),
        out_specs=spec,
        compiler_params=_params(("parallel",), (2 * len(flat) + 8) * tm * width * 4),
        name="mix_windows",
    )(*flat)


def _forget_body(a_ref, w_ref, wt_ref, b_ref, bt_ref, col_ref, row_ref, carry_col_ref, carry_row_ref, *, ts):
    @pl.when(pl.program_id(1) == 0)
    def _():
        carry_col_ref[...] = jnp.zeros_like(carry_col_ref)
        carry_row_ref[...] = jnp.zeros_like(carry_row_ref)

    a = a_ref[...]
    r = lax.broadcasted_iota(jnp.int32, (ts, ts), 0)
    c = lax.broadcasted_iota(jnp.int32, (ts, ts), 1)

    def log_sigmoid(x):
        return jnp.minimum(x, 0.0) - _softplus_neg_abs(x)

    logf_col = log_sigmoid(jnp.dot(a, w_ref[...].astype(BF16), preferred_element_type=F32) + b_ref[...])
    lower = (r >= c).astype(BF16)
    cum_col = carry_col_ref[...]
    for part in _split_bf16(logf_col, 3):
        cum_col = cum_col + jnp.dot(lower, part, preferred_element_type=F32)
    col_ref[...] = cum_col
    carry_col_ref[...] = cum_col[ts - 1:ts, :]

    logf_row = log_sigmoid(_nt_dot(wt_ref[...].astype(BF16), a) + bt_ref[...])
    upper = (r <= c).astype(BF16)
    cum_row = carry_row_ref[...]
    for part in _split_bf16(logf_row, 3):
        cum_row = cum_row + jnp.dot(part, upper, preferred_element_type=F32)
    row_ref[...] = cum_row
    carry_row_ref[...] = cum_row[:, ts - 1:ts]


def _forget_cumsum(a, w_f, b_f, B, S):
    T, D = a.shape
    H = w_f.shape[1]
    ts = _tile(S, 512)
    ns = S // ts
    body = functools.partial(_forget_body, ts=ts)
    vmem = 4 * ts * D * 2 + 8 * D * 128 * 4 + 12 * ts * ts * 4
    col, row = pl.pallas_call(
        body,
        out_shape=(jax.ShapeDtypeStruct((T, H), F32), jax.ShapeDtypeStruct((B, H, S), F32)),
        grid=(B, ns),
        in_specs=[pl.BlockSpec((ts, D), lambda b, s: (b * ns + s, 0)),
                  pl.BlockSpec((D, H), lambda b, s: (0, 0)),
                  pl.BlockSpec((H, D), lambda b, s: (0, 0)),
                  pl.BlockSpec((1, H), lambda b, s: (0, 0)),
                  pl.BlockSpec((H, 1), lambda b, s: (0, 0))],
        out_specs=(pl.BlockSpec((ts, H), lambda b, s: (b * ns + s, 0)),
                   pl.BlockSpec((None, H, ts), lambda b, s: (b, 0, s))),
        scratch_shapes=[pltpu.VMEM((1, H), F32), pltpu.VMEM((H, 1), F32)],
        compiler_params=_params(("parallel", "arbitrary"), vmem),
        name="forget_cumsum",
    )(a, w_f, w_f.T, b_f.reshape(1, H), b_f.reshape(H, 1))
    return col, row.reshape(B * H, 1, S)


def _fox_body(q_ref, k_ref, v_ref, ccol_ref, crow_ref, o_ref, acc_ref, m_ref, l_ref, ck_ref, *, tq, tk):
    h = pl.program_id(1)
    qi = pl.program_id(2)
    ratio = tq // tk
    lane_groups = tq // HEAD_DIM

    @pl.when(qi == 0)
    def _():
        ccol = ccol_ref[...]
        onehot = (lax.broadcasted_iota(jnp.int32, (ccol.shape[1], HEAD_DIM), 0) == h).astype(BF16)
        ck = jnp.zeros(ck_ref.shape, F32)
        for part in _split_bf16(ccol, 3):
            ck = ck + jnp.dot(part, onehot, preferred_element_type=F32)
        ck_ref[...] = ck

    q = q_ref[...]
    cq = crow_ref[:, pl.ds(pl.multiple_of(qi * tq, tq), tq)]
    acc_ref[...] = jnp.zeros_like(acc_ref)
    l_ref[...] = jnp.zeros_like(l_ref)
    m_ref[...] = jnp.full_like(m_ref, MASK_VALUE)

    def block(j, masked):
        start = pl.multiple_of(j * tk, tk)
        k = k_ref[pl.ds(start, tk), :]
        v = v_ref[pl.ds(start, tk), :]
        ck = ck_ref[pl.ds(start, tk), :]
        sk = _nt_dot(k, q) - jnp.concatenate([ck] * lane_groups, axis=1)
        if masked:
            kpos = j * tk + lax.broadcasted_iota(jnp.int32, (tk, tq), 0)
            qpos = qi * tq + lax.broadcasted_iota(jnp.int32, (tk, tq), 1)
            sk = jnp.where(kpos <= qpos, sk, MASK_VALUE)
        m_old = m_ref[...]
        m_new = jnp.maximum(m_old, jnp.max(sk, axis=0, keepdims=True) + cq)
        p = jnp.exp(sk + (cq - m_new))
        alpha = jnp.exp(m_old - m_new)
        l_ref[...] = alpha * l_ref[...] + jnp.sum(p, axis=0, keepdims=True)
        acc_ref[...] = alpha * acc_ref[...] + _tn_dot(v, p.astype(BF16))
        m_ref[...] = m_new

    def body(j, c):
        block(j, False)
        return c

    lax.fori_loop(0, qi * ratio, body, 0)
    for jj in range(ratio):
        block(qi * ratio + jj, True)
    o_ref[...] = (acc_ref[...] / l_ref[...]).T.astype(o_ref.dtype)


def _fox_attention(qkv, c_col, c_row, B, S, D):
    T = B * S
    H = D // HEAD_DIM
    tq = _tile(S, 512)
    tk = _tile(S, 512)
    nq = S // tq
    body = functools.partial(_fox_body, tq=tq, tk=tk)
    vmem = (4 * S * HEAD_DIM * 2 + 4 * tq * HEAD_DIM * 2 + 2 * S * 128 * 4 + 2 * SUBLANES * S * 4
            + S * HEAD_DIM * 4 + tq * HEAD_DIM * 4 + 2 * SUBLANES * tq * 4 + 10 * tq * tk * 4)
    return pl.pallas_call(
        body,
        out_shape=jax.ShapeDtypeStruct((T, D), BF16),
        grid=(B, H, nq),
        in_specs=[pl.BlockSpec((tq, HEAD_DIM), lambda b, h, i: (b * nq + i, h)),
                  pl.BlockSpec((S, HEAD_DIM), lambda b, h, i: (b, H + h)),
                  pl.BlockSpec((S, HEAD_DIM), lambda b, h, i: (b, 2 * H + h)),
                  pl.BlockSpec((S, H), lambda b, h, i: (b, 0)),
                  pl.BlockSpec((None, 1, S), lambda b, h, i: (b * H + h, 0, 0))],
        out_specs=pl.BlockSpec((tq, HEAD_DIM), lambda b, h, i: (b * nq + i, h)),
        scratch_shapes=[pltpu.VMEM((HEAD_DIM, tq), F32), pltpu.VMEM((1, tq), F32), pltpu.VMEM((1, tq), F32),
                        pltpu.VMEM((S, HEAD_DIM), F32)],
        compiler_params=_params(("parallel", "parallel", "arbitrary"), vmem),
        name="forgetting_attention",
    )(qkv, qkv, qkv, c_col, c_row)


def _residual_mm_body(*refs, n_parts):
    a_refs = refs[:n_parts]
    w_refs = refs[n_parts:2 * n_parts]
    h_ref, o_ref = refs[2 * n_parts], refs[2 * n_parts + 1]
    wbf_refs = refs[2 * n_parts + 2:]

    @pl.when(pl.program_id(1) == 0)
    def _():
        for w_ref, wbf_ref in zip(w_refs, wbf_refs):
            wbf_ref[...] = w_ref[...].astype(BF16)

    acc = h_ref[...]
    for a_ref, wbf_ref in zip(a_refs, wbf_refs):
        acc = acc + jnp.dot(a_ref[...], wbf_ref[...], preferred_element_type=F32)
    o_ref[...] = acc


def _residual_matmul(h, parts, w, layer, tm_pref=512):
    T, N = h.shape
    n_parts = len(parts)
    kp = parts[0].shape[1]
    tm = _tile(T, tm_pref)
    tn = _tile(N, 512)
    body = functools.partial(_residual_mm_body, n_parts=n_parts)
    a_specs = [pl.BlockSpec((tm, kp), lambda n, m: (m, 0)) for _ in parts]
    w_specs = [pl.BlockSpec((None, kp, tn), lambda n, m, i=i: (layer, i, n), pipeline_mode=pl.Buffered(1))
               for i in range(n_parts)]
    hspec = pl.BlockSpec((tm, tn), lambda n, m: (m, n))
    vmem = n_parts * (2 * tm * kp * 2 + kp * tn * 4 + kp * tn * 2) + 8 * tm * tn * 4
    return pl.pallas_call(
        body,
        out_shape=jax.ShapeDtypeStruct((T, N), F32),
        grid=(N // tn, T // tm),
        in_specs=a_specs + w_specs + [hspec],
        out_specs=hspec,
        scratch_shapes=[pltpu.VMEM((kp, tn), BF16) for _ in parts],
        compiler_params=_params(("parallel", "arbitrary"), vmem),
        name="residual_matmul",
    )(*parts, *([w] * n_parts), h)


def _ffn_up_body(a_ref, wg_ref, wv_ref, cwg_ref, cwv_ref, cbg_ref, cbv_ref, o_ref,
                 wg_bf_ref, wv_bf_ref, tail_g_ref, tail_v_ref, *, tm, tiles_per_seq):
    m = pl.program_id(1)

    @pl.when(m == 0)
    def _():
        wg_bf_ref[...] = wg_ref[...].astype(BF16)
        wv_bf_ref[...] = wv_ref[...].astype(BF16)

    @pl.when(m % tiles_per_seq == 0)
    def _():
        tail_g_ref[...] = jnp.zeros_like(tail_g_ref)
        tail_v_ref[...] = jnp.zeros_like(tail_v_ref)

    a = a_ref[...]

    def conv(wbf_ref, cw_ref, cb_ref, tail_ref):
        u = jnp.dot(a, wbf_ref[...], preferred_element_type=F32)
        row = lax.broadcasted_iota(jnp.int32, u.shape, 0)
        cw = cw_ref[...]
        out = u * cw[CONV_WIDTH - 1:CONV_WIDTH, :] + cb_ref[...]
        for back in range(1, CONV_WIDTH):
            shifted = pltpu.roll(u, back, 0)
            for i in range(back):
                prev_row = tail_ref[SUBLANES - back + i:SUBLANES - back + i + 1, :]
                shifted = jnp.where(row == i, prev_row, shifted)
            out = out + shifted * cw[CONV_WIDTH - 1 - back:CONV_WIDTH - back, :]
        tail_ref[...] = u[tm - SUBLANES:, :]
        return out

    gate = conv(wg_bf_ref, cwg_ref, cbg_ref, tail_g_ref)
    val = conv(wv_bf_ref, cwv_ref, cbv_ref, tail_v_ref)
    o_ref[...] = (gate / (1.0 + jnp.exp(-gate)) * val).astype(o_ref.dtype)


def _ffn_up(a, w_up, conv_w, conv_b, layer, S):
    T, D = a.shape
    F = w_up.shape[2] // 2
    tm = _tile(S, 512)
    tn = _tile(F, 512)
    nf = F // tn
    body = functools.partial(_ffn_up_body, tm=tm, tiles_per_seq=S // tm)
    conv_b = conv_b.reshape(conv_b.shape[0], 1, 2 * F)
    vmem = 2 * tm * D * 2 + 2 * (D * tn * 4 + D * tn * 2) + 16 * tm * tn * 4
    return pl.pallas_call(
        body,
        out_shape=jax.ShapeDtypeStruct((T, F), BF16),
        grid=(nf, T // tm),
        in_specs=[pl.BlockSpec((tm, D), lambda n, m: (m, 0)),
                  pl.BlockSpec((None, D, tn), lambda n, m: (layer, 0, n), pipeline_mode=pl.Buffered(1)),
                  pl.BlockSpec((None, D, tn), lambda n, m: (layer, 0, nf + n), pipeline_mode=pl.Buffered(1)),
                  pl.BlockSpec((None, CONV_WIDTH, tn), lambda n, m: (layer, 0, n)),
                  pl.BlockSpec((None, CONV_WIDTH, tn), lambda n, m: (layer, 0, nf + n)),
                  pl.BlockSpec((None, 1, tn), lambda n, m: (layer, 0, n)),
                  pl.BlockSpec((None, 1, tn), lambda n, m: (layer, 0, nf + n))],
        out_specs=pl.BlockSpec((tm, tn), lambda n, m: (m, n)),
        scratch_shapes=[pltpu.VMEM((D, tn), BF16), pltpu.VMEM((D, tn), BF16),
                        pltpu.VMEM((SUBLANES, tn), F32), pltpu.VMEM((SUBLANES, tn), F32)],
        compiler_params=_params(("parallel", "arbitrary"), vmem),
        name="ffn_up_conv_gate",
    )(a, w_up, w_up, conv_w, conv_w, conv_b, conv_b)


def _ple_body(a_ref, wg_ref, p_ref, wp_ref, h_ref, o_ref, wg_bf_ref, wp_bf_ref):
    @pl.when(pl.program_id(1) == 0)
    def _():
        wg_bf_ref[...] = wg_ref[...].astype(BF16)
        wp_bf_ref[...] = wp_ref[...].astype(BF16)

    logits = jnp.dot(a_ref[...], wg_bf_ref[...], preferred_element_type=F32)
    emb = jnp.dot(p_ref[...].astype(BF16), wp_bf_ref[...], preferred_element_type=F32)
    o_ref[...] = h_ref[...] + emb / (1.0 + jnp.exp(-logits))


def _ple(h, a, w_gate, p, w_ple, layer):
    T, D = h.shape
    P = p.shape[2]
    tm = _tile(T, 512)
    tn = _tile(D, 512)
    hspec = pl.BlockSpec((tm, tn), lambda n, m: (m, n))
    vmem = 2 * tm * D * 2 + D * tn * 6 + 2 * tm * P * 4 + P * tn * 6 + 10 * tm * tn * 4
    return pl.pallas_call(
        _ple_body,
        out_shape=jax.ShapeDtypeStruct((T, D), F32),
        grid=(D // tn, T // tm),
        in_specs=[pl.BlockSpec((tm, D), lambda n, m: (m, 0)),
                  pl.BlockSpec((None, D, tn), lambda n, m: (layer, 0, n), pipeline_mode=pl.Buffered(1)),
                  pl.BlockSpec((None, tm, P), lambda n, m: (layer, m, 0)),
                  pl.BlockSpec((None, P, tn), lambda n, m: (layer, 0, n), pipeline_mode=pl.Buffered(1)),
                  hspec],
        out_specs=hspec,
        scratch_shapes=[pltpu.VMEM((D, tn), BF16), pltpu.VMEM((P, tn), BF16)],
        compiler_params=_params(("parallel", "arbitrary"), vmem),
        name="gated_layer_embedding",
    )(a, w_gate, p, w_ple, h)


def kernel(x, p, positions, attn_norm, w_in_even, w_out_even, w_in_odd, b_forget, w_out_odd, ffn_norm,
           w_up, conv_w, conv_b, w_down, ple_norm, w_ple_gate, w_ple, final_norm):
    B, S, D = x.shape
    T = B * S
    depth = p.shape[0]
    H = D // HEAD_DIM
    n_sb = H // 2
    n_dil = H - n_sb
    h = x.reshape(T, D)
    p = p.reshape(depth, T, p.shape[-1])
    rope_tabs = _rope_tables(positions)
    for i in range(depth):
        a = _rmsnorm(h, attn_norm[i], BF16)
        if i % 2 == 0:
            qkv = _qkv_proj(a, w_in_even, i // 2, rope_tabs, n_sb)
            sb = _sb_attention(qkv, B, S, D, n_sb)
            windows = [_dilated_window(qkv, B, S, D, n_dil, window, dil) for window, dil in DILATED_CONFIGS]
            h = _residual_matmul(h, [sb, _mix_windows(windows)], w_out_even, i // 2)
        else:
            qkv = _qkv_proj(a, w_in_odd, i // 2, None, H)
            c_col, c_row = _forget_cumsum(a, w_in_odd[i // 2, :, 3 * D:], b_forget[i // 2], B, S)
            o = _fox_attention(qkv, c_col, c_row, B, S, D)
            h = _residual_matmul(h, [o], w_out_odd, i // 2)
        act = _ffn_up(_rmsnorm(h, ffn_norm[i], BF16), w_up, conv_w, conv_b, i, S)
        h = _residual_matmul(h, [act], w_down, i, tm_pref=256)
        h = _ple(h, _rmsnorm(h, ple_norm[i], BF16), w_ple_gate, p, w_ple, i)
    return _rmsnorm(h, final_norm, F32).reshape(B, S, D)
```

```python
import functools

import jax
import jax.numpy as jnp
from jax import lax
from jax.experimental import pallas as pl
from jax.experimental.pallas import tpu as pltpu

HEAD_DIM = 128
ROPE_DIM = HEAD_DIM // 4
ROPE_THETA = 500000.0
DILATED_CONFIGS = ((128, 1), (512, 4), (2048, 16))
BAND_BLOCK = 128
RMS_EPS = 1e-6
CONV_WIDTH = 3
MASK_VALUE = -1e30
ATTN_HEADS_PER_STEP = 2
LOG2_E = 1.4426950408889634
GATE_PARTS = 3
MATMUL_TN = 1024
FFN_SUBTILES = 2
SUBLANES = 8
VMEM_CAP_BYTES = 60 * 1024 * 1024

F32 = jnp.float32
BF16 = jnp.bfloat16


def _params(semantics, vmem_bytes):
    return pltpu.CompilerParams(dimension_semantics=semantics,
                                vmem_limit_bytes=int(min(vmem_bytes, VMEM_CAP_BYTES)))


def _tile(dim, pref):
    t = min(pref, dim)
    while dim % t:
        t //= 2
    return t


def _nt_dot(a, b):
    return lax.dot_general(a, b, (((1,), (1,)), ((), ())), preferred_element_type=F32)


def _tn_dot(a, b):
    return lax.dot_general(a, b, (((0,), (0,)), ((), ())), preferred_element_type=F32)


def _softplus_neg_abs(z):
    return jnp.log1p(jnp.exp(-jnp.abs(z)))


def _split_bf16(x, parts):
    out = []
    for _ in range(parts - 1):
        hi = x.astype(BF16)
        out.append(hi)
        x = x - hi.astype(F32)
    out.append(x.astype(BF16))
    return out


def _rmsnorm_body(h_ref, g_ref, o_ref):
    x = h_ref[...]
    ms = jnp.mean(x * x, axis=-1, keepdims=True)
    o_ref[...] = (x * lax.rsqrt(ms + RMS_EPS) * g_ref[...]).astype(o_ref.dtype)


def _rmsnorm(h, g, out_dtype):
    T, D = h.shape
    tm = _tile(T, 512)
    blk = tm * D * 4
    return pl.pallas_call(
        _rmsnorm_body,
        out_shape=jax.ShapeDtypeStruct((T, D), out_dtype),
        grid=(T // tm,),
        in_specs=[pl.BlockSpec((tm, D), lambda m: (m, 0)),
                  pl.BlockSpec((1, D), lambda m: (0, 0))],
        out_specs=pl.BlockSpec((tm, D), lambda m: (m, 0)),
        compiler_params=_params(("parallel",), 6 * blk),
        name="rmsnorm",
    )(h, g.reshape(1, D))


def _rope_table_body(pos_ref, invf_ref, cos_ref, sa_ref, sb_ref):
    ang = pos_ref[...].astype(F32) * invf_ref[...]
    lane = lax.broadcasted_iota(jnp.int32, ang.shape, 1)
    half = ROPE_DIM // 2
    s = jnp.sin(ang)
    cos_ref[...] = jnp.cos(ang)
    sa_ref[...] = jnp.where((lane >= half) & (lane < ROPE_DIM), s, 0.0)
    sb_ref[...] = jnp.where(lane < half, -s, 0.0)


def _rope_tables(positions):
    T = positions.size
    tm = _tile(T, 512)
    half = ROPE_DIM // 2
    inv_freq = ROPE_THETA ** (-jnp.arange(0, ROPE_DIM, 2, dtype=F32) / ROPE_DIM)
    invf = jnp.concatenate([inv_freq, inv_freq, jnp.zeros((HEAD_DIM - 2 * half,), F32)]).reshape(1, HEAD_DIM)
    tab = jax.ShapeDtypeStruct((T, HEAD_DIM), F32)
    spec = pl.BlockSpec((tm, HEAD_DIM), lambda m: (m, 0))
    return pl.pallas_call(
        _rope_table_body,
        out_shape=(tab, tab, tab),
        grid=(T // tm,),
        in_specs=[pl.BlockSpec((tm, 1), lambda m: (m, 0)),
                  pl.BlockSpec((1, HEAD_DIM), lambda m: (0, 0))],
        out_specs=(spec, spec, spec),
        compiler_params=_params(("parallel",), 32 * tm * HEAD_DIM * 4),
        name="rope_tables",
    )(positions.reshape(T, 1), invf)


def _qkv_body(*refs, rope, scale, tiles_per_section, heads_per_tile, first_rope_head):
    if rope:
        a_ref, w_ref, cos_ref, sa_ref, sb_ref, o_ref, wbf_ref = refs
    else:
        a_ref, w_ref, o_ref, wbf_ref = refs
    n = pl.program_id(0)

    @pl.when(pl.program_id(1) == 0)
    def _():
        wbf_ref[...] = w_ref[...].astype(BF16)

    acc = jnp.dot(a_ref[...], wbf_ref[...], preferred_element_type=F32)
    acc = acc * jnp.where(n < tiles_per_section, scale, 1.0).astype(F32)
    if not rope:
        o_ref[...] = acc.astype(o_ref.dtype)
        return

    section = n // tiles_per_section
    first_head = (n % tiles_per_section) * heads_per_tile
    is_rope = jnp.logical_and(section < 2, first_head >= first_rope_head)

    @pl.when(is_rope)
    def _():
        c, sa, sb = cos_ref[...], sa_ref[...], sb_ref[...]
        half = ROPE_DIM // 2
        for j in range(heads_per_tile):
            x = acc[:, j * HEAD_DIM:(j + 1) * HEAD_DIM]
            y = x * c + pltpu.roll(x, half, 1) * sa + pltpu.roll(x, HEAD_DIM - half, 1) * sb
            o_ref[:, j * HEAD_DIM:(j + 1) * HEAD_DIM] = y.astype(o_ref.dtype)

    @pl.when(jnp.logical_not(is_rope))
    def _():
        o_ref[...] = acc.astype(o_ref.dtype)


def _qkv_proj(a, w, layer, rope_tabs, first_rope_head, q_scale):
    T, D = a.shape
    rope = rope_tabs is not None
    tm = _tile(T, 512)
    tn = _tile(D, MATMUL_TN)
    if rope:
        tn = min(tn, first_rope_head * HEAD_DIM)
    body = functools.partial(
        _qkv_body, rope=rope, scale=q_scale, tiles_per_section=D // tn,
        heads_per_tile=tn // HEAD_DIM, first_rope_head=first_rope_head)
    in_specs = [pl.BlockSpec((tm, D), lambda n, m: (m, 0)),
                pl.BlockSpec((None, D, tn), lambda n, m: (layer, 0, n))]
    args = [a, w]
    if rope:
        in_specs += [pl.BlockSpec((tm, HEAD_DIM), lambda n, m: (m, 0))] * 3
        args += list(rope_tabs)
    vmem = 2 * tm * D * 2 + 2 * D * tn * 4 + D * tn * 2 + 6 * tm * tn * 4 + 6 * tm * HEAD_DIM * 4
    return pl.pallas_call(
        body,
        out_shape=jax.ShapeDtypeStruct((T, 3 * D), BF16),
        grid=(3 * D // tn, T // tm),
        in_specs=in_specs,
        out_specs=pl.BlockSpec((tm, tn), lambda n, m: (m, n)),
        scratch_shapes=[pltpu.VMEM((D, tn), BF16)],
        compiler_params=_params(("parallel", "arbitrary"), vmem),
        name="qkv_proj",
    )(*args)


def _sb_body(q_ref, k_ref, v_ref, o_ref, acc_ref, carry_ref, *, tq, tk, heads):
    qi = pl.program_id(2)
    ratio = tq // tk
    q = q_ref[...]
    acc_ref[...] = jnp.zeros_like(acc_ref)
    carry_ref[...] = jnp.zeros_like(carry_ref)
    strict_upper = (lax.broadcasted_iota(jnp.int32, (tk, tk), 1)
                    > lax.broadcasted_iota(jnp.int32, (tk, tk), 0)).astype(BF16)
    strict_upper2 = jnp.concatenate([strict_upper, strict_upper], axis=1)

    def block(j, masked):
        start = pl.multiple_of(j * tk, tk)
        if masked:
            kpos = j * tk + lax.broadcasted_iota(jnp.int32, (tk, tq), 0)
            qpos = qi * tq + lax.broadcasted_iota(jnp.int32, (tk, tq), 1)
            mask = kpos < qpos
        heads_sl = [slice(hh * HEAD_DIM, (hh + 1) * HEAD_DIM) for hh in range(heads)]
        scores = [_nt_dot(k_ref[pl.ds(start, tk), sl], q[:, sl]) for sl in heads_sl]
        tails = [_softplus_neg_abs(z) for z in scores]
        logs = [-jnp.maximum(z, 0.0) - tail for z, tail in zip(scores, tails)]
        if masked:
            logs = [jnp.where(mask, log_1mb, 0.0) for log_1mb in logs]
        suffixes = [jnp.dot(strict_upper2, jnp.concatenate(_split_bf16(log_1mb, 2), axis=0),
                            preferred_element_type=F32) for log_1mb in logs]
        for hh, sl in enumerate(heads_sl):
            carry = carry_ref[hh]
            w = jnp.exp(jnp.minimum(scores[hh], 0.0) - tails[hh] + suffixes[hh] + carry)
            if masked:
                w = jnp.where(mask, w, 0.0)
            acc_ref[hh] += _tn_dot(v_ref[pl.ds(start, tk), sl], w.astype(BF16))
            carry_ref[hh] = carry + suffixes[hh][0:1, :] + logs[hh][0:1, :]

    for jj in reversed(range(ratio)):
        block(qi * ratio + jj, True)

    def body(i, c):
        block(qi * ratio - 1 - i, False)
        return c

    lax.fori_loop(0, qi * ratio, body, 0)
    for hh in range(heads):
        o_ref[:, hh * HEAD_DIM:(hh + 1) * HEAD_DIM] = acc_ref[hh].T.astype(o_ref.dtype)


def _sb_attention(qkv, B, S, D, n_heads):
    T = B * S
    tq = _tile(S, 512)
    tk = _tile(S, 256)
    nq = S // tq
    heads = _tile(n_heads, ATTN_HEADS_PER_STEP)
    width = heads * HEAD_DIM
    hb = D // width
    body = functools.partial(_sb_body, tq=tq, tk=tk, heads=heads)
    vmem = heads * (4 * S * HEAD_DIM * 2 + 4 * tq * HEAD_DIM * 2 + tq * HEAD_DIM * 4 + SUBLANES * tq * 4
                    + 12 * tq * tk * 4)
    return pl.pallas_call(
        body,
        out_shape=jax.ShapeDtypeStruct((T, n_heads * HEAD_DIM), BF16),
        grid=(B, n_heads // heads, nq),
        in_specs=[pl.BlockSpec((tq, width), lambda b, h, i: (b * nq + i, h)),
                  pl.BlockSpec((S, width), lambda b, h, i: (b, hb + h)),
                  pl.BlockSpec((S, width), lambda b, h, i: (b, 2 * hb + h))],
        out_specs=pl.BlockSpec((tq, width), lambda b, h, i: (b * nq + i, h)),
        scratch_shapes=[pltpu.VMEM((heads, HEAD_DIM, tq), F32), pltpu.VMEM((heads, 1, tq), F32)],
        compiler_params=_params(("parallel", "parallel", "parallel"), vmem),
        name="stick_breaking_attention",
    )(qkv, qkv, qkv)


def _dilated_body(q_ref, kp_ref, kc_ref, vp_ref, vc_ref, o_ref, lse_ref, *, n_heads, n_back, blk):
    mb = pl.program_id(2)
    row = lax.broadcasted_iota(jnp.int32, (blk, 2 * blk), 0)
    col = lax.broadcasted_iota(jnp.int32, (blk, 2 * blk), 1)
    dist = row + blk - col
    valid = (dist >= 0) & (dist <= n_back) & ((mb > 0) | (col >= blk))
    for hd in range(n_heads):
        sl = slice(hd * HEAD_DIM, (hd + 1) * HEAD_DIM)
        kk = jnp.concatenate([kp_ref[:, sl], kc_ref[:, sl]], axis=0)
        vv = jnp.concatenate([vp_ref[:, sl], vc_ref[:, sl]], axis=0)
        s = jnp.where(valid, _nt_dot(q_ref[:, sl], kk), MASK_VALUE)
        m = jnp.max(s, axis=1, keepdims=True)
        p = jnp.exp(s - m)
        l = jnp.sum(p, axis=1, keepdims=True)
        o_ref[:, sl] = jnp.dot(p.astype(BF16), vv, preferred_element_type=F32) / l
        lse_ref[:, sl] = jnp.broadcast_to(m + jnp.log(l), (blk, HEAD_DIM))


def _dilated_window(qkv, B, S, D, n_heads, window, dil):
    T = B * S
    blk = BAND_BLOCK
    L = S // dil
    width = n_heads * HEAD_DIM
    per = D // width
    first = (D - width) // width
    grouped = qkv.reshape(B, L, dil * 3 * D)

    def spec(section, prev):
        def index(b, r, mb):
            row = jnp.maximum(mb - 1, 0) if prev else mb
            return (b, row, r * 3 * per + section * per + first)
        return pl.BlockSpec((None, blk, width), index)

    out = jax.ShapeDtypeStruct((B, L, dil * width), F32)
    out_spec = pl.BlockSpec((None, blk, width), lambda b, r, mb: (b, mb, r))
    body = functools.partial(_dilated_body, n_heads=n_heads, n_back=window // dil, blk=blk)
    vmem = 10 * blk * width * 2 + 4 * blk * width * 4 + 16 * blk * 2 * blk * 4
    o, lse = pl.pallas_call(
        body,
        out_shape=(out, out),
        grid=(B, dil, L // blk),
        in_specs=[spec(0, False), spec(1, True), spec(1, False), spec(2, True), spec(2, False)],
        out_specs=(out_spec, out_spec),
        compiler_params=_params(("parallel", "parallel", "parallel"), vmem),
        name=f"dilated_window_{dil}",
    )(grouped, grouped, grouped, grouped, grouped)
    return o.reshape(T, width), lse.reshape(T, width)


def _mix_windows_body(*refs):
    o_ref = refs[-1]
    outs, lses = refs[0:-1:2], refs[1:-1:2]
    top = lses[0][...]
    for l_ref in lses[1:]:
        top = jnp.maximum(top, l_ref[...])
    num = jnp.zeros(o_ref.shape, F32)
    den = jnp.zeros(o_ref.shape, F32)
    for out_ref, l_ref in zip(outs, lses):
        e = jnp.exp(l_ref[...] - top)
        num = num + e * out_ref[...]
        den = den + e
    o_ref[...] = (num / den).astype(o_ref.dtype)


def _mix_windows(parts):
    T, width = parts[0][0].shape
    tm = _tile(T, 256)
    flat = [x for pair in parts for x in pair]
    spec = pl.BlockSpec((tm, width), lambda m: (m, 0))
    return pl.pallas_call(
        _mix_windows_body,
        out_shape=jax.ShapeDtypeStruct((T, width), BF16),
        grid=(T // tm,),
        in_specs=[spec] * len(flat),
        out_specs=spec,
        compiler_params=_params(("parallel",), (2 * len(flat) + 8) * tm * width * 4),
        name="mix_windows",
    )(*flat)


def _forget_body(a_ref, w_ref, b_ref, c_ref, carry_ref, *, ts):
    @pl.when(pl.program_id(1) == 0)
    def _():
        carry_ref[...] = jnp.zeros_like(carry_ref)

    x = jnp.dot(a_ref[...], w_ref[...].astype(BF16), preferred_element_type=F32) + b_ref[...]
    log_f = jnp.minimum(x, 0.0) - _softplus_neg_abs(x)
    lower = (lax.broadcasted_iota(jnp.int32, (ts, ts), 0)
             >= lax.broadcasted_iota(jnp.int32, (ts, ts), 1)).astype(BF16)
    cum = carry_ref[...]
    for part in _split_bf16(log_f, 3):
        cum = cum + jnp.dot(lower, part, preferred_element_type=F32)
    c_ref[...] = cum
    carry_ref[...] = cum[ts - 1:ts, :]


def _forget_cumsum(a, w_f, b_f, B, S):
    T, D = a.shape
    H = w_f.shape[1]
    ts = _tile(S, 512)
    ns = S // ts
    body = functools.partial(_forget_body, ts=ts)
    vmem = 4 * ts * D * 2 + 4 * D * 128 * 4 + 8 * ts * ts * 4
    return pl.pallas_call(
        body,
        out_shape=jax.ShapeDtypeStruct((T, H), F32),
        grid=(B, ns),
        in_specs=[pl.BlockSpec((ts, D), lambda b, s: (b * ns + s, 0)),
                  pl.BlockSpec((D, H), lambda b, s: (0, 0)),
                  pl.BlockSpec((1, H), lambda b, s: (0, 0))],
        out_specs=pl.BlockSpec((ts, H), lambda b, s: (b * ns + s, 0)),
        scratch_shapes=[pltpu.VMEM((1, H), F32)],
        compiler_params=_params(("parallel", "arbitrary"), vmem),
        name="forget_cumsum",
    )(a, w_f, b_f.reshape(1, H))


def _gate_lanes(parts, head, first_lane, sign):
    n_heads = parts[0].shape[1]
    row = lax.broadcasted_iota(jnp.int32, (n_heads, HEAD_DIM), 0)
    lane = lax.broadcasted_iota(jnp.int32, (n_heads, HEAD_DIM), 1)
    lane_row = lax.broadcasted_iota(jnp.int32, (1, HEAD_DIM), 1)
    gate_lane = (lane_row >= first_lane) & (lane_row < first_lane + GATE_PARTS)
    out = jnp.where((lane_row < 2 * GATE_PARTS) & jnp.logical_not(gate_lane), 1.0, 0.0)
    for i, part in enumerate(parts):
        place = jnp.where((row == head) & (lane == first_lane + i), sign, 0.0).astype(BF16)
        out = out + jnp.dot(part, place, preferred_element_type=F32)
    return out.astype(BF16)


def _fox_body(q_ref, k_ref, v_ref, c_ref, o_ref, acc_ref, m_ref, l_ref, kx_ref, *, tq, tk, heads):
    hg = pl.program_id(1)
    qi = pl.program_id(2)
    ratio = tq // tk
    heads_sl = [slice(hh * HEAD_DIM, (hh + 1) * HEAD_DIM) for hh in range(heads)]

    @pl.when(qi == 0)
    def _():
        parts = _split_bf16(c_ref[...] * LOG2_E, GATE_PARTS)
        for hh, sl in enumerate(heads_sl):
            kx_ref[hh, :, :HEAD_DIM] = k_ref[:, sl]
            kx_ref[hh, :, HEAD_DIM:] = _gate_lanes(parts, hg * heads + hh, GATE_PARTS, -1.0)

    q_parts = _split_bf16(c_ref[pl.ds(pl.multiple_of(qi * tq, tq), tq), :] * LOG2_E, GATE_PARTS)
    qx = [jnp.concatenate([q_ref[:, sl], _gate_lanes(q_parts, hg * heads + hh, 0, 1.0)], axis=1)
          for hh, sl in enumerate(heads_sl)]
    acc_ref[...] = jnp.zeros_like(acc_ref)
    l_ref[...] = jnp.zeros_like(l_ref)
    m_ref[...] = jnp.full_like(m_ref, MASK_VALUE)

    def block(j, masked):
        start = pl.multiple_of(j * tk, tk)
        if masked:
            kpos = j * tk + lax.broadcasted_iota(jnp.int32, (tk, tq), 0)
            qpos = qi * tq + lax.broadcasted_iota(jnp.int32, (tk, tq), 1)
            mask = kpos <= qpos
        scores = [_nt_dot(kx_ref[hh, pl.ds(start, tk), :], qx[hh]) for hh in range(heads)]
        for hh, sl in enumerate(heads_sl):
            s = scores[hh]
            if masked:
                s = jnp.where(mask, s, MASK_VALUE)
            m_old = m_ref[hh]
            m_new = jnp.maximum(m_old, jnp.max(s, axis=0, keepdims=True))
            p = jnp.exp2(s - m_new)
            alpha = jnp.exp2(m_old - m_new)
            l_ref[hh] = alpha * l_ref[hh] + jnp.sum(p, axis=0, keepdims=True)
            acc_ref[hh] = alpha * acc_ref[hh] + _tn_dot(v_ref[pl.ds(start, tk), sl], p.astype(BF16))
            m_ref[hh] = m_new

    def body(j, c):
        block(j, False)
        return c

    lax.fori_loop(0, qi * ratio, body, 0)
    for jj in range(ratio):
        block(qi * ratio + jj, True)
    for hh in range(heads):
        o_ref[:, hh * HEAD_DIM:(hh + 1) * HEAD_DIM] = (acc_ref[hh] / l_ref[hh]).T.astype(o_ref.dtype)


def _fox_attention(qkv, c, B, S, D):
    T = B * S
    H = D // HEAD_DIM
    tq = _tile(S, 512)
    tk = _tile(S, 512)
    nq = S // tq
    heads = _tile(H, ATTN_HEADS_PER_STEP)
    width = heads * HEAD_DIM
    hb = D // width
    body = functools.partial(_fox_body, tq=tq, tk=tk, heads=heads)
    vmem = (2 * S * 128 * 4 + heads * (4 * S * HEAD_DIM * 2 + 4 * tq * HEAD_DIM * 2 + S * 2 * HEAD_DIM * 2
                                       + tq * HEAD_DIM * 4 + 2 * SUBLANES * tq * 4 + 10 * tq * tk * 4))
    return pl.pallas_call(
        body,
        out_shape=jax.ShapeDtypeStruct((T, D), BF16),
        grid=(B, hb, nq),
        in_specs=[pl.BlockSpec((tq, width), lambda b, h, i: (b * nq + i, h)),
                  pl.BlockSpec((S, width), lambda b, h, i: (b, hb + h)),
                  pl.BlockSpec((S, width), lambda b, h, i: (b, 2 * hb + h)),
                  pl.BlockSpec((S, H), lambda b, h, i: (b, 0))],
        out_specs=pl.BlockSpec((tq, width), lambda b, h, i: (b * nq + i, h)),
        scratch_shapes=[pltpu.VMEM((heads, HEAD_DIM, tq), F32), pltpu.VMEM((heads, 1, tq), F32),
                        pltpu.VMEM((heads, 1, tq), F32), pltpu.VMEM((heads, S, 2 * HEAD_DIM), BF16)],
        compiler_params=_params(("parallel", "parallel", "arbitrary"), vmem),
        name="forgetting_attention",
    )(qkv, qkv, qkv, c)


def _residual_mm_body(*refs, n_parts):
    a_refs = refs[:n_parts]
    w_refs = refs[n_parts:2 * n_parts]
    h_ref, o_ref = refs[2 * n_parts], refs[2 * n_parts + 1]
    wbf_refs = refs[2 * n_parts + 2:]

    @pl.when(pl.program_id(1) == 0)
    def _():
        for w_ref, wbf_ref in zip(w_refs, wbf_refs):
            wbf_ref[...] = w_ref[...].astype(BF16)

    acc = h_ref[...]
    for a_ref, wbf_ref in zip(a_refs, wbf_refs):
        acc = acc + jnp.dot(a_ref[...], wbf_ref[...], preferred_element_type=F32)
    o_ref[...] = acc


def _residual_matmul(h, parts, w, layer, tm_pref=512):
    T, N = h.shape
    n_parts = len(parts)
    kp = parts[0].shape[1]
    tm = _tile(T, tm_pref)
    tn = _tile(N, MATMUL_TN)
    body = functools.partial(_residual_mm_body, n_parts=n_parts)
    a_specs = [pl.BlockSpec((tm, kp), lambda n, m: (m, 0)) for _ in parts]
    w_specs = [pl.BlockSpec((None, kp, tn), lambda n, m, i=i: (layer, i, n), pipeline_mode=pl.Buffered(1))
               for i in range(n_parts)]
    hspec = pl.BlockSpec((tm, tn), lambda n, m: (m, n))
    vmem = n_parts * (2 * tm * kp * 2 + kp * tn * 4 + kp * tn * 2) + 8 * tm * tn * 4
    return pl.pallas_call(
        body,
        out_shape=jax.ShapeDtypeStruct((T, N), F32),
        grid=(N // tn, T // tm),
        in_specs=a_specs + w_specs + [hspec],
        out_specs=hspec,
        scratch_shapes=[pltpu.VMEM((kp, tn), BF16) for _ in parts],
        compiler_params=_params(("parallel", "arbitrary"), vmem),
        name="residual_matmul",
    )(*parts, *([w] * n_parts), h)


def _ffn_up_body(a_ref, wg_ref, wv_ref, cwg_ref, cwv_ref, cbg_ref, cbv_ref, o_ref,
                 wg_bf_ref, wv_bf_ref, tail_g_ref, tail_v_ref, *, tm, tiles_per_seq):
    m = pl.program_id(1)

    @pl.when(m == 0)
    def _():
        wg_bf_ref[...] = wg_ref[...].astype(BF16)
        wv_bf_ref[...] = wv_ref[...].astype(BF16)

    @pl.when(m % tiles_per_seq == 0)
    def _():
        tail_g_ref[...] = jnp.zeros_like(tail_g_ref)
        tail_v_ref[...] = jnp.zeros_like(tail_v_ref)

    def conv(a, wbf_ref, cw_ref, cb_ref, tail_ref):
        u = jnp.dot(a, wbf_ref[...], preferred_element_type=F32)
        row = lax.broadcasted_iota(jnp.int32, u.shape, 0)
        cw = cw_ref[...]
        out = u * cw[CONV_WIDTH - 1:CONV_WIDTH, :] + cb_ref[...]
        for back in range(1, CONV_WIDTH):
            shifted = pltpu.roll(u, back, 0)
            for i in range(back):
                prev_row = tail_ref[SUBLANES - back + i:SUBLANES - back + i + 1, :]
                shifted = jnp.where(row == i, prev_row, shifted)
            out = out + shifted * cw[CONV_WIDTH - 1 - back:CONV_WIDTH - back, :]
        tail_ref[...] = u[u.shape[0] - SUBLANES:, :]
        return out

    sub = tm // FFN_SUBTILES
    for i in range(FFN_SUBTILES):
        rows = slice(i * sub, (i + 1) * sub)
        a = a_ref[rows, :]
        gate = conv(a, wg_bf_ref, cwg_ref, cbg_ref, tail_g_ref)
        val = conv(a, wv_bf_ref, cwv_ref, cbv_ref, tail_v_ref)
        o_ref[rows, :] = (gate / (1.0 + jnp.exp(-gate)) * val).astype(o_ref.dtype)


def _ffn_up(a, w_up, conv_w, conv_b, layer, S):
    T, D = a.shape
    F = w_up.shape[2] // 2
    tm = _tile(S, 512 * FFN_SUBTILES)
    tn = _tile(F, 512)
    nf = F // tn
    body = functools.partial(_ffn_up_body, tm=tm, tiles_per_seq=S // tm)
    conv_b = conv_b.reshape(conv_b.shape[0], 1, 2 * F)
    vmem = 2 * tm * D * 2 + 2 * (D * tn * 4 + D * tn * 2) + 16 * tm * tn * 4
    return pl.pallas_call(
        body,
        out_shape=jax.ShapeDtypeStruct((T, F), BF16),
        grid=(nf, T // tm),
        in_specs=[pl.BlockSpec((tm, D), lambda n, m: (m, 0)),
                  pl.BlockSpec((None, D, tn), lambda n, m: (layer, 0, n), pipeline_mode=pl.Buffered(1)),
                  pl.BlockSpec((None, D, tn), lambda n, m: (layer, 0, nf + n), pipeline_mode=pl.Buffered(1)),
                  pl.BlockSpec((None, CONV_WIDTH, tn), lambda n, m: (layer, 0, n)),
                  pl.BlockSpec((None, CONV_WIDTH, tn), lambda n, m: (layer, 0, nf + n)),
                  pl.BlockSpec((None, 1, tn), lambda n, m: (layer, 0, n)),
                  pl.BlockSpec((None, 1, tn), lambda n, m: (layer, 0, nf + n))],
        out_specs=pl.BlockSpec((tm, tn), lambda n, m: (m, n)),
        scratch_shapes=[pltpu.VMEM((D, tn), BF16), pltpu.VMEM((D, tn), BF16),
                        pltpu.VMEM((SUBLANES, tn), F32), pltpu.VMEM((SUBLANES, tn), F32)],
        compiler_params=_params(("parallel", "arbitrary"), vmem),
        name="ffn_up_conv_gate",
    )(a, w_up, w_up, conv_w, conv_w, conv_b, conv_b)


def _ple_body(a_ref, wg_ref, p_ref, wp_ref, h_ref, o_ref, wg_bf_ref, wp_bf_ref):
    @pl.when(pl.program_id(1) == 0)
    def _():
        wg_bf_ref[...] = wg_ref[...].astype(BF16)
        wp_bf_ref[...] = wp_ref[...].astype(BF16)

    logits = jnp.dot(a_ref[...], wg_bf_ref[...], preferred_element_type=F32)
    emb = jnp.dot(p_ref[...].astype(BF16), wp_bf_ref[...], preferred_element_type=F32)
    o_ref[...] = h_ref[...] + emb / (1.0 + jnp.exp(-logits))


def _ple(h, a, w_gate, p, w_ple, layer):
    T, D = h.shape
    P = p.shape[2]
    tm = _tile(T, 512)
    tn = _tile(D, MATMUL_TN)
    hspec = pl.BlockSpec((tm, tn), lambda n, m: (m, n))
    vmem = 2 * tm * D * 2 + D * tn * 6 + 2 * tm * P * 4 + P * tn * 6 + 10 * tm * tn * 4
    return pl.pallas_call(
        _ple_body,
        out_shape=jax.ShapeDtypeStruct((T, D), F32),
        grid=(D // tn, T // tm),
        in_specs=[pl.BlockSpec((tm, D), lambda n, m: (m, 0)),
                  pl.BlockSpec((None, D, tn), lambda n, m: (layer, 0, n), pipeline_mode=pl.Buffered(1)),
                  pl.BlockSpec((None, tm, P), lambda n, m: (layer, m, 0)),
                  pl.BlockSpec((None, P, tn), lambda n, m: (layer, 0, n), pipeline_mode=pl.Buffered(1)),
                  hspec],
        out_specs=hspec,
        scratch_shapes=[pltpu.VMEM((D, tn), BF16), pltpu.VMEM((P, tn), BF16)],
        compiler_params=_params(("parallel", "arbitrary"), vmem),
        name="gated_layer_embedding",
    )(a, w_gate, p, w_ple, h)


def kernel(x, p, positions, attn_norm, w_in_even, w_out_even, w_in_odd, b_forget, w_out_odd, ffn_norm,
           w_up, conv_w, conv_b, w_down, ple_norm, w_ple_gate, w_ple, final_norm):
    B, S, D = x.shape
    T = B * S
    depth = p.shape[0]
    H = D // HEAD_DIM
    n_sb = H // 2
    n_dil = H - n_sb
    h = x.reshape(T, D)
    p = p.reshape(depth, T, p.shape[-1])
    rope_tabs = _rope_tables(positions)
    for i in range(depth):
        a = _rmsnorm(h, attn_norm[i], BF16)
        if i % 2 == 0:
            qkv = _qkv_proj(a, w_in_even, i // 2, rope_tabs, n_sb, HEAD_DIM ** -0.5)
            sb = _sb_attention(qkv, B, S, D, n_sb)
            windows = [_dilated_window(qkv, B, S, D, n_dil, window, dil) for window, dil in DILATED_CONFIGS]
            h = _residual_matmul(h, [sb, _mix_windows(windows)], w_out_even, i // 2)
        else:
            qkv = _qkv_proj(a, w_in_odd, i // 2, None, H, LOG2_E * HEAD_DIM ** -0.5)
            c = _forget_cumsum(a, w_in_odd[i // 2, :, 3 * D:], b_forget[i // 2], B, S)
            o = _fox_attention(qkv, c, B, S, D)
            h = _residual_matmul(h, [o], w_out_odd, i // 2)
        act = _ffn_up(_rmsnorm(h, ffn_norm[i], BF16), w_up, conv_w, conv_b, i, S)
        h = _residual_matmul(h, [act], w_down, i, tm_pref=256)
        h = _ple(h, _rmsnorm(h, ple_norm[i], BF16), w_ple_gate, p, w_ple, i)
    return _rmsnorm(h, final_norm, F32).reshape(B, S, D)
```

```python
import functools

import jax
import jax.numpy as jnp
from jax import lax
from jax.experimental import pallas as pl
from jax.experimental.pallas import tpu as pltpu

HEAD_DIM = 128
ROPE_DIM = HEAD_DIM // 4
ROPE_THETA = 500000.0
DILATED_CONFIGS = ((128, 1), (512, 4), (2048, 16))
BAND_BLOCK = 128
DILATED_SPAN = 2048
RMS_EPS = 1e-6
CONV_WIDTH = 3
MASK_VALUE = -1e30
ATTN_HEADS_PER_STEP = 2
LOG2_E = 1.4426950408889634
GATE_PARTS = 3
MATMUL_TN = 1024
FFN_SUBTILES = 4
SUBLANES = 8
VMEM_CAP_BYTES = 60 * 1024 * 1024

F32 = jnp.float32
BF16 = jnp.bfloat16


def _params(semantics, vmem_bytes):
    return pltpu.CompilerParams(dimension_semantics=semantics,
                                vmem_limit_bytes=int(min(vmem_bytes, VMEM_CAP_BYTES)))


def _tile(dim, pref):
    t = min(pref, dim)
    while dim % t:
        t //= 2
    return t


def _nt_dot(a, b):
    return lax.dot_general(a, b, (((1,), (1,)), ((), ())), preferred_element_type=F32)


def _tn_dot(a, b):
    return lax.dot_general(a, b, (((0,), (0,)), ((), ())), preferred_element_type=F32)


def _softplus_neg_abs(z):
    return jnp.log1p(jnp.exp(-jnp.abs(z)))


def _split_bf16(x, parts):
    out = []
    for _ in range(parts - 1):
        hi = x.astype(BF16)
        out.append(hi)
        x = x - hi.astype(F32)
    out.append(x.astype(BF16))
    return out


def _rmsnorm_body(h_ref, g_ref, o_ref):
    x = h_ref[...]
    ms = jnp.mean(x * x, axis=-1, keepdims=True)
    o_ref[...] = (x * lax.rsqrt(ms + RMS_EPS) * g_ref[...]).astype(o_ref.dtype)


def _rmsnorm(h, g, out_dtype):
    T, D = h.shape
    tm = _tile(T, 512)
    blk = tm * D * 4
    return pl.pallas_call(
        _rmsnorm_body,
        out_shape=jax.ShapeDtypeStruct((T, D), out_dtype),
        grid=(T // tm,),
        in_specs=[pl.BlockSpec((tm, D), lambda m: (m, 0)),
                  pl.BlockSpec((1, D), lambda m: (0, 0))],
        out_specs=pl.BlockSpec((tm, D), lambda m: (m, 0)),
        compiler_params=_params(("parallel",), 6 * blk),
        name="rmsnorm",
    )(h, g.reshape(1, D))


def _rope_table_body(pos_ref, invf_ref, cos_ref, sa_ref, sb_ref):
    ang = pos_ref[...].astype(F32) * invf_ref[...]
    lane = lax.broadcasted_iota(jnp.int32, ang.shape, 1)
    half = ROPE_DIM // 2
    s = jnp.sin(ang)
    cos_ref[...] = jnp.cos(ang)
    sa_ref[...] = jnp.where((lane >= half) & (lane < ROPE_DIM), s, 0.0)
    sb_ref[...] = jnp.where(lane < half, -s, 0.0)


def _rope_tables(positions):
    T = positions.size
    tm = _tile(T, 512)
    half = ROPE_DIM // 2
    inv_freq = ROPE_THETA ** (-jnp.arange(0, ROPE_DIM, 2, dtype=F32) / ROPE_DIM)
    invf = jnp.concatenate([inv_freq, inv_freq, jnp.zeros((HEAD_DIM - 2 * half,), F32)]).reshape(1, HEAD_DIM)
    tab = jax.ShapeDtypeStruct((T, HEAD_DIM), F32)
    spec = pl.BlockSpec((tm, HEAD_DIM), lambda m: (m, 0))
    return pl.pallas_call(
        _rope_table_body,
        out_shape=(tab, tab, tab),
        grid=(T // tm,),
        in_specs=[pl.BlockSpec((tm, 1), lambda m: (m, 0)),
                  pl.BlockSpec((1, HEAD_DIM), lambda m: (0, 0))],
        out_specs=(spec, spec, spec),
        compiler_params=_params(("parallel",), 32 * tm * HEAD_DIM * 4),
        name="rope_tables",
    )(positions.reshape(T, 1), invf)


def _qkv_body(*refs, rope, scale, tiles_per_section, heads_per_tile):
    if rope:
        a_ref, w_ref, cos_ref, sa_ref, sb_ref, o_ref, wbf_ref = refs
    else:
        a_ref, w_ref, o_ref, wbf_ref = refs
    n = pl.program_id(0)

    @pl.when(pl.program_id(1) == 0)
    def _():
        wbf_ref[...] = w_ref[...].astype(BF16)

    acc = jnp.dot(a_ref[...], wbf_ref[...], preferred_element_type=F32)
    acc = acc * jnp.where(n < tiles_per_section, scale, 1.0).astype(F32)
    if not rope:
        o_ref[...] = acc.astype(o_ref.dtype)
        return

    @pl.when(n < 2 * tiles_per_section)
    def _():
        c, sa, sb = cos_ref[...], sa_ref[...], sb_ref[...]
        half = ROPE_DIM // 2
        for j in range(heads_per_tile):
            x = acc[:, j * HEAD_DIM:(j + 1) * HEAD_DIM]
            y = x * c + pltpu.roll(x, half, 1) * sa + pltpu.roll(x, HEAD_DIM - half, 1) * sb
            o_ref[:, j * HEAD_DIM:(j + 1) * HEAD_DIM] = y.astype(o_ref.dtype)

    @pl.when(n >= 2 * tiles_per_section)
    def _():
        o_ref[...] = acc.astype(o_ref.dtype)


def _qkv_proj(a, w, layer, q_scale, first_head, n_heads, out_dtype, rope_tabs=None):
    T, D = a.shape
    rope = rope_tabs is not None
    width = n_heads * HEAD_DIM
    tm = _tile(T, 512)
    tn = _tile(width, MATMUL_TN)
    tiles = width // tn
    per_section, first = D // tn, first_head * HEAD_DIM // tn
    body = functools.partial(_qkv_body, rope=rope, scale=q_scale, tiles_per_section=tiles,
                             heads_per_tile=tn // HEAD_DIM)
    in_specs = [pl.BlockSpec((tm, D), lambda n, m: (m, 0)),
                pl.BlockSpec((None, D, tn), lambda n, m: (layer, 0, (n // tiles) * per_section + first + n % tiles))]
    args = [a, w]
    if rope:
        in_specs += [pl.BlockSpec((tm, HEAD_DIM), lambda n, m: (m, 0))] * 3
        args += list(rope_tabs)
    vmem = 2 * tm * D * 2 + 2 * D * tn * 4 + D * tn * 2 + 8 * tm * tn * 4 + 6 * tm * HEAD_DIM * 4
    return pl.pallas_call(
        body,
        out_shape=jax.ShapeDtypeStruct((T, 3 * width), out_dtype),
        grid=(3 * tiles, T // tm),
        in_specs=in_specs,
        out_specs=pl.BlockSpec((tm, tn), lambda n, m: (m, n)),
        scratch_shapes=[pltpu.VMEM((D, tn), BF16)],
        compiler_params=_params(("parallel", "arbitrary"), vmem),
        name="qkv_proj",
    )(*args)


def _sb_body(q_ref, k_ref, v_ref, o_ref, acc_ref, carry_ref, *, tq, tk, heads):
    qi = pl.program_id(2)
    ratio = tq // tk
    q = q_ref[...]
    acc_ref[...] = jnp.zeros_like(acc_ref)
    carry_ref[...] = jnp.zeros_like(carry_ref)
    strict_upper = (lax.broadcasted_iota(jnp.int32, (tk, tk), 1)
                    > lax.broadcasted_iota(jnp.int32, (tk, tk), 0)).astype(BF16)
    strict_upper2 = jnp.concatenate([strict_upper, strict_upper], axis=1)

    def block(j, masked):
        start = pl.multiple_of(j * tk, tk)
        if masked:
            kpos = j * tk + lax.broadcasted_iota(jnp.int32, (tk, tq), 0)
            qpos = qi * tq + lax.broadcasted_iota(jnp.int32, (tk, tq), 1)
            mask = kpos < qpos
        heads_sl = [slice(hh * HEAD_DIM, (hh + 1) * HEAD_DIM) for hh in range(heads)]
        scores = [_nt_dot(k_ref[pl.ds(start, tk), sl], q[:, sl]) for sl in heads_sl]
        tails = [_softplus_neg_abs(z) for z in scores]
        logs = [-jnp.maximum(z, 0.0) - tail for z, tail in zip(scores, tails)]
        if masked:
            logs = [jnp.where(mask, log_1mb, 0.0) for log_1mb in logs]
        suffixes = [jnp.dot(strict_upper2, jnp.concatenate(_split_bf16(log_1mb, 2), axis=0),
                            preferred_element_type=F32) for log_1mb in logs]
        for hh, sl in enumerate(heads_sl):
            carry = carry_ref[hh]
            w = jnp.exp(jnp.minimum(scores[hh], 0.0) - tails[hh] + suffixes[hh] + carry)
            if masked:
                w = jnp.where(mask, w, 0.0)
            acc_ref[hh] += _tn_dot(v_ref[pl.ds(start, tk), sl], w.astype(BF16))
            carry_ref[hh] = carry + suffixes[hh][0:1, :] + logs[hh][0:1, :]

    for jj in reversed(range(ratio)):
        block(qi * ratio + jj, True)

    def body(i, c):
        block(qi * ratio - 1 - i, False)
        return c

    lax.fori_loop(0, qi * ratio, body, 0)
    for hh in range(heads):
        o_ref[:, hh * HEAD_DIM:(hh + 1) * HEAD_DIM] = acc_ref[hh].T.astype(o_ref.dtype)


def _sb_attention(qkv, B, S, n_heads):
    T = B * S
    tq = _tile(S, 512)
    tk = _tile(S, 256)
    nq = S // tq
    heads = _tile(n_heads, ATTN_HEADS_PER_STEP)
    width = heads * HEAD_DIM
    hb = n_heads // heads
    body = functools.partial(_sb_body, tq=tq, tk=tk, heads=heads)
    vmem = heads * (4 * S * HEAD_DIM * 2 + 4 * tq * HEAD_DIM * 2 + tq * HEAD_DIM * 4 + SUBLANES * tq * 4
                    + 12 * tq * tk * 4)
    return pl.pallas_call(
        body,
        out_shape=jax.ShapeDtypeStruct((T, n_heads * HEAD_DIM), BF16),
        grid=(B, n_heads // heads, nq),
        in_specs=[pl.BlockSpec((tq, width), lambda b, h, i: (b * nq + i, h)),
                  pl.BlockSpec((S, width), lambda b, h, i: (b, hb + h)),
                  pl.BlockSpec((S, width), lambda b, h, i: (b, 2 * hb + h))],
        out_specs=pl.BlockSpec((tq, width), lambda b, h, i: (b * nq + i, h)),
        scratch_shapes=[pltpu.VMEM((heads, HEAD_DIM, tq), F32), pltpu.VMEM((heads, 1, tq), F32)],
        compiler_params=_params(("parallel", "parallel", "parallel"), vmem),
        name="stick_breaking_attention",
    )(qkv, qkv, qkv)


def _dilated_body(q_ref, kp_ref, kc_ref, vp_ref, vc_ref, o_ref, lse_ref, *, dil, n_back, blk, blocks):
    span_idx = pl.program_id(2)
    row = lax.broadcasted_iota(jnp.int32, (blk, 2 * blk), 0)
    col = lax.broadcasted_iota(jnp.int32, (blk, 2 * blk), 1)
    dist = row + blk - col
    band = (dist >= 0) & (dist <= n_back)
    band_first = band & ((span_idx > 0) | (col >= blk))

    def rows(ref, start):
        return ref[pl.ds(start, blk, stride=dil), :] if dil > 1 else ref[pl.ds(start, blk), :]

    for r in range(dil):
        for j in range(blocks):
            start = r + dil * blk * j
            before = start - dil * blk
            k_prev = rows(kc_ref, before) if j else rows(kp_ref, r)
            v_prev = rows(vc_ref, before) if j else rows(vp_ref, r)
            kk = jnp.concatenate([k_prev, rows(kc_ref, start)], axis=0).astype(BF16)
            vv = jnp.concatenate([v_prev, rows(vc_ref, start)], axis=0).astype(BF16)
            s = _nt_dot(rows(q_ref, start).astype(BF16), kk)
            s = jnp.where(band if j else band_first, s, MASK_VALUE)
            m = jnp.max(s, axis=1, keepdims=True)
            p = jnp.exp(s - m)
            l = jnp.sum(p, axis=1, keepdims=True)
            out = jnp.dot(p.astype(BF16), vv, preferred_element_type=F32) / l
            lse = jnp.broadcast_to(m + jnp.log(l), (blk, HEAD_DIM))
            if dil > 1:
                o_ref[pl.ds(start, blk, stride=dil), :] = out
                lse_ref[pl.ds(start, blk, stride=dil), :] = lse
            else:
                o_ref[pl.ds(start, blk), :] = out
                lse_ref[pl.ds(start, blk), :] = lse


def _dilated_window(qkv, B, S, n_heads, window, dil):
    T = B * S
    blk = BAND_BLOCK
    span = _tile(S, DILATED_SPAN)
    blocks = span // (dil * blk)
    spans = S // span
    halo = dil * blk

    def cur(section):
        return pl.BlockSpec((span, HEAD_DIM), lambda b, h, i: (b * spans + i, section * n_heads + h))

    def prev(section):
        return pl.BlockSpec((halo, HEAD_DIM),
                            lambda b, h, i: (jnp.maximum((b * spans + i) * blocks - 1, 0), section * n_heads + h))

    out = jax.ShapeDtypeStruct((T, n_heads * HEAD_DIM), F32)
    out_spec = pl.BlockSpec((span, HEAD_DIM), lambda b, h, i: (b * spans + i, h))
    body = functools.partial(_dilated_body, dil=dil, n_back=window // dil, blk=blk, blocks=blocks)
    vmem = 2 * (5 * span + 2 * halo) * HEAD_DIM * 4 + 24 * blk * 2 * blk * 4
    return pl.pallas_call(
        body,
        out_shape=(out, out),
        grid=(B, n_heads, spans),
        in_specs=[cur(0), prev(1), cur(1), prev(2), cur(2)],
        out_specs=(out_spec, out_spec),
        compiler_params=_params(("parallel", "parallel", "parallel"), vmem),
        name=f"dilated_window_{dil}",
    )(qkv, qkv, qkv, qkv, qkv)


def _mix_windows_body(*refs):
    o_ref = refs[-1]
    outs, lses = refs[0:-1:2], refs[1:-1:2]
    top = lses[0][...]
    for l_ref in lses[1:]:
        top = jnp.maximum(top, l_ref[...])
    num = jnp.zeros(o_ref.shape, F32)
    den = jnp.zeros(o_ref.shape, F32)
    for out_ref, l_ref in zip(outs, lses):
        e = jnp.exp(l_ref[...] - top)
        num = num + e * out_ref[...]
        den = den + e
    o_ref[...] = (num / den).astype(o_ref.dtype)


def _mix_windows(parts):
    T, width = parts[0][0].shape
    tm = _tile(T, 256)
    flat = [x for pair in parts for x in pair]
    spec = pl.BlockSpec((tm, width), lambda m: (m, 0))
    return pl.pallas_call(
        _mix_windows_body,
        out_shape=jax.ShapeDtypeStruct((T, width), BF16),
        grid=(T // tm,),
        in_specs=[spec] * len(flat),
        out_specs=spec,
        compiler_params=_params(("parallel",), (2 * len(flat) + 8) * tm * width * 4),
        name="mix_windows",
    )(*flat)


def _forget_body(a_ref, w_ref, b_ref, c_ref, carry_ref, *, ts):
    @pl.when(pl.program_id(1) == 0)
    def _():
        carry_ref[...] = jnp.zeros_like(carry_ref)

    x = jnp.dot(a_ref[...], w_ref[...].astype(BF16), preferred_element_type=F32) + b_ref[...]
    log_f = jnp.minimum(x, 0.0) - _softplus_neg_abs(x)
    lower = (lax.broadcasted_iota(jnp.int32, (ts, ts), 0)
             >= lax.broadcasted_iota(jnp.int32, (ts, ts), 1)).astype(BF16)
    cum = carry_ref[...]
    for part in _split_bf16(log_f, 3):
        cum = cum + jnp.dot(lower, part, preferred_element_type=F32)
    c_ref[...] = cum
    carry_ref[...] = cum[ts - 1:ts, :]


def _forget_cumsum(a, w_f, b_f, B, S):
    T, D = a.shape
    H = w_f.shape[1]
    ts = _tile(S, 512)
    ns = S // ts
    body = functools.partial(_forget_body, ts=ts)
    vmem = 4 * ts * D * 2 + 4 * D * 128 * 4 + 8 * ts * ts * 4
    return pl.pallas_call(
        body,
        out_shape=jax.ShapeDtypeStruct((T, H), F32),
        grid=(B, ns),
        in_specs=[pl.BlockSpec((ts, D), lambda b, s: (b * ns + s, 0)),
                  pl.BlockSpec((D, H), lambda b, s: (0, 0)),
                  pl.BlockSpec((1, H), lambda b, s: (0, 0))],
        out_specs=pl.BlockSpec((ts, H), lambda b, s: (b * ns + s, 0)),
        scratch_shapes=[pltpu.VMEM((1, H), F32)],
        compiler_params=_params(("parallel", "arbitrary"), vmem),
        name="forget_cumsum",
    )(a, w_f, b_f.reshape(1, H))


def _gate_lanes(parts, head, first_lane, sign):
    n_heads = parts[0].shape[1]
    row = lax.broadcasted_iota(jnp.int32, (n_heads, HEAD_DIM), 0)
    lane = lax.broadcasted_iota(jnp.int32, (n_heads, HEAD_DIM), 1)
    lane_row = lax.broadcasted_iota(jnp.int32, (1, HEAD_DIM), 1)
    gate_lane = (lane_row >= first_lane) & (lane_row < first_lane + GATE_PARTS)
    out = jnp.where((lane_row < 2 * GATE_PARTS) & jnp.logical_not(gate_lane), 1.0, 0.0)
    for i, part in enumerate(parts):
        place = jnp.where((row == head) & (lane == first_lane + i), sign, 0.0).astype(BF16)
        out = out + jnp.dot(part, place, preferred_element_type=F32)
    return out.astype(BF16)


def _fox_body(q_ref, k_ref, v_ref, c_ref, o_ref, acc_ref, m_ref, l_ref, kx_ref, *, tq, tk, heads):
    hg = pl.program_id(1)
    qi = pl.program_id(2)
    ratio = tq // tk
    heads_sl = [slice(hh * HEAD_DIM, (hh + 1) * HEAD_DIM) for hh in range(heads)]

    @pl.when(qi == 0)
    def _():
        parts = _split_bf16(c_ref[...] * LOG2_E, GATE_PARTS)
        for hh, sl in enumerate(heads_sl):
            kx_ref[hh, :, :HEAD_DIM] = k_ref[:, sl]
            kx_ref[hh, :, HEAD_DIM:] = _gate_lanes(parts, hg * heads + hh, GATE_PARTS, -1.0)

    q_parts = _split_bf16(c_ref[pl.ds(pl.multiple_of(qi * tq, tq), tq), :] * LOG2_E, GATE_PARTS)
    qx = [jnp.concatenate([q_ref[:, sl], _gate_lanes(q_parts, hg * heads + hh, 0, 1.0)], axis=1)
          for hh, sl in enumerate(heads_sl)]
    acc_ref[...] = jnp.zeros_like(acc_ref)
    l_ref[...] = jnp.zeros_like(l_ref)
    m_ref[...] = jnp.full_like(m_ref, MASK_VALUE)

    def block(j, masked):
        start = pl.multiple_of(j * tk, tk)
        if masked:
            kpos = j * tk + lax.broadcasted_iota(jnp.int32, (tk, tq), 0)
            qpos = qi * tq + lax.broadcasted_iota(jnp.int32, (tk, tq), 1)
            mask = kpos <= qpos
        scores = [_nt_dot(kx_ref[hh, pl.ds(start, tk), :], qx[hh]) for hh in range(heads)]
        for hh, sl in enumerate(heads_sl):
            s = scores[hh]
            if masked:
                s = jnp.where(mask, s, MASK_VALUE)
            m_old = m_ref[hh]
            m_new = jnp.maximum(m_old, jnp.max(s, axis=0, keepdims=True))
            p = jnp.exp2(s - m_new)
            alpha = jnp.exp2(m_old - m_new)
            l_ref[hh] = alpha * l_ref[hh] + jnp.sum(p, axis=0, keepdims=True)
            acc_ref[hh] = alpha * acc_ref[hh] + _tn_dot(v_ref[pl.ds(start, tk), sl], p.astype(BF16))
            m_ref[hh] = m_new

    def body(j, c):
        block(j, False)
        return c

    lax.fori_loop(0, qi * ratio, body, 0)
    for jj in range(ratio):
        block(qi * ratio + jj, True)
    for hh in range(heads):
        o_ref[:, hh * HEAD_DIM:(hh + 1) * HEAD_DIM] = (acc_ref[hh] / l_ref[hh]).T.astype(o_ref.dtype)


def _fox_attention(qkv, c, B, S, D):
    T = B * S
    H = D // HEAD_DIM
    tq = _tile(S, 512)
    tk = _tile(S, 512)
    nq = S // tq
    heads = _tile(H, ATTN_HEADS_PER_STEP)
    width = heads * HEAD_DIM
    hb = D // width
    body = functools.partial(_fox_body, tq=tq, tk=tk, heads=heads)
    vmem = (2 * S * 128 * 4 + heads * (4 * S * HEAD_DIM * 2 + 4 * tq * HEAD_DIM * 2 + S * 2 * HEAD_DIM * 2
                                       + tq * HEAD_DIM * 4 + 2 * SUBLANES * tq * 4 + 10 * tq * tk * 4))
    return pl.pallas_call(
        body,
        out_shape=jax.ShapeDtypeStruct((T, D), BF16),
        grid=(B, hb, nq),
        in_specs=[pl.BlockSpec((tq, width), lambda b, h, i: (b * nq + i, h)),
                  pl.BlockSpec((S, width), lambda b, h, i: (b, hb + h)),
                  pl.BlockSpec((S, width), lambda b, h, i: (b, 2 * hb + h)),
                  pl.BlockSpec((S, H), lambda b, h, i: (b, 0))],
        out_specs=pl.BlockSpec((tq, width), lambda b, h, i: (b * nq + i, h)),
        scratch_shapes=[pltpu.VMEM((heads, HEAD_DIM, tq), F32), pltpu.VMEM((heads, 1, tq), F32),
                        pltpu.VMEM((heads, 1, tq), F32), pltpu.VMEM((heads, S, 2 * HEAD_DIM), BF16)],
        compiler_params=_params(("parallel", "parallel", "arbitrary"), vmem),
        name="forgetting_attention",
    )(qkv, qkv, qkv, c)


def _residual_mm_body(*refs, n_parts):
    a_refs = refs[:n_parts]
    w_refs = refs[n_parts:2 * n_parts]
    h_ref, o_ref = refs[2 * n_parts], refs[2 * n_parts + 1]
    wbf_refs = refs[2 * n_parts + 2:]

    @pl.when(pl.program_id(1) == 0)
    def _():
        for w_ref, wbf_ref in zip(w_refs, wbf_refs):
            wbf_ref[...] = w_ref[...].astype(BF16)

    acc = h_ref[...]
    for a_ref, wbf_ref in zip(a_refs, wbf_refs):
        acc = acc + jnp.dot(a_ref[...], wbf_ref[...], preferred_element_type=F32)
    o_ref[...] = acc


def _residual_matmul(h, parts, w, layer, tm_pref=512):
    T, N = h.shape
    n_parts = len(parts)
    kp = parts[0].shape[1]
    tm = _tile(T, tm_pref)
    tn = _tile(N, MATMUL_TN)
    body = functools.partial(_residual_mm_body, n_parts=n_parts)
    a_specs = [pl.BlockSpec((tm, kp), lambda n, m: (m, 0)) for _ in parts]
    w_specs = [pl.BlockSpec((None, kp, tn), lambda n, m, i=i: (layer, i, n), pipeline_mode=pl.Buffered(1))
               for i in range(n_parts)]
    hspec = pl.BlockSpec((tm, tn), lambda n, m: (m, n))
    vmem = n_parts * (2 * tm * kp * 2 + kp * tn * 4 + kp * tn * 2) + 8 * tm * tn * 4
    return pl.pallas_call(
        body,
        out_shape=jax.ShapeDtypeStruct((T, N), F32),
        grid=(N // tn, T // tm),
        in_specs=a_specs + w_specs + [hspec],
        out_specs=hspec,
        scratch_shapes=[pltpu.VMEM((kp, tn), BF16) for _ in parts],
        compiler_params=_params(("parallel", "arbitrary"), vmem),
        name="residual_matmul",
    )(*parts, *([w] * n_parts), h)


def _ffn_up_body(a_ref, wg_ref, wv_ref, cwg_ref, cwv_ref, cbg_ref, cbv_ref, o_ref,
                 wg_bf_ref, wv_bf_ref, u_ref, tails_ref, *, tm, tiles_per_seq):
    m = pl.program_id(1)
    sub = tm // FFN_SUBTILES
    wbf_refs, cw_refs, cb_refs = (wg_bf_ref, wv_bf_ref), (cwg_ref, cwv_ref), (cbg_ref, cbv_ref)

    @pl.when(m == 0)
    def _():
        wg_bf_ref[...] = wg_ref[...].astype(BF16)
        wv_bf_ref[...] = wv_ref[...].astype(BF16)

    @pl.when(m % tiles_per_seq == 0)
    def _():
        tails_ref[0] = jnp.zeros(tails_ref.shape[1:], F32)

    def products(i):
        a = a_ref[i * sub:(i + 1) * sub, :]
        for g in range(2):
            u = jnp.dot(a, wbf_refs[g][...], preferred_element_type=F32)
            u_ref[i % 2, g] = u
            tails_ref[i + 1, g] = u[sub - SUBLANES:, :]

    def conv(i, g):
        u = u_ref[i % 2, g]
        row = lax.broadcasted_iota(jnp.int32, u.shape, 0)
        cw = cw_refs[g][...]
        out = u * cw[CONV_WIDTH - 1:CONV_WIDTH, :] + cb_refs[g][...]
        for back in range(1, CONV_WIDTH):
            shifted = pltpu.roll(u, back, 0)
            for k in range(back):
                prev_row = tails_ref[i, g, SUBLANES - back + k:SUBLANES - back + k + 1, :]
                shifted = jnp.where(row == k, prev_row, shifted)
            out = out + shifted * cw[CONV_WIDTH - 1 - back:CONV_WIDTH - back, :]
        return out

    def gated(i):
        gate, val = conv(i, 0), conv(i, 1)
        o_ref[i * sub:(i + 1) * sub, :] = (gate / (1.0 + jnp.exp(-gate)) * val).astype(o_ref.dtype)

    products(0)
    for i in range(1, FFN_SUBTILES):
        products(i)
        gated(i - 1)
    gated(FFN_SUBTILES - 1)
    tails_ref[0] = tails_ref[FFN_SUBTILES]


def _ffn_up(a, w_up, conv_w, conv_b, layer, S):
    T, D = a.shape
    F = w_up.shape[2] // 2
    tm = _tile(S, 512 * FFN_SUBTILES)
    tn = _tile(F, 512)
    nf = F // tn
    body = functools.partial(_ffn_up_body, tm=tm, tiles_per_seq=S // tm)
    conv_b = conv_b.reshape(conv_b.shape[0], 1, 2 * F)
    sub = tm // FFN_SUBTILES
    vmem = (2 * tm * D * 2 + 2 * (D * tn * 4 + D * tn * 2) + 2 * tm * tn * 2 + 4 * sub * tn * 4
            + 16 * sub * tn * 4)
    return pl.pallas_call(
        body,
        out_shape=jax.ShapeDtypeStruct((T, F), BF16),
        grid=(nf, T // tm),
        in_specs=[pl.BlockSpec((tm, D), lambda n, m: (m, 0)),
                  pl.BlockSpec((None, D, tn), lambda n, m: (layer, 0, n), pipeline_mode=pl.Buffered(1)),
                  pl.BlockSpec((None, D, tn), lambda n, m: (layer, 0, nf + n), pipeline_mode=pl.Buffered(1)),
                  pl.BlockSpec((None, CONV_WIDTH, tn), lambda n, m: (layer, 0, n)),
                  pl.BlockSpec((None, CONV_WIDTH, tn), lambda n, m: (layer, 0, nf + n)),
                  pl.BlockSpec((None, 1, tn), lambda n, m: (layer, 0, n)),
                  pl.BlockSpec((None, 1, tn), lambda n, m: (layer, 0, nf + n))],
        out_specs=pl.BlockSpec((tm, tn), lambda n, m: (m, n)),
        scratch_shapes=[pltpu.VMEM((D, tn), BF16), pltpu.VMEM((D, tn), BF16),
                        pltpu.VMEM((2, 2, sub, tn), F32),
                        pltpu.VMEM((FFN_SUBTILES + 1, 2, SUBLANES, tn), F32)],
        compiler_params=_params(("parallel", "arbitrary"), vmem),
        name="ffn_up_conv_gate",
    )(a, w_up, w_up, conv_w, conv_w, conv_b, conv_b)


def _ple_body(a_ref, wg_ref, p_ref, wp_ref, h_ref, o_ref, wg_bf_ref, wp_bf_ref):
    @pl.when(pl.program_id(1) == 0)
    def _():
        wg_bf_ref[...] = wg_ref[...].astype(BF16)
        wp_bf_ref[...] = wp_ref[...].astype(BF16)

    logits = jnp.dot(a_ref[...], wg_bf_ref[...], preferred_element_type=F32)
    emb = jnp.dot(p_ref[...].astype(BF16), wp_bf_ref[...], preferred_element_type=F32)
    o_ref[...] = h_ref[...] + emb / (1.0 + jnp.exp(-logits))


def _ple(h, a, w_gate, p, w_ple, layer):
    T, D = h.shape
    P = p.shape[2]
    tm = _tile(T, 512)
    tn = _tile(D, MATMUL_TN)
    hspec = pl.BlockSpec((tm, tn), lambda n, m: (m, n))
    vmem = 2 * tm * D * 2 + D * tn * 6 + 2 * tm * P * 4 + P * tn * 6 + 10 * tm * tn * 4
    return pl.pallas_call(
        _ple_body,
        out_shape=jax.ShapeDtypeStruct((T, D), F32),
        grid=(D // tn, T // tm),
        in_specs=[pl.BlockSpec((tm, D), lambda n, m: (m, 0)),
                  pl.BlockSpec((None, D, tn), lambda n, m: (layer, 0, n), pipeline_mode=pl.Buffered(1)),
                  pl.BlockSpec((None, tm, P), lambda n, m: (layer, m, 0)),
                  pl.BlockSpec((None, P, tn), lambda n, m: (layer, 0, n), pipeline_mode=pl.Buffered(1)),
                  hspec],
        out_specs=hspec,
        scratch_shapes=[pltpu.VMEM((D, tn), BF16), pltpu.VMEM((P, tn), BF16)],
        compiler_params=_params(("parallel", "arbitrary"), vmem),
        name="gated_layer_embedding",
    )(a, w_gate, p, w_ple, h)


def kernel(x, p, positions, attn_norm, w_in_even, w_out_even, w_in_odd, b_forget, w_out_odd, ffn_norm,
           w_up, conv_w, conv_b, w_down, ple_norm, w_ple_gate, w_ple, final_norm):
    B, S, D = x.shape
    T = B * S
    depth = p.shape[0]
    H = D // HEAD_DIM
    n_sb = H // 2
    n_dil = H - n_sb
    h = x.reshape(T, D)
    p = p.reshape(depth, T, p.shape[-1])
    rope_tabs = _rope_tables(positions)
    for i in range(depth):
        a = _rmsnorm(h, attn_norm[i], BF16)
        if i % 2 == 0:
            scale = HEAD_DIM ** -0.5
            qkv_sb = _qkv_proj(a, w_in_even, i // 2, scale, 0, n_sb, BF16)
            qkv_dil = _qkv_proj(a, w_in_even, i // 2, scale, n_sb, n_dil, F32, rope_tabs)
            sb = _sb_attention(qkv_sb, B, S, n_sb)
            windows = [_dilated_window(qkv_dil, B, S, n_dil, window, dil) for window, dil in DILATED_CONFIGS]
            h = _residual_matmul(h, [sb, _mix_windows(windows)], w_out_even, i // 2)
        else:
            qkv = _qkv_proj(a, w_in_odd, i // 2, LOG2_E * HEAD_DIM ** -0.5, 0, H, BF16)
            c = _forget_cumsum(a, w_in_odd[i // 2, :, 3 * D:], b_forget[i // 2], B, S)
            o = _fox_attention(qkv, c, B, S, D)
            h = _residual_matmul(h, [o], w_out_odd, i // 2)
        act = _ffn_up(_rmsnorm(h, ffn_norm[i], BF16), w_up, conv_w, conv_b, i, S)
        h = _residual_matmul(h, [act], w_down, i, tm_pref=256)
        h = _ple(h, _rmsnorm(h, ple_norm[i], BF16), w_ple_gate, p, w_ple, i)
    return _rmsnorm(h, final_norm, F32).reshape(B, S, D)
```

```python
import functools

import jax
import jax.numpy as jnp
from jax import lax
from jax.experimental import pallas as pl
from jax.experimental.pallas import tpu as pltpu

HEAD_DIM = 128
ROPE_DIM = HEAD_DIM // 4
ROPE_THETA = 500000.0
DILATED_CONFIGS = ((128, 1), (512, 4), (2048, 16))
BAND_BLOCK = 128
DILATED_SPAN = 2048
RMS_EPS = 1e-6
CONV_WIDTH = 3
MASK_VALUE = -1e30
ATTN_HEADS_PER_STEP = 4
LOG2_E = 1.4426950408889634
GATE_PARTS = 3
MATMUL_TN = 1024
FFN_SUBTILES = 4
SUBLANES = 8
VMEM_CAP_BYTES = 60 * 1024 * 1024

F32 = jnp.float32
BF16 = jnp.bfloat16


def _params(semantics, vmem_bytes):
    return pltpu.CompilerParams(dimension_semantics=semantics,
                                vmem_limit_bytes=int(min(vmem_bytes, VMEM_CAP_BYTES)))


def _tile(dim, pref):
    t = min(pref, dim)
    while dim % t:
        t //= 2
    return t


def _nt_dot(a, b):
    return lax.dot_general(a, b, (((1,), (1,)), ((), ())), preferred_element_type=F32)


def _tn_dot(a, b):
    return lax.dot_general(a, b, (((0,), (0,)), ((), ())), preferred_element_type=F32)


def _softplus_neg_abs(z):
    return jnp.log1p(jnp.exp(-jnp.abs(z)))


def _split_bf16(x, parts):
    out = []
    for _ in range(parts - 1):
        hi = x.astype(BF16)
        out.append(hi)
        x = x - hi.astype(F32)
    out.append(x.astype(BF16))
    return out


def _rmsnorm_body(h_ref, g_ref, o_ref):
    x = h_ref[...]
    ms = jnp.mean(x * x, axis=-1, keepdims=True)
    o_ref[...] = (x * lax.rsqrt(ms + RMS_EPS) * g_ref[...]).astype(o_ref.dtype)


def _rmsnorm(h, g, out_dtype):
    T, D = h.shape
    tm = _tile(T, 512)
    blk = tm * D * 4
    return pl.pallas_call(
        _rmsnorm_body,
        out_shape=jax.ShapeDtypeStruct((T, D), out_dtype),
        grid=(T // tm,),
        in_specs=[pl.BlockSpec((tm, D), lambda m: (m, 0)),
                  pl.BlockSpec((1, D), lambda m: (0, 0))],
        out_specs=pl.BlockSpec((tm, D), lambda m: (m, 0)),
        compiler_params=_params(("parallel",), 6 * blk),
        name="rmsnorm",
    )(h, g.reshape(1, D))


def _rope_table_body(pos_ref, invf_ref, cos_ref, sa_ref, sb_ref):
    ang = pos_ref[...].astype(F32) * invf_ref[...]
    lane = lax.broadcasted_iota(jnp.int32, ang.shape, 1)
    half = ROPE_DIM // 2
    s = jnp.sin(ang)
    cos_ref[...] = jnp.cos(ang)
    sa_ref[...] = jnp.where((lane >= half) & (lane < ROPE_DIM), s, 0.0)
    sb_ref[...] = jnp.where(lane < half, -s, 0.0)


def _rope_tables(positions):
    T = positions.size
    tm = _tile(T, 512)
    half = ROPE_DIM // 2
    inv_freq = ROPE_THETA ** (-jnp.arange(0, ROPE_DIM, 2, dtype=F32) / ROPE_DIM)
    invf = jnp.concatenate([inv_freq, inv_freq, jnp.zeros((HEAD_DIM - 2 * half,), F32)]).reshape(1, HEAD_DIM)
    tab = jax.ShapeDtypeStruct((T, HEAD_DIM), F32)
    spec = pl.BlockSpec((tm, HEAD_DIM), lambda m: (m, 0))
    return pl.pallas_call(
        _rope_table_body,
        out_shape=(tab, tab, tab),
        grid=(T // tm,),
        in_specs=[pl.BlockSpec((tm, 1), lambda m: (m, 0)),
                  pl.BlockSpec((1, HEAD_DIM), lambda m: (0, 0))],
        out_specs=(spec, spec, spec),
        compiler_params=_params(("parallel",), 32 * tm * HEAD_DIM * 4),
        name="rope_tables",
    )(positions.reshape(T, 1), invf)


def _qkv_body(*refs, rope, scale, tiles_per_section, heads_per_tile):
    if rope:
        a_ref, w_ref, cos_ref, sa_ref, sb_ref, o_ref, wbf_ref = refs
    else:
        a_ref, w_ref, o_ref, wbf_ref = refs
    n = pl.program_id(0)

    @pl.when(pl.program_id(1) == 0)
    def _():
        wbf_ref[...] = w_ref[...].astype(BF16)

    acc = jnp.dot(a_ref[...], wbf_ref[...], preferred_element_type=F32)
    acc = acc * jnp.where(n < tiles_per_section, scale, 1.0).astype(F32)
    if not rope:
        o_ref[...] = acc.astype(o_ref.dtype)
        return

    @pl.when(n < 2 * tiles_per_section)
    def _():
        c, sa, sb = cos_ref[...], sa_ref[...], sb_ref[...]
        half = ROPE_DIM // 2
        for j in range(heads_per_tile):
            x = acc[:, j * HEAD_DIM:(j + 1) * HEAD_DIM]
            y = x * c + pltpu.roll(x, half, 1) * sa + pltpu.roll(x, HEAD_DIM - half, 1) * sb
            o_ref[:, j * HEAD_DIM:(j + 1) * HEAD_DIM] = y.astype(o_ref.dtype)

    @pl.when(n >= 2 * tiles_per_section)
    def _():
        o_ref[...] = acc.astype(o_ref.dtype)


def _qkv_proj(a, w, layer, q_scale, first_head, n_heads, out_dtype, rope_tabs=None):
    T, D = a.shape
    rope = rope_tabs is not None
    width = n_heads * HEAD_DIM
    tm = _tile(T, 512)
    tn = _tile(width, MATMUL_TN)
    tiles = width // tn
    per_section, first = D // tn, first_head * HEAD_DIM // tn
    body = functools.partial(_qkv_body, rope=rope, scale=q_scale, tiles_per_section=tiles,
                             heads_per_tile=tn // HEAD_DIM)
    in_specs = [pl.BlockSpec((tm, D), lambda n, m: (m, 0)),
                pl.BlockSpec((None, D, tn), lambda n, m: (layer, 0, (n // tiles) * per_section + first + n % tiles))]
    args = [a, w]
    if rope:
        in_specs += [pl.BlockSpec((tm, HEAD_DIM), lambda n, m: (m, 0))] * 3
        args += list(rope_tabs)
    vmem = 2 * tm * D * 2 + 2 * D * tn * 4 + D * tn * 2 + 8 * tm * tn * 4 + 6 * tm * HEAD_DIM * 4
    return pl.pallas_call(
        body,
        out_shape=jax.ShapeDtypeStruct((T, 3 * width), out_dtype),
        grid=(3 * tiles, T // tm),
        in_specs=in_specs,
        out_specs=pl.BlockSpec((tm, tn), lambda n, m: (m, n)),
        scratch_shapes=[pltpu.VMEM((D, tn), BF16)],
        compiler_params=_params(("parallel", "arbitrary"), vmem),
        name="qkv_proj",
    )(*args)


def _sb_body(q_ref, k_ref, v_ref, o_ref, acc_ref, carry_ref, *, tq, tk, heads):
    qi = pl.program_id(2)
    ratio = tq // tk
    q = q_ref[...]
    acc_ref[...] = jnp.zeros_like(acc_ref)
    carry_ref[...] = jnp.zeros_like(carry_ref)
    strict_upper = (lax.broadcasted_iota(jnp.int32, (tk, tk), 1)
                    > lax.broadcasted_iota(jnp.int32, (tk, tk), 0)).astype(BF16)
    strict_upper2 = jnp.concatenate([strict_upper, strict_upper], axis=1)

    def block(j, masked):
        start = pl.multiple_of(j * tk, tk)
        if masked:
            kpos = j * tk + lax.broadcasted_iota(jnp.int32, (tk, tq), 0)
            qpos = qi * tq + lax.broadcasted_iota(jnp.int32, (tk, tq), 1)
            mask = kpos < qpos
        heads_sl = [slice(hh * HEAD_DIM, (hh + 1) * HEAD_DIM) for hh in range(heads)]
        scores = [_nt_dot(k_ref[pl.ds(start, tk), sl], q[:, sl]) for sl in heads_sl]
        logs = [-(jnp.maximum(z, 0.0) + jnp.log2(1.0 + jnp.exp2(-jnp.abs(z)))) for z in scores]
        if masked:
            logs = [jnp.where(mask, log_1mb, 0.0) for log_1mb in logs]
        suffixes = [jnp.dot(strict_upper2, jnp.concatenate(_split_bf16(log_1mb, 2), axis=0),
                            preferred_element_type=F32) for log_1mb in logs]
        for hh, sl in enumerate(heads_sl):
            carry = carry_ref[hh]
            w = jnp.exp2(scores[hh] + logs[hh] + suffixes[hh] + carry)
            if masked:
                w = jnp.where(mask, w, 0.0)
            acc_ref[hh] += _tn_dot(v_ref[pl.ds(start, tk), sl], w.astype(BF16))
            carry_ref[hh] = carry + suffixes[hh][0:1, :] + logs[hh][0:1, :]

    for jj in reversed(range(ratio)):
        block(qi * ratio + jj, True)

    def body(i, c):
        block(qi * ratio - 1 - i, False)
        return c

    lax.fori_loop(0, qi * ratio, body, 0)
    for hh in range(heads):
        o_ref[:, hh * HEAD_DIM:(hh + 1) * HEAD_DIM] = acc_ref[hh].T.astype(o_ref.dtype)


def _sb_attention(qkv, B, S, n_heads):
    T = B * S
    tq = _tile(S, 512)
    tk = _tile(S, 256)
    nq = S // tq
    heads = _tile(n_heads, ATTN_HEADS_PER_STEP)
    width = heads * HEAD_DIM
    hb = n_heads // heads
    body = functools.partial(_sb_body, tq=tq, tk=tk, heads=heads)
    vmem = heads * (4 * S * HEAD_DIM * 2 + 4 * tq * HEAD_DIM * 2 + tq * HEAD_DIM * 4 + SUBLANES * tq * 4
                    + 12 * tq * tk * 4)
    return pl.pallas_call(
        body,
        out_shape=jax.ShapeDtypeStruct((T, n_heads * HEAD_DIM), BF16),
        grid=(B, n_heads // heads, nq),
        in_specs=[pl.BlockSpec((tq, width), lambda b, h, i: (b * nq + i, h)),
                  pl.BlockSpec((S, width), lambda b, h, i: (b, hb + h)),
                  pl.BlockSpec((S, width), lambda b, h, i: (b, 2 * hb + h))],
        out_specs=pl.BlockSpec((tq, width), lambda b, h, i: (b * nq + i, h)),
        scratch_shapes=[pltpu.VMEM((heads, HEAD_DIM, tq), F32), pltpu.VMEM((heads, 1, tq), F32)],
        compiler_params=_params(("parallel", "parallel", "parallel"), vmem),
        name="stick_breaking_attention",
    )(qkv, qkv, qkv)


def _dilated_body(q_ref, k_ref, v_ref, *refs, configs, blk):
    n_cfg = len(configs)
    halos, o_ref = refs[:2 * n_cfg], refs[2 * n_cfg]
    part_o, part_lse = refs[2 * n_cfg + 1:3 * n_cfg + 1], refs[3 * n_cfg + 1:]
    span = q_ref.shape[0]
    span_idx = pl.program_id(2)
    row = lax.broadcasted_iota(jnp.int32, (blk, 2 * blk), 0)
    col = lax.broadcasted_iota(jnp.int32, (blk, 2 * blk), 1)
    dist = row + blk - col

    for c, (window, dil) in enumerate(configs):
        kp_ref, vp_ref = halos[2 * c], halos[2 * c + 1]
        band = (dist >= 0) & (dist <= window // dil)
        band_first = band & ((span_idx > 0) | (col >= blk))

        def rows(ref, start, dil=dil):
            return ref[pl.ds(start, blk, stride=dil), :] if dil > 1 else ref[pl.ds(start, blk), :]

        for r in range(dil):
            for j in range(span // (dil * blk)):
                start = r + dil * blk * j
                before = start - dil * blk
                k_prev = rows(k_ref, before) if j else rows(kp_ref, r)
                v_prev = rows(v_ref, before) if j else rows(vp_ref, r)
                kk = jnp.concatenate([k_prev, rows(k_ref, start)], axis=0).astype(BF16)
                vv = jnp.concatenate([v_prev, rows(v_ref, start)], axis=0).astype(BF16)
                s = _nt_dot(rows(q_ref, start).astype(BF16), kk)
                s = jnp.where(band if j else band_first, s, MASK_VALUE)
                m = jnp.max(s, axis=1, keepdims=True)
                p = jnp.exp(s - m)
                l = jnp.sum(p, axis=1, keepdims=True)
                out = jnp.dot(p.astype(BF16), vv, preferred_element_type=F32) / l
                lse = jnp.broadcast_to(m + jnp.log(l), (blk, HEAD_DIM))
                if dil > 1:
                    part_o[c][pl.ds(start, blk, stride=dil), :] = out
                    part_lse[c][pl.ds(start, blk, stride=dil), :] = lse
                else:
                    part_o[c][pl.ds(start, blk), :] = out
                    part_lse[c][pl.ds(start, blk), :] = lse

    top = part_lse[0][...]
    for lse_ref in part_lse[1:]:
        top = jnp.maximum(top, lse_ref[...])
    num = jnp.zeros(o_ref.shape, F32)
    den = jnp.zeros(o_ref.shape, F32)
    for out_ref, lse_ref in zip(part_o, part_lse):
        e = jnp.exp(lse_ref[...] - top)
        num = num + e * out_ref[...]
        den = den + e
    o_ref[...] = (num / den).astype(o_ref.dtype)


def _dilated_attention(qkv, B, S, n_heads):
    T = B * S
    blk = BAND_BLOCK
    span = _tile(S, DILATED_SPAN)
    spans = S // span

    def cur(section):
        return pl.BlockSpec((span, HEAD_DIM), lambda b, h, i: (b * spans + i, section * n_heads + h))

    def halo(section, dil):
        rows = dil * blk
        return pl.BlockSpec((rows, HEAD_DIM), lambda b, h, i: (
            jnp.maximum((b * spans + i) * (span // rows) - 1, 0), section * n_heads + h))

    halo_specs = [halo(section, dil) for _, dil in DILATED_CONFIGS for section in (1, 2)]
    n_cfg = len(DILATED_CONFIGS)
    body = functools.partial(_dilated_body, configs=DILATED_CONFIGS, blk=blk)
    halo_rows = sum(dil * blk for _, dil in DILATED_CONFIGS)
    vmem = (2 * (3 * span + 2 * halo_rows) * HEAD_DIM * 4 + 2 * n_cfg * span * HEAD_DIM * 4
            + 10 * span * HEAD_DIM * 4 + 24 * blk * 2 * blk * 4)
    return pl.pallas_call(
        body,
        out_shape=jax.ShapeDtypeStruct((T, n_heads * HEAD_DIM), BF16),
        grid=(B, n_heads, spans),
        in_specs=[cur(0), cur(1), cur(2)] + halo_specs,
        out_specs=pl.BlockSpec((span, HEAD_DIM), lambda b, h, i: (b * spans + i, h)),
        scratch_shapes=[pltpu.VMEM((span, HEAD_DIM), F32) for _ in range(2 * n_cfg)],
        compiler_params=_params(("parallel", "parallel", "parallel"), vmem),
        name="dilated_attention",
    )(*([qkv] * (3 + 2 * n_cfg)))


def _forget_body(a_ref, w_ref, b_ref, c_ref, carry_ref, *, ts):
    @pl.when(pl.program_id(1) == 0)
    def _():
        carry_ref[...] = jnp.zeros_like(carry_ref)

    x = jnp.dot(a_ref[...], w_ref[...].astype(BF16), preferred_element_type=F32) + b_ref[...]
    log_f = jnp.minimum(x, 0.0) - _softplus_neg_abs(x)
    lower = (lax.broadcasted_iota(jnp.int32, (ts, ts), 0)
             >= lax.broadcasted_iota(jnp.int32, (ts, ts), 1)).astype(BF16)
    cum = carry_ref[...]
    for part in _split_bf16(log_f, 3):
        cum = cum + jnp.dot(lower, part, preferred_element_type=F32)
    c_ref[...] = cum
    carry_ref[...] = cum[ts - 1:ts, :]


def _forget_cumsum(a, w_f, b_f, B, S):
    T, D = a.shape
    H = w_f.shape[1]
    ts = _tile(S, 512)
    ns = S // ts
    body = functools.partial(_forget_body, ts=ts)
    vmem = 4 * ts * D * 2 + 4 * D * 128 * 4 + 8 * ts * ts * 4
    return pl.pallas_call(
        body,
        out_shape=jax.ShapeDtypeStruct((T, H), F32),
        grid=(B, ns),
        in_specs=[pl.BlockSpec((ts, D), lambda b, s: (b * ns + s, 0)),
                  pl.BlockSpec((D, H), lambda b, s: (0, 0)),
                  pl.BlockSpec((1, H), lambda b, s: (0, 0))],
        out_specs=pl.BlockSpec((ts, H), lambda b, s: (b * ns + s, 0)),
        scratch_shapes=[pltpu.VMEM((1, H), F32)],
        compiler_params=_params(("parallel", "arbitrary"), vmem),
        name="forget_cumsum",
    )(a, w_f, b_f.reshape(1, H))


def _gate_lanes(parts, head, first_lane, sign):
    n_heads = parts[0].shape[1]
    row = lax.broadcasted_iota(jnp.int32, (n_heads, HEAD_DIM), 0)
    lane = lax.broadcasted_iota(jnp.int32, (n_heads, HEAD_DIM), 1)
    lane_row = lax.broadcasted_iota(jnp.int32, (1, HEAD_DIM), 1)
    gate_lane = (lane_row >= first_lane) & (lane_row < first_lane + GATE_PARTS)
    out = jnp.where((lane_row < 2 * GATE_PARTS) & jnp.logical_not(gate_lane), 1.0, 0.0)
    for i, part in enumerate(parts):
        place = jnp.where((row == head) & (lane == first_lane + i), sign, 0.0).astype(BF16)
        out = out + jnp.dot(part, place, preferred_element_type=F32)
    return out.astype(BF16)


def _fox_body(q_ref, k_ref, v_ref, c_ref, o_ref, acc_ref, m_ref, l_ref, kx_ref, *, tq, tk, heads):
    hg = pl.program_id(1)
    qi = pl.program_id(2)
    heads_sl = [slice(hh * HEAD_DIM, (hh + 1) * HEAD_DIM) for hh in range(heads)]

    @pl.when(qi == 0)
    def _():
        parts = _split_bf16(c_ref[...] * LOG2_E, GATE_PARTS)
        for hh, sl in enumerate(heads_sl):
            kx_ref[hh, :, :HEAD_DIM] = k_ref[:, sl]
            kx_ref[hh, :, HEAD_DIM:] = _gate_lanes(parts, hg * heads + hh, GATE_PARTS, -1.0)

    q_parts = _split_bf16(c_ref[pl.ds(pl.multiple_of(qi * tq, tq), tq), :] * LOG2_E, GATE_PARTS)
    qx = [jnp.concatenate([q_ref[:, sl], _gate_lanes(q_parts, hg * heads + hh, 0, 1.0)], axis=1)
          for hh, sl in enumerate(heads_sl)]
    acc_ref[...] = jnp.zeros_like(acc_ref)
    l_ref[...] = jnp.zeros_like(l_ref)
    m_ref[...] = jnp.full_like(m_ref, MASK_VALUE)

    def block(j, masked):
        start = pl.multiple_of(j * tk, tk)
        if masked:
            kpos = j * tk + lax.broadcasted_iota(jnp.int32, (tk, tq), 0)
            qpos = qi * tq + lax.broadcasted_iota(jnp.int32, (tk, tq), 1)
            mask = kpos <= qpos
        scores = [_nt_dot(kx_ref[hh, pl.ds(start, tk), :], qx[hh]) for hh in range(heads)]
        for hh, sl in enumerate(heads_sl):
            s = scores[hh]
            if masked:
                s = jnp.where(mask, s, MASK_VALUE)
            m_old = m_ref[hh]
            m_new = jnp.maximum(m_old, jnp.max(s, axis=0, keepdims=True))
            p = jnp.exp2(s - m_new)
            alpha = jnp.exp2(m_old - m_new)
            l_ref[hh] = alpha * l_ref[hh] + jnp.sum(p, axis=0, keepdims=True)
            acc_ref[hh] = alpha * acc_ref[hh] + _tn_dot(v_ref[pl.ds(start, tk), sl], p.astype(BF16))
            m_ref[hh] = m_new

    def body(j, c):
        block(j, False)
        return c

    lax.fori_loop(0, qi, body, 0)
    block(qi, True)
    for hh in range(heads):
        o_ref[:, hh * HEAD_DIM:(hh + 1) * HEAD_DIM] = (acc_ref[hh] / l_ref[hh]).T.astype(o_ref.dtype)


def _fox_attention(qkv, c, B, S, D):
    T = B * S
    H = D // HEAD_DIM
    tq = tk = _tile(S, 512)
    nq = S // tq
    heads = _tile(H, ATTN_HEADS_PER_STEP)
    width = heads * HEAD_DIM
    hb = D // width
    body = functools.partial(_fox_body, tq=tq, tk=tk, heads=heads)
    vmem = (2 * S * 128 * 4 + heads * (4 * S * HEAD_DIM * 2 + 4 * tq * HEAD_DIM * 2 + S * 2 * HEAD_DIM * 2
                                       + tq * HEAD_DIM * 4 + 2 * SUBLANES * tq * 4 + 12 * tq * tk * 4))
    return pl.pallas_call(
        body,
        out_shape=jax.ShapeDtypeStruct((T, D), BF16),
        grid=(B, hb, nq),
        in_specs=[pl.BlockSpec((tq, width), lambda b, h, i: (b * nq + i, h)),
                  pl.BlockSpec((S, width), lambda b, h, i: (b, hb + h)),
                  pl.BlockSpec((S, width), lambda b, h, i: (b, 2 * hb + h)),
                  pl.BlockSpec((S, H), lambda b, h, i: (b, 0))],
        out_specs=pl.BlockSpec((tq, width), lambda b, h, i: (b * nq + i, h)),
        scratch_shapes=[pltpu.VMEM((heads, HEAD_DIM, tq), F32), pltpu.VMEM((heads, 1, tq), F32),
                        pltpu.VMEM((heads, 1, tq), F32), pltpu.VMEM((heads, S, 2 * HEAD_DIM), BF16)],
        compiler_params=_params(("parallel", "parallel", "arbitrary"), vmem),
        name="forgetting_attention",
    )(qkv, qkv, qkv, c)


def _residual_mm_body(*refs, n_parts):
    a_refs = refs[:n_parts]
    w_refs = refs[n_parts:2 * n_parts]
    h_ref, o_ref = refs[2 * n_parts], refs[2 * n_parts + 1]
    wbf_refs = refs[2 * n_parts + 2:]

    @pl.when(pl.program_id(1) == 0)
    def _():
        for w_ref, wbf_ref in zip(w_refs, wbf_refs):
            wbf_ref[...] = w_ref[...].astype(BF16)

    acc = h_ref[...]
    for a_ref, wbf_ref in zip(a_refs, wbf_refs):
        acc = acc + jnp.dot(a_ref[...], wbf_ref[...], preferred_element_type=F32)
    o_ref[...] = acc


def _residual_matmul(h, parts, w, layer, tm_pref=512):
    T, N = h.shape
    n_parts = len(parts)
    kp = parts[0].shape[1]
    tm = _tile(T, tm_pref)
    tn = _tile(N, MATMUL_TN)
    body = functools.partial(_residual_mm_body, n_parts=n_parts)
    a_specs = [pl.BlockSpec((tm, kp), lambda n, m: (m, 0)) for _ in parts]
    w_specs = [pl.BlockSpec((None, kp, tn), lambda n, m, i=i: (layer, i, n), pipeline_mode=pl.Buffered(1))
               for i in range(n_parts)]
    hspec = pl.BlockSpec((tm, tn), lambda n, m: (m, n))
    vmem = n_parts * (2 * tm * kp * 2 + kp * tn * 4 + kp * tn * 2) + 8 * tm * tn * 4
    return pl.pallas_call(
        body,
        out_shape=jax.ShapeDtypeStruct((T, N), F32),
        grid=(N // tn, T // tm),
        in_specs=a_specs + w_specs + [hspec],
        out_specs=hspec,
        scratch_shapes=[pltpu.VMEM((kp, tn), BF16) for _ in parts],
        compiler_params=_params(("parallel", "arbitrary"), vmem),
        name="residual_matmul",
    )(*parts, *([w] * n_parts), h)


def _ffn_up_body(a_ref, wg_ref, wv_ref, cwg_ref, cwv_ref, cbg_ref, cbv_ref, o_ref,
                 wg_bf_ref, wv_bf_ref, u_ref, tails_ref, *, tm, tiles_per_seq):
    m = pl.program_id(1)
    sub = tm // FFN_SUBTILES
    wbf_refs, cw_refs, cb_refs = (wg_bf_ref, wv_bf_ref), (cwg_ref, cwv_ref), (cbg_ref, cbv_ref)

    @pl.when(m == 0)
    def _():
        wg_bf_ref[...] = wg_ref[...].astype(BF16)
        wv_bf_ref[...] = wv_ref[...].astype(BF16)

    @pl.when(m % tiles_per_seq == 0)
    def _():
        tails_ref[0] = jnp.zeros(tails_ref.shape[1:], F32)

    def products(i):
        a = a_ref[i * sub:(i + 1) * sub, :]
        for g in range(2):
            u = jnp.dot(a, wbf_refs[g][...], preferred_element_type=F32)
            u_ref[i % 2, g] = u
            tails_ref[i + 1, g] = u[sub - SUBLANES:, :]

    def conv(i, g):
        u = u_ref[i % 2, g]
        row = lax.broadcasted_iota(jnp.int32, u.shape, 0)
        cw = cw_refs[g][...]
        out = u * cw[CONV_WIDTH - 1:CONV_WIDTH, :] + cb_refs[g][...]
        for back in range(1, CONV_WIDTH):
            shifted = pltpu.roll(u, back, 0)
            for k in range(back):
                prev_row = tails_ref[i, g, SUBLANES - back + k:SUBLANES - back + k + 1, :]
                shifted = jnp.where(row == k, prev_row, shifted)
            out = out + shifted * cw[CONV_WIDTH - 1 - back:CONV_WIDTH - back, :]
        return out

    def gated(i):
        gate, val = conv(i, 0), conv(i, 1)
        o_ref[i * sub:(i + 1) * sub, :] = (gate / (1.0 + jnp.exp(-gate)) * val).astype(o_ref.dtype)

    products(0)
    for i in range(1, FFN_SUBTILES):
        products(i)
        gated(i - 1)
    gated(FFN_SUBTILES - 1)
    tails_ref[0] = tails_ref[FFN_SUBTILES]


def _ffn_up(a, w_up, conv_w, conv_b, layer, S):
    T, D = a.shape
    F = w_up.shape[2] // 2
    tm = _tile(S, 512 * FFN_SUBTILES)
    tn = _tile(F, 512)
    nf = F // tn
    body = functools.partial(_ffn_up_body, tm=tm, tiles_per_seq=S // tm)
    conv_b = conv_b.reshape(conv_b.shape[0], 1, 2 * F)
    sub = tm // FFN_SUBTILES
    vmem = (2 * tm * D * 2 + 2 * (D * tn * 4 + D * tn * 2) + 2 * tm * tn * 2 + 4 * sub * tn * 4
            + 16 * sub * tn * 4)
    return pl.pallas_call(
        body,
        out_shape=jax.ShapeDtypeStruct((T, F), BF16),
        grid=(nf, T // tm),
        in_specs=[pl.BlockSpec((tm, D), lambda n, m: (m, 0)),
                  pl.BlockSpec((None, D, tn), lambda n, m: (layer, 0, n), pipeline_mode=pl.Buffered(1)),
                  pl.BlockSpec((None, D, tn), lambda n, m: (layer, 0, nf + n), pipeline_mode=pl.Buffered(1)),
                  pl.BlockSpec((None, CONV_WIDTH, tn), lambda n, m: (layer, 0, n)),
                  pl.BlockSpec((None, CONV_WIDTH, tn), lambda n, m: (layer, 0, nf + n)),
                  pl.BlockSpec((None, 1, tn), lambda n, m: (layer, 0, n)),
                  pl.BlockSpec((None, 1, tn), lambda n, m: (layer, 0, nf + n))],
        out_specs=pl.BlockSpec((tm, tn), lambda n, m: (m, n)),
        scratch_shapes=[pltpu.VMEM((D, tn), BF16), pltpu.VMEM((D, tn), BF16),
                        pltpu.VMEM((2, 2, sub, tn), F32),
                        pltpu.VMEM((FFN_SUBTILES + 1, 2, SUBLANES, tn), F32)],
        compiler_params=_params(("parallel", "arbitrary"), vmem),
        name="ffn_up_conv_gate",
    )(a, w_up, w_up, conv_w, conv_w, conv_b, conv_b)


def _ple_body(a_ref, wg_ref, p_ref, wp_ref, h_ref, o_ref, wg_bf_ref, wp_bf_ref):
    @pl.when(pl.program_id(1) == 0)
    def _():
        wg_bf_ref[...] = wg_ref[...].astype(BF16)
        wp_bf_ref[...] = wp_ref[...].astype(BF16)

    logits = jnp.dot(a_ref[...], wg_bf_ref[...], preferred_element_type=F32)
    emb = jnp.dot(p_ref[...].astype(BF16), wp_bf_ref[...], preferred_element_type=F32)
    o_ref[...] = h_ref[...] + emb / (1.0 + jnp.exp(-logits))


def _ple(h, a, w_gate, p, w_ple, layer):
    T, D = h.shape
    P = p.shape[2]
    tm = _tile(T, 512)
    tn = _tile(D, MATMUL_TN)
    hspec = pl.BlockSpec((tm, tn), lambda n, m: (m, n))
    vmem = 2 * tm * D * 2 + D * tn * 6 + 2 * tm * P * 4 + P * tn * 6 + 10 * tm * tn * 4
    return pl.pallas_call(
        _ple_body,
        out_shape=jax.ShapeDtypeStruct((T, D), F32),
        grid=(D // tn, T // tm),
        in_specs=[pl.BlockSpec((tm, D), lambda n, m: (m, 0)),
                  pl.BlockSpec((None, D, tn), lambda n, m: (layer, 0, n), pipeline_mode=pl.Buffered(1)),
                  pl.BlockSpec((None, tm, P), lambda n, m: (layer, m, 0)),
                  pl.BlockSpec((None, P, tn), lambda n, m: (layer, 0, n), pipeline_mode=pl.Buffered(1)),
                  hspec],
        out_specs=hspec,
        scratch_shapes=[pltpu.VMEM((D, tn), BF16), pltpu.VMEM((P, tn), BF16)],
        compiler_params=_params(("parallel", "arbitrary"), vmem),
        name="gated_layer_embedding",
    )(a, w_gate, p, w_ple, h)


def kernel(x, p, positions, attn_norm, w_in_even, w_out_even, w_in_odd, b_forget, w_out_odd, ffn_norm,
           w_up, conv_w, conv_b, w_down, ple_norm, w_ple_gate, w_ple, final_norm):
    B, S, D = x.shape
    T = B * S
    depth = p.shape[0]
    H = D // HEAD_DIM
    n_sb = H // 2
    n_dil = H - n_sb
    h = x.reshape(T, D)
    p = p.reshape(depth, T, p.shape[-1])
    rope_tabs = _rope_tables(positions)
    for i in range(depth):
        a = _rmsnorm(h, attn_norm[i], BF16)
        if i % 2 == 0:
            scale = HEAD_DIM ** -0.5
            qkv_sb = _qkv_proj(a, w_in_even, i // 2, LOG2_E * scale, 0, n_sb, BF16)
            qkv_dil = _qkv_proj(a, w_in_even, i // 2, scale, n_sb, n_dil, F32, rope_tabs)
            sb = _sb_attention(qkv_sb, B, S, n_sb)
            dl = _dilated_attention(qkv_dil, B, S, n_dil)
            h = _residual_matmul(h, [sb, dl], w_out_even, i // 2)
        else:
            qkv = _qkv_proj(a, w_in_odd, i // 2, LOG2_E * HEAD_DIM ** -0.5, 0, H, BF16)
            c = _forget_cumsum(a, w_in_odd[i // 2, :, 3 * D:], b_forget[i // 2], B, S)
            o = _fox_attention(qkv, c, B, S, D)
            h = _residual_matmul(h, [o], w_out_odd, i // 2)
        act = _ffn_up(_rmsnorm(h, ffn_norm[i], BF16), w_up, conv_w, conv_b, i, S)
        h = _residual_matmul(h, [act], w_down, i, tm_pref=256)
        h = _ple(h, _rmsnorm(h, ple_norm[i], BF16), w_ple_gate, p, w_ple, i)
    return _rmsnorm(h, final_norm, F32).reshape(B, S, D)
```

```python
import functools

import jax
import jax.numpy as jnp
from jax import lax
from jax.experimental import pallas as pl
from jax.experimental.pallas import tpu as pltpu

HEAD_DIM = 128
ROPE_DIM = HEAD_DIM // 4
ROPE_THETA = 500000.0
DILATED_CONFIGS = ((128, 1), (512, 4), (2048, 16))
BAND_BLOCK = 128
DILATED_SPAN = 2048
RMS_EPS = 1e-6
CONV_WIDTH = 3
MASK_VALUE = -1e30
ATTN_HEADS_PER_STEP = 4
LOG2_E = 1.4426950408889634
GATE_PARTS = 3
SB_SUFFIX_PARTS = 1
MATMUL_TN = 1024
FFN_SUBTILES = 4
SUBLANES = 8
VMEM_CAP_BYTES = 60 * 1024 * 1024

F32 = jnp.float32
BF16 = jnp.bfloat16


def _params(semantics, vmem_bytes):
    return pltpu.CompilerParams(dimension_semantics=semantics,
                                vmem_limit_bytes=int(min(vmem_bytes, VMEM_CAP_BYTES)))


def _tile(dim, pref):
    t = min(pref, dim)
    while dim % t:
        t //= 2
    return t


def _nt_dot(a, b):
    return lax.dot_general(a, b, (((1,), (1,)), ((), ())), preferred_element_type=F32)


def _tn_dot(a, b):
    return lax.dot_general(a, b, (((0,), (0,)), ((), ())), preferred_element_type=F32)


def _softplus_neg_abs(z):
    return jnp.log1p(jnp.exp(-jnp.abs(z)))


def _split_bf16(x, parts):
    out = []
    for _ in range(parts - 1):
        hi = x.astype(BF16)
        out.append(hi)
        x = x - hi.astype(F32)
    out.append(x.astype(BF16))
    return out


def _rmsnorm_body(h_ref, g_ref, o_ref):
    x = h_ref[...]
    ms = jnp.mean(x * x, axis=-1, keepdims=True)
    o_ref[...] = (x * lax.rsqrt(ms + RMS_EPS) * g_ref[...]).astype(o_ref.dtype)


def _rmsnorm(h, g, out_dtype):
    T, D = h.shape
    tm = _tile(T, 512)
    blk = tm * D * 4
    return pl.pallas_call(
        _rmsnorm_body,
        out_shape=jax.ShapeDtypeStruct((T, D), out_dtype),
        grid=(T // tm,),
        in_specs=[pl.BlockSpec((tm, D), lambda m: (m, 0)),
                  pl.BlockSpec((1, D), lambda m: (0, 0))],
        out_specs=pl.BlockSpec((tm, D), lambda m: (m, 0)),
        compiler_params=_params(("parallel",), 6 * blk),
        name="rmsnorm",
    )(h, g.reshape(1, D))


def _rope_table_body(pos_ref, invf_ref, cos_ref, sa_ref, sb_ref):
    ang = pos_ref[...].astype(F32) * invf_ref[...]
    lane = lax.broadcasted_iota(jnp.int32, ang.shape, 1)
    half = ROPE_DIM // 2
    s = jnp.sin(ang)
    cos_ref[...] = jnp.cos(ang)
    sa_ref[...] = jnp.where((lane >= half) & (lane < ROPE_DIM), s, 0.0)
    sb_ref[...] = jnp.where(lane < half, -s, 0.0)


def _rope_tables(positions):
    T = positions.size
    tm = _tile(T, 512)
    half = ROPE_DIM // 2
    inv_freq = ROPE_THETA ** (-jnp.arange(0, ROPE_DIM, 2, dtype=F32) / ROPE_DIM)
    invf = jnp.concatenate([inv_freq, inv_freq, jnp.zeros((HEAD_DIM - 2 * half,), F32)]).reshape(1, HEAD_DIM)
    tab = jax.ShapeDtypeStruct((T, HEAD_DIM), F32)
    spec = pl.BlockSpec((tm, HEAD_DIM), lambda m: (m, 0))
    return pl.pallas_call(
        _rope_table_body,
        out_shape=(tab, tab, tab),
        grid=(T // tm,),
        in_specs=[pl.BlockSpec((tm, 1), lambda m: (m, 0)),
                  pl.BlockSpec((1, HEAD_DIM), lambda m: (0, 0))],
        out_specs=(spec, spec, spec),
        compiler_params=_params(("parallel",), 32 * tm * HEAD_DIM * 4),
        name="rope_tables",
    )(positions.reshape(T, 1), invf)


def _qkv_body(*refs, rope, scale, tiles_per_section, heads_per_tile):
    if rope:
        a_ref, w_ref, cos_ref, sa_ref, sb_ref, o_ref, wbf_ref = refs
    else:
        a_ref, w_ref, o_ref, wbf_ref = refs
    n = pl.program_id(0)

    @pl.when(pl.program_id(1) == 0)
    def _():
        wbf_ref[...] = w_ref[...].astype(BF16)

    acc = jnp.dot(a_ref[...], wbf_ref[...], preferred_element_type=F32)
    acc = acc * jnp.where(n < tiles_per_section, scale, 1.0).astype(F32)
    if not rope:
        o_ref[...] = acc.astype(o_ref.dtype)
        return

    @pl.when(n < 2 * tiles_per_section)
    def _():
        c, sa, sb = cos_ref[...], sa_ref[...], sb_ref[...]
        half = ROPE_DIM // 2
        for j in range(heads_per_tile):
            x = acc[:, j * HEAD_DIM:(j + 1) * HEAD_DIM]
            y = x * c + pltpu.roll(x, half, 1) * sa + pltpu.roll(x, HEAD_DIM - half, 1) * sb
            o_ref[:, j * HEAD_DIM:(j + 1) * HEAD_DIM] = y.astype(o_ref.dtype)

    @pl.when(n >= 2 * tiles_per_section)
    def _():
        o_ref[...] = acc.astype(o_ref.dtype)


def _qkv_proj(a, w, layer, q_scale, first_head, n_heads, out_dtype, rope_tabs=None):
    T, D = a.shape
    rope = rope_tabs is not None
    width = n_heads * HEAD_DIM
    tm = _tile(T, 512)
    tn = _tile(width, MATMUL_TN)
    tiles = width // tn
    per_section, first = D // tn, first_head * HEAD_DIM // tn
    body = functools.partial(_qkv_body, rope=rope, scale=q_scale, tiles_per_section=tiles,
                             heads_per_tile=tn // HEAD_DIM)
    in_specs = [pl.BlockSpec((tm, D), lambda n, m: (m, 0)),
                pl.BlockSpec((None, D, tn), lambda n, m: (layer, 0, (n // tiles) * per_section + first + n % tiles))]
    args = [a, w]
    if rope:
        in_specs += [pl.BlockSpec((tm, HEAD_DIM), lambda n, m: (m, 0))] * 3
        args += list(rope_tabs)
    vmem = 2 * tm * D * 2 + 2 * D * tn * 4 + D * tn * 2 + 8 * tm * tn * 4 + 6 * tm * HEAD_DIM * 4
    return pl.pallas_call(
        body,
        out_shape=jax.ShapeDtypeStruct((T, 3 * width), out_dtype),
        grid=(3 * tiles, T // tm),
        in_specs=in_specs,
        out_specs=pl.BlockSpec((tm, tn), lambda n, m: (m, n)),
        scratch_shapes=[pltpu.VMEM((D, tn), BF16)],
        compiler_params=_params(("parallel", "arbitrary"), vmem),
        name="qkv_proj",
    )(*args)


def _sb_body(q_ref, k_ref, v_ref, o_ref, acc_ref, carry_ref, *, tq, tk, heads):
    qi = pl.program_id(2)
    ratio = tq // tk
    q = q_ref[...]
    acc_ref[...] = jnp.zeros_like(acc_ref)
    carry_ref[...] = jnp.zeros_like(carry_ref)
    strict_upper = (lax.broadcasted_iota(jnp.int32, (tk, tk), 1)
                    > lax.broadcasted_iota(jnp.int32, (tk, tk), 0)).astype(BF16)
    strict_upper_n = jnp.concatenate([strict_upper] * SB_SUFFIX_PARTS, axis=1)

    def block(j, masked):
        start = pl.multiple_of(j * tk, tk)
        if masked:
            kpos = j * tk + lax.broadcasted_iota(jnp.int32, (tk, tq), 0)
            qpos = qi * tq + lax.broadcasted_iota(jnp.int32, (tk, tq), 1)
            mask = kpos < qpos
        heads_sl = [slice(hh * HEAD_DIM, (hh + 1) * HEAD_DIM) for hh in range(heads)]
        scores = [_nt_dot(k_ref[pl.ds(start, tk), sl], q[:, sl]) for sl in heads_sl]
        costs = [jnp.maximum(z, 0.0) + jnp.log2(1.0 + jnp.exp2(-jnp.abs(z))) for z in scores]
        if masked:
            costs = [jnp.where(mask, cost, 0.0) for cost in costs]
        suffixes = [jnp.dot(strict_upper_n, jnp.concatenate(_split_bf16(cost, SB_SUFFIX_PARTS), axis=0),
                            preferred_element_type=F32) for cost in costs]
        for hh, sl in enumerate(heads_sl):
            carry = carry_ref[hh]
            w = jnp.exp2(scores[hh] - costs[hh] - suffixes[hh] - carry)
            if masked:
                w = jnp.where(mask, w, 0.0)
            acc_ref[hh] += _tn_dot(v_ref[pl.ds(start, tk), sl], w.astype(BF16))
            carry_ref[hh] = carry + suffixes[hh][0:1, :] + costs[hh][0:1, :]

    for jj in reversed(range(ratio)):
        block(qi * ratio + jj, True)

    def body(i, c):
        block(qi * ratio - 1 - i, False)
        return c

    lax.fori_loop(0, qi * ratio, body, 0)
    for hh in range(heads):
        o_ref[:, hh * HEAD_DIM:(hh + 1) * HEAD_DIM] = acc_ref[hh].T.astype(o_ref.dtype)


def _sb_attention(qkv, B, S, n_heads):
    T = B * S
    tq = _tile(S, 512)
    tk = _tile(S, 256)
    nq = S // tq
    heads = _tile(n_heads, ATTN_HEADS_PER_STEP)
    width = heads * HEAD_DIM
    hb = n_heads // heads
    body = functools.partial(_sb_body, tq=tq, tk=tk, heads=heads)
    vmem = heads * (4 * S * HEAD_DIM * 2 + 4 * tq * HEAD_DIM * 2 + tq * HEAD_DIM * 4 + SUBLANES * tq * 4
                    + 12 * tq * tk * 4)
    return pl.pallas_call(
        body,
        out_shape=jax.ShapeDtypeStruct((T, n_heads * HEAD_DIM), BF16),
        grid=(B, n_heads // heads, nq),
        in_specs=[pl.BlockSpec((tq, width), lambda b, h, i: (b * nq + i, h)),
                  pl.BlockSpec((S, width), lambda b, h, i: (b, hb + h)),
                  pl.BlockSpec((S, width), lambda b, h, i: (b, 2 * hb + h))],
        out_specs=pl.BlockSpec((tq, width), lambda b, h, i: (b * nq + i, h)),
        scratch_shapes=[pltpu.VMEM((heads, HEAD_DIM, tq), F32), pltpu.VMEM((heads, 1, tq), F32)],
        compiler_params=_params(("parallel", "parallel", "parallel"), vmem),
        name="stick_breaking_attention",
    )(qkv, qkv, qkv)


def _dilated_body(q_ref, k_ref, v_ref, *refs, configs, blk):
    n_cfg = len(configs)
    halos, o_ref = refs[:2 * n_cfg], refs[2 * n_cfg]
    part_o, part_lse = refs[2 * n_cfg + 1:3 * n_cfg + 1], refs[3 * n_cfg + 1:]
    span = q_ref.shape[0]
    span_idx = pl.program_id(2)
    row = lax.broadcasted_iota(jnp.int32, (blk, 2 * blk), 0)
    col = lax.broadcasted_iota(jnp.int32, (blk, 2 * blk), 1)
    dist = row + blk - col

    for c, (window, dil) in enumerate(configs):
        kp_ref, vp_ref = halos[2 * c], halos[2 * c + 1]
        band = (dist >= 0) & (dist <= window // dil)
        band_first = band & ((span_idx > 0) | (col >= blk))

        def rows(ref, start, dil=dil):
            return ref[pl.ds(start, blk, stride=dil), :] if dil > 1 else ref[pl.ds(start, blk), :]

        for r in range(dil):
            k_prev, v_prev = rows(kp_ref, r).astype(BF16), rows(vp_ref, r).astype(BF16)
            for j in range(span // (dil * blk)):
                start = r + dil * blk * j
                k_run, v_run = rows(k_ref, start).astype(BF16), rows(v_ref, start).astype(BF16)
                kk = jnp.concatenate([k_prev, k_run], axis=0)
                vv = jnp.concatenate([v_prev, v_run], axis=0)
                k_prev, v_prev = k_run, v_run
                s = _nt_dot(rows(q_ref, start).astype(BF16), kk)
                s = jnp.where(band if j else band_first, s, MASK_VALUE)
                m = jnp.max(s, axis=1, keepdims=True)
                p = jnp.exp(s - m)
                l = jnp.sum(p, axis=1, keepdims=True)
                out = jnp.dot(p.astype(BF16), vv, preferred_element_type=F32) / l
                lse = jnp.broadcast_to(m + jnp.log(l), (blk, HEAD_DIM))
                if dil > 1:
                    part_o[c][pl.ds(start, blk, stride=dil), :] = out
                    part_lse[c][pl.ds(start, blk, stride=dil), :] = lse
                else:
                    part_o[c][pl.ds(start, blk), :] = out
                    part_lse[c][pl.ds(start, blk), :] = lse

    top = part_lse[0][...]
    for lse_ref in part_lse[1:]:
        top = jnp.maximum(top, lse_ref[...])
    num = jnp.zeros(o_ref.shape, F32)
    den = jnp.zeros(o_ref.shape, F32)
    for out_ref, lse_ref in zip(part_o, part_lse):
        e = jnp.exp(lse_ref[...] - top)
        num = num + e * out_ref[...]
        den = den + e
    o_ref[...] = (num / den).astype(o_ref.dtype)


def _dilated_attention(qkv, B, S, n_heads):
    T = B * S
    blk = BAND_BLOCK
    span = _tile(S, DILATED_SPAN)
    spans = S // span

    def cur(section):
        return pl.BlockSpec((span, HEAD_DIM), lambda b, h, i: (b * spans + i, section * n_heads + h))

    def halo(section, dil):
        rows = dil * blk
        return pl.BlockSpec((rows, HEAD_DIM), lambda b, h, i: (
            jnp.maximum((b * spans + i) * (span // rows) - 1, 0), section * n_heads + h))

    halo_specs = [halo(section, dil) for _, dil in DILATED_CONFIGS for section in (1, 2)]
    n_cfg = len(DILATED_CONFIGS)
    body = functools.partial(_dilated_body, configs=DILATED_CONFIGS, blk=blk)
    halo_rows = sum(dil * blk for _, dil in DILATED_CONFIGS)
    vmem = (2 * (3 * span + 2 * halo_rows) * HEAD_DIM * 4 + 2 * n_cfg * span * HEAD_DIM * 4
            + 10 * span * HEAD_DIM * 4 + 24 * blk * 2 * blk * 4)
    return pl.pallas_call(
        body,
        out_shape=jax.ShapeDtypeStruct((T, n_heads * HEAD_DIM), BF16),
        grid=(B, n_heads, spans),
        in_specs=[cur(0), cur(1), cur(2)] + halo_specs,
        out_specs=pl.BlockSpec((span, HEAD_DIM), lambda b, h, i: (b * spans + i, h)),
        scratch_shapes=[pltpu.VMEM((span, HEAD_DIM), F32) for _ in range(2 * n_cfg)],
        compiler_params=_params(("parallel", "parallel", "parallel"), vmem),
        name="dilated_attention",
    )(*([qkv] * (3 + 2 * n_cfg)))


def _forget_body(a_ref, w_ref, b_ref, c_ref, carry_ref, *, ts):
    @pl.when(pl.program_id(1) == 0)
    def _():
        carry_ref[...] = jnp.zeros_like(carry_ref)

    x = jnp.dot(a_ref[...], w_ref[...].astype(BF16), preferred_element_type=F32) + b_ref[...]
    log_f = jnp.minimum(x, 0.0) - _softplus_neg_abs(x)
    lower = (lax.broadcasted_iota(jnp.int32, (ts, ts), 0)
             >= lax.broadcasted_iota(jnp.int32, (ts, ts), 1)).astype(BF16)
    cum = carry_ref[...]
    for part in _split_bf16(log_f, 3):
        cum = cum + jnp.dot(lower, part, preferred_element_type=F32)
    c_ref[...] = cum
    carry_ref[...] = cum[ts - 1:ts, :]


def _forget_cumsum(a, w_f, b_f, B, S):
    T, D = a.shape
    H = w_f.shape[1]
    ts = _tile(S, 512)
    ns = S // ts
    body = functools.partial(_forget_body, ts=ts)
    vmem = 4 * ts * D * 2 + 4 * D * 128 * 4 + 8 * ts * ts * 4
    return pl.pallas_call(
        body,
        out_shape=jax.ShapeDtypeStruct((T, H), F32),
        grid=(B, ns),
        in_specs=[pl.BlockSpec((ts, D), lambda b, s: (b * ns + s, 0)),
                  pl.BlockSpec((D, H), lambda b, s: (0, 0)),
                  pl.BlockSpec((1, H), lambda b, s: (0, 0))],
        out_specs=pl.BlockSpec((ts, H), lambda b, s: (b * ns + s, 0)),
        scratch_shapes=[pltpu.VMEM((1, H), F32)],
        compiler_params=_params(("parallel", "arbitrary"), vmem),
        name="forget_cumsum",
    )(a, w_f, b_f.reshape(1, H))


def _gate_lanes(parts, head, first_lane, sign):
    n_heads = parts[0].shape[1]
    row = lax.broadcasted_iota(jnp.int32, (n_heads, HEAD_DIM), 0)
    lane = lax.broadcasted_iota(jnp.int32, (n_heads, HEAD_DIM), 1)
    lane_row = lax.broadcasted_iota(jnp.int32, (1, HEAD_DIM), 1)
    gate_lane = (lane_row >= first_lane) & (lane_row < first_lane + GATE_PARTS)
    out = jnp.where((lane_row < 2 * GATE_PARTS) & jnp.logical_not(gate_lane), 1.0, 0.0)
    for i, part in enumerate(parts):
        place = jnp.where((row == head) & (lane == first_lane + i), sign, 0.0).astype(BF16)
        out = out + jnp.dot(part, place, preferred_element_type=F32)
    return out.astype(BF16)


def _fox_body(q_ref, k_ref, v_ref, c_ref, o_ref, acc_ref, m_ref, l_ref, kx_ref, *, tq, tk, heads):
    hg = pl.program_id(1)
    qi = pl.program_id(2)
    heads_sl = [slice(hh * HEAD_DIM, (hh + 1) * HEAD_DIM) for hh in range(heads)]

    @pl.when(qi == 0)
    def _():
        parts = _split_bf16(c_ref[...] * LOG2_E, GATE_PARTS)
        for hh, sl in enumerate(heads_sl):
            kx_ref[hh, :, :HEAD_DIM] = k_ref[:, sl]
            kx_ref[hh, :, HEAD_DIM:] = _gate_lanes(parts, hg * heads + hh, GATE_PARTS, -1.0)

    q_parts = _split_bf16(c_ref[pl.ds(pl.multiple_of(qi * tq, tq), tq), :] * LOG2_E, GATE_PARTS)
    qx = [jnp.concatenate([q_ref[:, sl], _gate_lanes(q_parts, hg * heads + hh, 0, 1.0)], axis=1)
          for hh, sl in enumerate(heads_sl)]
    acc_ref[...] = jnp.zeros_like(acc_ref)
    l_ref[...] = jnp.zeros_like(l_ref)
    m_ref[...] = jnp.full_like(m_ref, MASK_VALUE)

    def block(j, masked):
        start = pl.multiple_of(j * tk, tk)
        if masked:
            kpos = j * tk + lax.broadcasted_iota(jnp.int32, (tk, tq), 0)
            qpos = qi * tq + lax.broadcasted_iota(jnp.int32, (tk, tq), 1)
            mask = kpos <= qpos
        scores = [_nt_dot(kx_ref[hh, pl.ds(start, tk), :], qx[hh]) for hh in range(heads)]
        for hh, sl in enumerate(heads_sl):
            s = scores[hh]
            if masked:
                s = jnp.where(mask, s, MASK_VALUE)
            m_old = m_ref[hh]
            m_new = jnp.maximum(m_old, jnp.max(s, axis=0, keepdims=True))
            p = jnp.exp2(s - m_new)
            alpha = jnp.exp2(m_old - m_new)
            l_ref[hh] = alpha * l_ref[hh] + jnp.sum(p, axis=0, keepdims=True)
            acc_ref[hh] = alpha * acc_ref[hh] + _tn_dot(v_ref[pl.ds(start, tk), sl], p.astype(BF16))
            m_ref[hh] = m_new

    def body(j, c):
        block(j, False)
        return c

    lax.fori_loop(0, qi, body, 0)
    block(qi, True)
    for hh in range(heads):
        o_ref[:, hh * HEAD_DIM:(hh + 1) * HEAD_DIM] = (acc_ref[hh] / l_ref[hh]).T.astype(o_ref.dtype)


def _fox_attention(qkv, c, B, S, D):
    T = B * S
    H = D // HEAD_DIM
    tq = tk = _tile(S, 512)
    nq = S // tq
    heads = _tile(H, ATTN_HEADS_PER_STEP)
    width = heads * HEAD_DIM
    hb = D // width
    body = functools.partial(_fox_body, tq=tq, tk=tk, heads=heads)
    vmem = (2 * S * 128 * 4 + heads * (4 * S * HEAD_DIM * 2 + 4 * tq * HEAD_DIM * 2 + S * 2 * HEAD_DIM * 2
                                       + tq * HEAD_DIM * 4 + 2 * SUBLANES * tq * 4 + 12 * tq * tk * 4))
    return pl.pallas_call(
        body,
        out_shape=jax.ShapeDtypeStruct((T, D), BF16),
        grid=(B, hb, nq),
        in_specs=[pl.BlockSpec((tq, width), lambda b, h, i: (b * nq + i, h)),
                  pl.BlockSpec((S, width), lambda b, h, i: (b, hb + h)),
                  pl.BlockSpec((S, width), lambda b, h, i: (b, 2 * hb + h)),
                  pl.BlockSpec((S, H), lambda b, h, i: (b, 0))],
        out_specs=pl.BlockSpec((tq, width), lambda b, h, i: (b * nq + i, h)),
        scratch_shapes=[pltpu.VMEM((heads, HEAD_DIM, tq), F32), pltpu.VMEM((heads, 1, tq), F32),
                        pltpu.VMEM((heads, 1, tq), F32), pltpu.VMEM((heads, S, 2 * HEAD_DIM), BF16)],
        compiler_params=_params(("parallel", "parallel", "arbitrary"), vmem),
        name="forgetting_attention",
    )(qkv, qkv, qkv, c)


def _residual_mm_body(*refs, n_parts):
    a_refs = refs[:n_parts]
    w_refs = refs[n_parts:2 * n_parts]
    h_ref, o_ref = refs[2 * n_parts], refs[2 * n_parts + 1]
    wbf_refs = refs[2 * n_parts + 2:]

    @pl.when(pl.program_id(1) == 0)
    def _():
        for w_ref, wbf_ref in zip(w_refs, wbf_refs):
            wbf_ref[...] = w_ref[...].astype(BF16)

    acc = h_ref[...]
    for a_ref, wbf_ref in zip(a_refs, wbf_refs):
        acc = acc + jnp.dot(a_ref[...], wbf_ref[...], preferred_element_type=F32)
    o_ref[...] = acc


def _residual_matmul(h, parts, w, layer, tm_pref=512):
    T, N = h.shape
    n_parts = len(parts)
    kp = parts[0].shape[1]
    tm = _tile(T, tm_pref)
    tn = _tile(N, MATMUL_TN)
    body = functools.partial(_residual_mm_body, n_parts=n_parts)
    a_specs = [pl.BlockSpec((tm, kp), lambda n, m: (m, 0)) for _ in parts]
    w_specs = [pl.BlockSpec((None, kp, tn), lambda n, m, i=i: (layer, i, n), pipeline_mode=pl.Buffered(1))
               for i in range(n_parts)]
    hspec = pl.BlockSpec((tm, tn), lambda n, m: (m, n))
    vmem = n_parts * (2 * tm * kp * 2 + kp * tn * 4 + kp * tn * 2) + 8 * tm * tn * 4
    return pl.pallas_call(
        body,
        out_shape=jax.ShapeDtypeStruct((T, N), F32),
        grid=(N // tn, T // tm),
        in_specs=a_specs + w_specs + [hspec],
        out_specs=hspec,
        scratch_shapes=[pltpu.VMEM((kp, tn), BF16) for _ in parts],
        compiler_params=_params(("parallel", "arbitrary"), vmem),
        name="residual_matmul",
    )(*parts, *([w] * n_parts), h)


def _ffn_up_body(a_ref, wg_ref, wv_ref, cwg_ref, cwv_ref, cbg_ref, cbv_ref, o_ref,
                 wg_bf_ref, wv_bf_ref, u_ref, tails_ref, *, tm, tiles_per_seq):
    m = pl.program_id(1)
    sub = tm // FFN_SUBTILES
    wbf_refs, cw_refs, cb_refs = (wg_bf_ref, wv_bf_ref), (cwg_ref, cwv_ref), (cbg_ref, cbv_ref)

    @pl.when(m == 0)
    def _():
        wg_bf_ref[...] = wg_ref[...].astype(BF16)
        wv_bf_ref[...] = wv_ref[...].astype(BF16)

    @pl.when(m % tiles_per_seq == 0)
    def _():
        tails_ref[0] = jnp.zeros(tails_ref.shape[1:], F32)

    def products(i):
        a = a_ref[i * sub:(i + 1) * sub, :]
        for g in range(2):
            u = jnp.dot(a, wbf_refs[g][...], preferred_element_type=F32)
            u_ref[i % 2, g] = u
            tails_ref[i + 1, g] = u[sub - SUBLANES:, :]

    def conv(i, g):
        u = u_ref[i % 2, g]
        row = lax.broadcasted_iota(jnp.int32, (SUBLANES, u.shape[1]), 0)
        cw = cw_refs[g][...]
        out = u * cw[CONV_WIDTH - 1:CONV_WIDTH, :] + cb_refs[g][...]
        for back in range(1, CONV_WIDTH):
            shifted = pltpu.roll(u, back, 0)
            top = shifted[:SUBLANES]
            for k in range(back):
                prev_row = tails_ref[i, g, SUBLANES - back + k:SUBLANES - back + k + 1, :]
                top = jnp.where(row == k, prev_row, top)
            shifted = jnp.concatenate([top, shifted[SUBLANES:]], axis=0)
            out = out + shifted * cw[CONV_WIDTH - 1 - back:CONV_WIDTH - back, :]
        return out

    def gated(i):
        gate, val = conv(i, 0), conv(i, 1)
        silu = gate / (1.0 + jnp.exp2(gate * -LOG2_E))
        o_ref[i * sub:(i + 1) * sub, :] = (silu * val).astype(o_ref.dtype)

    products(0)
    for i in range(1, FFN_SUBTILES):
        products(i)
        gated(i - 1)
    gated(FFN_SUBTILES - 1)
    tails_ref[0] = tails_ref[FFN_SUBTILES]


def _ffn_up(a, w_up, conv_w, conv_b, layer, S):
    T, D = a.shape
    F = w_up.shape[2] // 2
    tm = _tile(S, 512 * FFN_SUBTILES)
    tn = _tile(F, 512)
    nf = F // tn
    body = functools.partial(_ffn_up_body, tm=tm, tiles_per_seq=S // tm)
    conv_b = conv_b.reshape(conv_b.shape[0], 1, 2 * F)
    sub = tm // FFN_SUBTILES
    vmem = (2 * tm * D * 2 + 2 * (D * tn * 4 + D * tn * 2) + 2 * tm * tn * 2 + 4 * sub * tn * 4
            + 16 * sub * tn * 4)
    return pl.pallas_call(
        body,
        out_shape=jax.ShapeDtypeStruct((T, F), BF16),
        grid=(nf, T // tm),
        in_specs=[pl.BlockSpec((tm, D), lambda n, m: (m, 0)),
                  pl.BlockSpec((None, D, tn), lambda n, m: (layer, 0, n), pipeline_mode=pl.Buffered(1)),
                  pl.BlockSpec((None, D, tn), lambda n, m: (layer, 0, nf + n), pipeline_mode=pl.Buffered(1)),
                  pl.BlockSpec((None, CONV_WIDTH, tn), lambda n, m: (layer, 0, n)),
                  pl.BlockSpec((None, CONV_WIDTH, tn), lambda n, m: (layer, 0, nf + n)),
                  pl.BlockSpec((None, 1, tn), lambda n, m: (layer, 0, n)),
                  pl.BlockSpec((None, 1, tn), lambda n, m: (layer, 0, nf + n))],
        out_specs=pl.BlockSpec((tm, tn), lambda n, m: (m, n)),
        scratch_shapes=[pltpu.VMEM((D, tn), BF16), pltpu.VMEM((D, tn), BF16),
                        pltpu.VMEM((2, 2, sub, tn), F32),
                        pltpu.VMEM((FFN_SUBTILES + 1, 2, SUBLANES, tn), F32)],
        compiler_params=_params(("parallel", "arbitrary"), vmem),
        name="ffn_up_conv_gate",
    )(a, w_up, w_up, conv_w, conv_w, conv_b, conv_b)


def _ple_body(a_ref, wg_ref, p_ref, wp_ref, h_ref, o_ref, wg_bf_ref, wp_bf_ref):
    @pl.when(pl.program_id(1) == 0)
    def _():
        wg_bf_ref[...] = wg_ref[...].astype(BF16)
        wp_bf_ref[...] = wp_ref[...].astype(BF16)

    logits = jnp.dot(a_ref[...], wg_bf_ref[...], preferred_element_type=F32)
    emb = jnp.dot(p_ref[...].astype(BF16), wp_bf_ref[...], preferred_element_type=F32)
    o_ref[...] = h_ref[...] + emb / (1.0 + jnp.exp(-logits))


def _ple(h, a, w_gate, p, w_ple, layer):
    T, D = h.shape
    P = p.shape[2]
    tm = _tile(T, 512)
    tn = _tile(D, MATMUL_TN)
    hspec = pl.BlockSpec((tm, tn), lambda n, m: (m, n))
    vmem = 2 * tm * D * 2 + D * tn * 6 + 2 * tm * P * 4 + P * tn * 6 + 10 * tm * tn * 4
    return pl.pallas_call(
        _ple_body,
        out_shape=jax.ShapeDtypeStruct((T, D), F32),
        grid=(D // tn, T // tm),
        in_specs=[pl.BlockSpec((tm, D), lambda n, m: (m, 0)),
                  pl.BlockSpec((None, D, tn), lambda n, m: (layer, 0, n), pipeline_mode=pl.Buffered(1)),
                  pl.BlockSpec((None, tm, P), lambda n, m: (layer, m, 0)),
                  pl.BlockSpec((None, P, tn), lambda n, m: (layer, 0, n), pipeline_mode=pl.Buffered(1)),
                  hspec],
        out_specs=hspec,
        scratch_shapes=[pltpu.VMEM((D, tn), BF16), pltpu.VMEM((P, tn), BF16)],
        compiler_params=_params(("parallel", "arbitrary"), vmem),
        name="gated_layer_embedding",
    )(a, w_gate, p, w_ple, h)


def kernel(x, p, positions, attn_norm, w_in_even, w_out_even, w_in_odd, b_forget, w_out_odd, ffn_norm,
           w_up, conv_w, conv_b, w_down, ple_norm, w_ple_gate, w_ple, final_norm):
    B, S, D = x.shape
    T = B * S
    depth = p.shape[0]
    H = D // HEAD_DIM
    n_sb = H // 2
    n_dil = H - n_sb
    h = x.reshape(T, D)
    p = p.reshape(depth, T, p.shape[-1])
    rope_tabs = _rope_tables(positions)
    for i in range(depth):
        a = _rmsnorm(h, attn_norm[i], BF16)
        if i % 2 == 0:
            scale = HEAD_DIM ** -0.5
            qkv_sb = _qkv_proj(a, w_in_even, i // 2, LOG2_E * scale, 0, n_sb, BF16)
            qkv_dil = _qkv_proj(a, w_in_even, i // 2, scale, n_sb, n_dil, F32, rope_tabs)
            sb = _sb_attention(qkv_sb, B, S, n_sb)
            dl = _dilated_attention(qkv_dil, B, S, n_dil)
            h = _residual_matmul(h, [sb, dl], w_out_even, i // 2)
        else:
            qkv = _qkv_proj(a, w_in_odd, i // 2, LOG2_E * HEAD_DIM ** -0.5, 0, H, BF16)
            c = _forget_cumsum(a, w_in_odd[i // 2, :, 3 * D:], b_forget[i // 2], B, S)
            o = _fox_attention(qkv, c, B, S, D)
            h = _residual_matmul(h, [o], w_out_odd, i // 2)
        act = _ffn_up(_rmsnorm(h, ffn_norm[i], BF16), w_up, conv_w, conv_b, i, S)
        h = _residual_matmul(h, [act], w_down, i, tm_pref=256)
        h = _ple(h, _rmsnorm(h, ple_norm[i], BF16), w_ple_gate, p, w_ple, i)
    return _rmsnorm(h, final_norm, F32).reshape(B, S, D)
```

```python
import functools

import jax
import jax.numpy as jnp
from jax import lax
from jax.experimental import pallas as pl
from jax.experimental.pallas import tpu as pltpu

HEAD_DIM = 128
ROPE_DIM = HEAD_DIM // 4
ROPE_THETA = 500000.0
DILATED_CONFIGS = ((128, 1), (512, 4), (2048, 16))
BAND_BLOCK = 128
DILATED_SPAN = 2048
RMS_EPS = 1e-6
CONV_WIDTH = 3
MASK_VALUE = -1e30
ATTN_HEADS_PER_STEP = 4
LOG2_E = 1.4426950408889634
GATE_PARTS = 3
SB_SUFFIX_PARTS = 1
MATMUL_TN = 1024
FULL_ROW_TM = 256
FFN_TOKEN_TILE = 2048
FFN_SUBTILES = 8
SUBLANES = 8
VMEM_CAP_BYTES = 60 * 1024 * 1024

F32 = jnp.float32
BF16 = jnp.bfloat16


def _params(semantics, vmem_bytes):
    return pltpu.CompilerParams(dimension_semantics=semantics,
                                vmem_limit_bytes=int(min(vmem_bytes, VMEM_CAP_BYTES)))


def _tile(dim, pref):
    t = min(pref, dim)
    while dim % t:
        t //= 2
    return t


def _nt_dot(a, b):
    return lax.dot_general(a, b, (((1,), (1,)), ((), ())), preferred_element_type=F32)


def _tn_dot(a, b):
    return lax.dot_general(a, b, (((0,), (0,)), ((), ())), preferred_element_type=F32)


def _softplus_neg_abs(z):
    return jnp.log1p(jnp.exp(-jnp.abs(z)))


def _split_bf16(x, parts):
    out = []
    for _ in range(parts - 1):
        hi = x.astype(BF16)
        out.append(hi)
        x = x - hi.astype(F32)
    out.append(x.astype(BF16))
    return out


def _rmsnorm_body(h_ref, g_ref, o_ref):
    x = h_ref[...]
    ms = jnp.mean(x * x, axis=-1, keepdims=True)
    o_ref[...] = (x * lax.rsqrt(ms + RMS_EPS) * g_ref[...]).astype(o_ref.dtype)


def _rmsnorm(h, g, out_dtype):
    T, D = h.shape
    tm = _tile(T, 512)
    blk = tm * D * 4
    return pl.pallas_call(
        _rmsnorm_body,
        out_shape=jax.ShapeDtypeStruct((T, D), out_dtype),
        grid=(T // tm,),
        in_specs=[pl.BlockSpec((tm, D), lambda m: (m, 0)),
                  pl.BlockSpec((1, D), lambda m: (0, 0))],
        out_specs=pl.BlockSpec((tm, D), lambda m: (m, 0)),
        compiler_params=_params(("parallel",), 6 * blk),
        name="rmsnorm",
    )(h, g.reshape(1, D))


def _rope_table_body(pos_ref, invf_ref, cos_ref, sa_ref, sb_ref):
    ang = pos_ref[...].astype(F32) * invf_ref[...]
    lane = lax.broadcasted_iota(jnp.int32, ang.shape, 1)
    half = ROPE_DIM // 2
    s = jnp.sin(ang)
    cos_ref[...] = jnp.cos(ang)
    sa_ref[...] = jnp.where((lane >= half) & (lane < ROPE_DIM), s, 0.0)
    sb_ref[...] = jnp.where(lane < half, -s, 0.0)


def _rope_tables(positions):
    T = positions.size
    tm = _tile(T, 512)
    half = ROPE_DIM // 2
    inv_freq = ROPE_THETA ** (-jnp.arange(0, ROPE_DIM, 2, dtype=F32) / ROPE_DIM)
    invf = jnp.concatenate([inv_freq, inv_freq, jnp.zeros((HEAD_DIM - 2 * half,), F32)]).reshape(1, HEAD_DIM)
    tab = jax.ShapeDtypeStruct((T, HEAD_DIM), F32)
    spec = pl.BlockSpec((tm, HEAD_DIM), lambda m: (m, 0))
    return pl.pallas_call(
        _rope_table_body,
        out_shape=(tab, tab, tab),
        grid=(T // tm,),
        in_specs=[pl.BlockSpec((tm, 1), lambda m: (m, 0)),
                  pl.BlockSpec((1, HEAD_DIM), lambda m: (0, 0))],
        out_specs=(spec, spec, spec),
        compiler_params=_params(("parallel",), 32 * tm * HEAD_DIM * 4),
        name="rope_tables",
    )(positions.reshape(T, 1), invf)


def _qkv_body(*refs, rope, scale, tiles_per_section, heads_per_tile, w_transposed):
    if rope:
        a_ref, w_ref, cos_ref, sa_ref, sb_ref, o_ref, wbf_ref = refs
    else:
        a_ref, w_ref, o_ref, wbf_ref = refs
    n = pl.program_id(0)

    @pl.when(pl.program_id(1) == 0)
    def _():
        wbf_ref[...] = w_ref[...].astype(BF16)

    if w_transposed:
        acc = _nt_dot(a_ref[...], wbf_ref[...])
    else:
        acc = jnp.dot(a_ref[...], wbf_ref[...], preferred_element_type=F32)
    acc = acc * jnp.where(n < tiles_per_section, scale, 1.0).astype(F32)
    if not rope:
        o_ref[...] = acc.astype(o_ref.dtype)
        return

    @pl.when(n < 2 * tiles_per_section)
    def _():
        c, sa, sb = cos_ref[...], sa_ref[...], sb_ref[...]
        half = ROPE_DIM // 2
        for j in range(heads_per_tile):
            x = acc[:, j * HEAD_DIM:(j + 1) * HEAD_DIM]
            y = x * c + pltpu.roll(x, half, 1) * sa + pltpu.roll(x, HEAD_DIM - half, 1) * sb
            o_ref[:, j * HEAD_DIM:(j + 1) * HEAD_DIM] = y.astype(o_ref.dtype)

    @pl.when(n >= 2 * tiles_per_section)
    def _():
        o_ref[...] = acc.astype(o_ref.dtype)


def _qkv_proj(a, w, layer, q_scale, first_head, n_heads, out_dtype, rope_tabs=None, w_transposed=False):
    T, D = a.shape
    rope = rope_tabs is not None
    width = n_heads * HEAD_DIM
    tm = _tile(T, 512)
    tn = _tile(width, MATMUL_TN)
    tiles = width // tn
    per_section, first = D // tn, first_head * HEAD_DIM // tn
    body = functools.partial(_qkv_body, rope=rope, scale=q_scale, tiles_per_section=tiles,
                             heads_per_tile=tn // HEAD_DIM, w_transposed=w_transposed)

    def w_tile(n):
        return (n // tiles) * per_section + first + n % tiles

    if w_transposed:
        w_spec = pl.BlockSpec((None, tn, D), lambda n, m: (layer, w_tile(n), 0))
    else:
        w_spec = pl.BlockSpec((None, D, tn), lambda n, m: (layer, 0, w_tile(n)))
    in_specs = [pl.BlockSpec((tm, D), lambda n, m: (m, 0)), w_spec]
    args = [a, w]
    if rope:
        in_specs += [pl.BlockSpec((tm, HEAD_DIM), lambda n, m: (m, 0))] * 3
        args += list(rope_tabs)
    vmem = 2 * tm * D * 2 + 2 * D * tn * 4 + D * tn * 2 + 8 * tm * tn * 4 + 6 * tm * HEAD_DIM * 4
    return pl.pallas_call(
        body,
        out_shape=jax.ShapeDtypeStruct((T, 3 * width), out_dtype),
        grid=(3 * tiles, T // tm),
        in_specs=in_specs,
        out_specs=pl.BlockSpec((tm, tn), lambda n, m: (m, n)),
        scratch_shapes=[pltpu.VMEM((tn, D) if w_transposed else (D, tn), BF16)],
        compiler_params=_params(("parallel", "arbitrary"), vmem),
        name="qkv_proj",
    )(*args)


def _sb_body(q_ref, k_ref, v_ref, o_ref, acc_ref, carry_ref, *, tq, tk, heads):
    qi = pl.program_id(2)
    ratio = tq // tk
    q = q_ref[...]
    acc_ref[...] = jnp.zeros_like(acc_ref)
    carry_ref[...] = jnp.zeros_like(carry_ref)
    strict_upper = (lax.broadcasted_iota(jnp.int32, (tk, tk), 1)
                    > lax.broadcasted_iota(jnp.int32, (tk, tk), 0)).astype(BF16)
    strict_upper_n = jnp.concatenate([strict_upper] * SB_SUFFIX_PARTS, axis=1)

    def block(j, masked):
        start = pl.multiple_of(j * tk, tk)
        if masked:
            kpos = j * tk + lax.broadcasted_iota(jnp.int32, (tk, tq), 0)
            qpos = qi * tq + lax.broadcasted_iota(jnp.int32, (tk, tq), 1)
            mask = kpos < qpos
        heads_sl = [slice(hh * HEAD_DIM, (hh + 1) * HEAD_DIM) for hh in range(heads)]
        scores = [_nt_dot(k_ref[pl.ds(start, tk), sl], q[:, sl]) for sl in heads_sl]
        costs = [jnp.maximum(z, 0.0) + jnp.log2(1.0 + jnp.exp2(-jnp.abs(z))) for z in scores]
        if masked:
            costs = [jnp.where(mask, cost, 0.0) for cost in costs]
        suffixes = [jnp.dot(strict_upper_n, jnp.concatenate(_split_bf16(cost, SB_SUFFIX_PARTS), axis=0),
                            preferred_element_type=F32) for cost in costs]
        for hh, sl in enumerate(heads_sl):
            carry = carry_ref[hh]
            w = jnp.exp2(scores[hh] - costs[hh] - suffixes[hh] - carry)
            if masked:
                w = jnp.where(mask, w, 0.0)
            acc_ref[hh] += _tn_dot(v_ref[pl.ds(start, tk), sl], w.astype(BF16))
            carry_ref[hh] = carry + suffixes[hh][0:1, :] + costs[hh][0:1, :]

    for jj in reversed(range(ratio)):
        block(qi * ratio + jj, True)

    def body(i, c):
        block(qi * ratio - 1 - i, False)
        return c

    lax.fori_loop(0, qi * ratio, body, 0)
    for hh in range(heads):
        o_ref[:, hh * HEAD_DIM:(hh + 1) * HEAD_DIM] = acc_ref[hh].T.astype(o_ref.dtype)


def _sb_attention(qkv, B, S, n_heads):
    T = B * S
    tq = _tile(S, 512)
    tk = _tile(S, 256)
    nq = S // tq
    heads = _tile(n_heads, ATTN_HEADS_PER_STEP)
    width = heads * HEAD_DIM
    hb = n_heads // heads
    body = functools.partial(_sb_body, tq=tq, tk=tk, heads=heads)
    vmem = heads * (4 * S * HEAD_DIM * 2 + 4 * tq * HEAD_DIM * 2 + tq * HEAD_DIM * 4 + SUBLANES * tq * 4
                    + 12 * tq * tk * 4)
    return pl.pallas_call(
        body,
        out_shape=jax.ShapeDtypeStruct((T, n_heads * HEAD_DIM), BF16),
        grid=(B, n_heads // heads, nq),
        in_specs=[pl.BlockSpec((tq, width), lambda b, h, i: (b * nq + i, h)),
                  pl.BlockSpec((S, width), lambda b, h, i: (b, hb + h)),
                  pl.BlockSpec((S, width), lambda b, h, i: (b, 2 * hb + h))],
        out_specs=pl.BlockSpec((tq, width), lambda b, h, i: (b * nq + i, h)),
        scratch_shapes=[pltpu.VMEM((heads, HEAD_DIM, tq), F32), pltpu.VMEM((heads, 1, tq), F32)],
        compiler_params=_params(("parallel", "parallel", "parallel"), vmem),
        name="stick_breaking_attention",
    )(qkv, qkv, qkv)


def _dilated_body(q_ref, k_ref, v_ref, *refs, configs, blk):
    n_cfg = len(configs)
    halos, o_ref = refs[:2 * n_cfg], refs[2 * n_cfg]
    part_o, part_lse = refs[2 * n_cfg + 1:3 * n_cfg + 1], refs[3 * n_cfg + 1:]
    span = q_ref.shape[0]
    span_idx = pl.program_id(2)
    row = lax.broadcasted_iota(jnp.int32, (blk, 2 * blk), 0)
    col = lax.broadcasted_iota(jnp.int32, (blk, 2 * blk), 1)
    dist = row + blk - col

    for c, (window, dil) in enumerate(configs):
        kp_ref, vp_ref = halos[2 * c], halos[2 * c + 1]
        band = (dist >= 0) & (dist <= window // dil)
        band_first = band & ((span_idx > 0) | (col >= blk))

        def rows(ref, start, dil=dil):
            return ref[pl.ds(start, blk, stride=dil), :] if dil > 1 else ref[pl.ds(start, blk), :]

        for r in range(dil):
            k_prev, v_prev = rows(kp_ref, r).astype(BF16), rows(vp_ref, r).astype(BF16)
            for j in range(span // (dil * blk)):
                start = r + dil * blk * j
                k_run, v_run = rows(k_ref, start).astype(BF16), rows(v_ref, start).astype(BF16)
                kk = jnp.concatenate([k_prev, k_run], axis=0)
                vv = jnp.concatenate([v_prev, v_run], axis=0)
                k_prev, v_prev = k_run, v_run
                s = _nt_dot(rows(q_ref, start).astype(BF16), kk)
                s = jnp.where(band if j else band_first, s, MASK_VALUE)
                m = jnp.max(s, axis=1, keepdims=True)
                p = jnp.exp(s - m)
                l = jnp.sum(p, axis=1, keepdims=True)
                out = jnp.dot(p.astype(BF16), vv, preferred_element_type=F32) / l
                lse = jnp.broadcast_to(m + jnp.log(l), (blk, HEAD_DIM))
                if dil > 1:
                    part_o[c][pl.ds(start, blk, stride=dil), :] = out
                    part_lse[c][pl.ds(start, blk, stride=dil), :] = lse
                else:
                    part_o[c][pl.ds(start, blk), :] = out
                    part_lse[c][pl.ds(start, blk), :] = lse

    top = part_lse[0][...]
    for lse_ref in part_lse[1:]:
        top = jnp.maximum(top, lse_ref[...])
    num = jnp.zeros(o_ref.shape, F32)
    den = jnp.zeros(o_ref.shape, F32)
    for out_ref, lse_ref in zip(part_o, part_lse):
        e = jnp.exp(lse_ref[...] - top)
        num = num + e * out_ref[...]
        den = den + e
    o_ref[...] = (num / den).astype(o_ref.dtype)


def _dilated_attention(qkv, B, S, n_heads):
    T = B * S
    blk = BAND_BLOCK
    span = _tile(S, DILATED_SPAN)
    spans = S // span

    def cur(section):
        return pl.BlockSpec((span, HEAD_DIM), lambda b, h, i: (b * spans + i, section * n_heads + h))

    def halo(section, dil):
        rows = dil * blk
        return pl.BlockSpec((rows, HEAD_DIM), lambda b, h, i: (
            jnp.maximum((b * spans + i) * (span // rows) - 1, 0), section * n_heads + h))

    halo_specs = [halo(section, dil) for _, dil in DILATED_CONFIGS for section in (1, 2)]
    n_cfg = len(DILATED_CONFIGS)
    body = functools.partial(_dilated_body, configs=DILATED_CONFIGS, blk=blk)
    halo_rows = sum(dil * blk for _, dil in DILATED_CONFIGS)
    vmem = (2 * (3 * span + 2 * halo_rows) * HEAD_DIM * 4 + 2 * n_cfg * span * HEAD_DIM * 4
            + 10 * span * HEAD_DIM * 4 + 24 * blk * 2 * blk * 4)
    return pl.pallas_call(
        body,
        out_shape=jax.ShapeDtypeStruct((T, n_heads * HEAD_DIM), BF16),
        grid=(B, n_heads, spans),
        in_specs=[cur(0), cur(1), cur(2)] + halo_specs,
        out_specs=pl.BlockSpec((span, HEAD_DIM), lambda b, h, i: (b * spans + i, h)),
        scratch_shapes=[pltpu.VMEM((span, HEAD_DIM), F32) for _ in range(2 * n_cfg)],
        compiler_params=_params(("parallel", "parallel", "parallel"), vmem),
        name="dilated_attention",
    )(*([qkv] * (3 + 2 * n_cfg)))


def _forget_body(a_ref, w_ref, b_ref, c_ref, carry_ref, *, ts):
    @pl.when(pl.program_id(1) == 0)
    def _():
        carry_ref[...] = jnp.zeros_like(carry_ref)

    x = _nt_dot(a_ref[...], w_ref[...].astype(BF16)) + b_ref[...]
    log_f = jnp.minimum(x, 0.0) - _softplus_neg_abs(x)
    lower = (lax.broadcasted_iota(jnp.int32, (ts, ts), 0)
             >= lax.broadcasted_iota(jnp.int32, (ts, ts), 1)).astype(BF16)
    cum = carry_ref[...]
    for part in _split_bf16(log_f, 3):
        cum = cum + jnp.dot(lower, part, preferred_element_type=F32)
    c_ref[...] = cum
    carry_ref[...] = cum[ts - 1:ts, :]


def _forget_cumsum(a, w_f, b_f, B, S):
    T, D = a.shape
    H = w_f.shape[0]
    ts = _tile(S, 512)
    ns = S // ts
    body = functools.partial(_forget_body, ts=ts)
    vmem = 4 * ts * D * 2 + 4 * D * 128 * 4 + 8 * ts * ts * 4
    return pl.pallas_call(
        body,
        out_shape=jax.ShapeDtypeStruct((T, H), F32),
        grid=(B, ns),
        in_specs=[pl.BlockSpec((ts, D), lambda b, s: (b * ns + s, 0)),
                  pl.BlockSpec((H, D), lambda b, s: (0, 0)),
                  pl.BlockSpec((1, H), lambda b, s: (0, 0))],
        out_specs=pl.BlockSpec((ts, H), lambda b, s: (b * ns + s, 0)),
        scratch_shapes=[pltpu.VMEM((1, H), F32)],
        compiler_params=_params(("parallel", "arbitrary"), vmem),
        name="forget_cumsum",
    )(a, w_f, b_f.reshape(1, H))


def _gate_lanes(parts, head, first_lane, sign):
    n_heads = parts[0].shape[1]
    row = lax.broadcasted_iota(jnp.int32, (n_heads, HEAD_DIM), 0)
    lane = lax.broadcasted_iota(jnp.int32, (n_heads, HEAD_DIM), 1)
    lane_row = lax.broadcasted_iota(jnp.int32, (1, HEAD_DIM), 1)
    gate_lane = (lane_row >= first_lane) & (lane_row < first_lane + GATE_PARTS)
    out = jnp.where((lane_row < 2 * GATE_PARTS) & jnp.logical_not(gate_lane), 1.0, 0.0)
    for i, part in enumerate(parts):
        place = jnp.where((row == head) & (lane == first_lane + i), sign, 0.0).astype(BF16)
        out = out + jnp.dot(part, place, preferred_element_type=F32)
    return out.astype(BF16)


def _fox_body(q_ref, k_ref, v_ref, c_ref, o_ref, acc_ref, m_ref, l_ref, kx_ref, *, tq, tk, heads):
    hg = pl.program_id(1)
    qi = pl.program_id(2)
    heads_sl = [slice(hh * HEAD_DIM, (hh + 1) * HEAD_DIM) for hh in range(heads)]

    @pl.when(qi == 0)
    def _():
        parts = _split_bf16(c_ref[...] * LOG2_E, GATE_PARTS)
        for hh, sl in enumerate(heads_sl):
            kx_ref[hh, :, :HEAD_DIM] = k_ref[:, sl]
            kx_ref[hh, :, HEAD_DIM:] = _gate_lanes(parts, hg * heads + hh, GATE_PARTS, -1.0)

    q_parts = _split_bf16(c_ref[pl.ds(pl.multiple_of(qi * tq, tq), tq), :] * LOG2_E, GATE_PARTS)
    qx = [jnp.concatenate([q_ref[:, sl], _gate_lanes(q_parts, hg * heads + hh, 0, 1.0)], axis=1)
          for hh, sl in enumerate(heads_sl)]
    acc_ref[...] = jnp.zeros_like(acc_ref)
    l_ref[...] = jnp.zeros_like(l_ref)
    m_ref[...] = jnp.full_like(m_ref, MASK_VALUE)

    def block(j, masked):
        start = pl.multiple_of(j * tk, tk)
        if masked:
            kpos = j * tk + lax.broadcasted_iota(jnp.int32, (tk, tq), 0)
            qpos = qi * tq + lax.broadcasted_iota(jnp.int32, (tk, tq), 1)
            mask = kpos <= qpos
        scores = [_nt_dot(kx_ref[hh, pl.ds(start, tk), :], qx[hh]) for hh in range(heads)]
        for hh, sl in enumerate(heads_sl):
            s = scores[hh]
            if masked:
                s = jnp.where(mask, s, MASK_VALUE)
            m_old = m_ref[hh]
            m_new = jnp.maximum(m_old, jnp.max(s, axis=0, keepdims=True))
            p = jnp.exp2(s - m_new)
            alpha = jnp.exp2(m_old - m_new)
            l_ref[hh] = alpha * l_ref[hh] + jnp.sum(p, axis=0, keepdims=True)
            acc_ref[hh] = alpha * acc_ref[hh] + _tn_dot(v_ref[pl.ds(start, tk), sl], p.astype(BF16))
            m_ref[hh] = m_new

    def body(j, c):
        block(j, False)
        return c

    lax.fori_loop(0, qi, body, 0)
    block(qi, True)
    for hh in range(heads):
        o_ref[:, hh * HEAD_DIM:(hh + 1) * HEAD_DIM] = (acc_ref[hh] / l_ref[hh]).T.astype(o_ref.dtype)


def _fox_attention(qkv, c, B, S, D):
    T = B * S
    H = D // HEAD_DIM
    tq = tk = _tile(S, 512)
    nq = S // tq
    heads = _tile(H, ATTN_HEADS_PER_STEP)
    width = heads * HEAD_DIM
    hb = D // width
    body = functools.partial(_fox_body, tq=tq, tk=tk, heads=heads)
    vmem = (2 * S * 128 * 4 + heads * (4 * S * HEAD_DIM * 2 + 4 * tq * HEAD_DIM * 2 + S * 2 * HEAD_DIM * 2
                                       + tq * HEAD_DIM * 4 + 2 * SUBLANES * tq * 4 + 12 * tq * tk * 4))
    return pl.pallas_call(
        body,
        out_shape=jax.ShapeDtypeStruct((T, D), BF16),
        grid=(B, hb, nq),
        in_specs=[pl.BlockSpec((tq, width), lambda b, h, i: (b * nq + i, h)),
                  pl.BlockSpec((S, width), lambda b, h, i: (b, hb + h)),
                  pl.BlockSpec((S, width), lambda b, h, i: (b, 2 * hb + h)),
                  pl.BlockSpec((S, H), lambda b, h, i: (b, 0))],
        out_specs=pl.BlockSpec((tq, width), lambda b, h, i: (b * nq + i, h)),
        scratch_shapes=[pltpu.VMEM((heads, HEAD_DIM, tq), F32), pltpu.VMEM((heads, 1, tq), F32),
                        pltpu.VMEM((heads, 1, tq), F32), pltpu.VMEM((heads, S, 2 * HEAD_DIM), BF16)],
        compiler_params=_params(("parallel", "parallel", "arbitrary"), vmem),
        name="forgetting_attention",
    )(qkv, qkv, qkv, c)


def _rms_scaled(x, g):
    return x * lax.rsqrt(jnp.mean(x * x, axis=-1, keepdims=True) + RMS_EPS) * g


def _residual_mm_body(*refs, n_parts, norm):
    a_refs = refs[:n_parts]
    w_refs = refs[n_parts:2 * n_parts]
    rest = refs[2 * n_parts:]
    if norm:
        h_ref, g_ref, o_ref, on_ref = rest[:4]
        wbf_refs = rest[4:]
    else:
        h_ref, o_ref = rest[:2]
        wbf_refs = rest[2:]

    @pl.when(pl.program_id(1) == 0)
    def _():
        for w_ref, wbf_ref in zip(w_refs, wbf_refs):
            wbf_ref[...] = w_ref[...].astype(BF16)

    acc = h_ref[...]
    for a_ref, wbf_ref in zip(a_refs, wbf_refs):
        acc = acc + jnp.dot(a_ref[...], wbf_ref[...], preferred_element_type=F32)
    o_ref[...] = acc
    if norm:
        on_ref[...] = _rms_scaled(acc, g_ref[...]).astype(on_ref.dtype)


def _residual_matmul(h, parts, w, layer, tm_pref=512, norm_gain=None):
    T, N = h.shape
    n_parts = len(parts)
    kp = parts[0].shape[1]
    norm = norm_gain is not None
    tm = _tile(T, FULL_ROW_TM if norm else tm_pref)
    tn = N if norm else _tile(N, MATMUL_TN)
    body = functools.partial(_residual_mm_body, n_parts=n_parts, norm=norm)
    a_specs = [pl.BlockSpec((tm, kp), lambda n, m: (m, 0)) for _ in parts]
    w_specs = [pl.BlockSpec((None, kp, tn), lambda n, m, i=i: (layer, i, n), pipeline_mode=pl.Buffered(1))
               for i in range(n_parts)]
    hspec = pl.BlockSpec((tm, tn), lambda n, m: (m, n))
    vmem = n_parts * (2 * tm * kp * 2 + kp * tn * 4 + kp * tn * 2) + 8 * tm * tn * 4
    extra_in, extra_args, out_shape, out_specs = [], [], jax.ShapeDtypeStruct((T, N), F32), hspec
    if norm:
        extra_in, extra_args = [pl.BlockSpec((1, N), lambda n, m: (0, 0))], [norm_gain.reshape(1, N)]
        out_shape, out_specs = (out_shape, jax.ShapeDtypeStruct((T, N), BF16)), (hspec, hspec)
        vmem += 4 * tm * tn * 4
    return pl.pallas_call(
        body,
        out_shape=out_shape,
        grid=(N // tn, T // tm),
        in_specs=a_specs + w_specs + [hspec] + extra_in,
        out_specs=out_specs,
        scratch_shapes=[pltpu.VMEM((kp, tn), BF16) for _ in parts],
        compiler_params=_params(("parallel", "arbitrary"), vmem),
        name="residual_matmul",
    )(*parts, *([w] * n_parts), h, *extra_args)


def _ffn_up_body(a_ref, wg_ref, wv_ref, cwg_ref, cwv_ref, cbg_ref, cbv_ref, o_ref,
                 wg_bf_ref, wv_bf_ref, u_ref, tails_ref, *, tm, tiles_per_seq):
    m = pl.program_id(1)
    sub = tm // FFN_SUBTILES
    wbf_refs, cw_refs, cb_refs = (wg_bf_ref, wv_bf_ref), (cwg_ref, cwv_ref), (cbg_ref, cbv_ref)

    @pl.when(m == 0)
    def _():
        wg_bf_ref[...] = wg_ref[...].astype(BF16)
        wv_bf_ref[...] = wv_ref[...].astype(BF16)

    @pl.when(m % tiles_per_seq == 0)
    def _():
        tails_ref[0] = jnp.zeros(tails_ref.shape[1:], F32)

    def products(i):
        a = a_ref[i * sub:(i + 1) * sub, :]
        for g in range(2):
            u = jnp.dot(a, wbf_refs[g][...], preferred_element_type=F32)
            u_ref[i % 2, g] = u
            tails_ref[i + 1, g] = u[sub - SUBLANES:, :]

    def conv(i, g):
        u = u_ref[i % 2, g]
        row = lax.broadcasted_iota(jnp.int32, (SUBLANES, u.shape[1]), 0)
        cw = cw_refs[g][...]
        out = u * cw[CONV_WIDTH - 1:CONV_WIDTH, :] + cb_refs[g][...]
        for back in range(1, CONV_WIDTH):
            shifted = pltpu.roll(u, back, 0)
            top = shifted[:SUBLANES]
            for k in range(back):
                prev_row = tails_ref[i, g, SUBLANES - back + k:SUBLANES - back + k + 1, :]
                top = jnp.where(row == k, prev_row, top)
            shifted = jnp.concatenate([top, shifted[SUBLANES:]], axis=0)
            out = out + shifted * cw[CONV_WIDTH - 1 - back:CONV_WIDTH - back, :]
        return out

    def gated(i):
        gate, val = conv(i, 0), conv(i, 1)
        silu = gate / (1.0 + jnp.exp2(gate * -LOG2_E))
        o_ref[i * sub:(i + 1) * sub, :] = (silu * val).astype(o_ref.dtype)

    products(0)
    for i in range(1, FFN_SUBTILES):
        products(i)
        gated(i - 1)
    gated(FFN_SUBTILES - 1)
    tails_ref[0] = tails_ref[FFN_SUBTILES]


def _ffn_up(a, w_up, conv_w, conv_b, layer, S):
    T, D = a.shape
    F = w_up.shape[2] // 2
    tm = _tile(S, FFN_TOKEN_TILE)
    tn = _tile(F, 512)
    nf = F // tn
    body = functools.partial(_ffn_up_body, tm=tm, tiles_per_seq=S // tm)
    conv_b = conv_b.reshape(conv_b.shape[0], 1, 2 * F)
    sub = tm // FFN_SUBTILES
    vmem = (2 * tm * D * 2 + 2 * (D * tn * 4 + D * tn * 2) + 2 * tm * tn * 2 + 4 * sub * tn * 4
            + 16 * sub * tn * 4)
    return pl.pallas_call(
        body,
        out_shape=jax.ShapeDtypeStruct((T, F), BF16),
        grid=(nf, T // tm),
        in_specs=[pl.BlockSpec((tm, D), lambda n, m: (m, 0)),
                  pl.BlockSpec((None, D, tn), lambda n, m: (layer, 0, n), pipeline_mode=pl.Buffered(1)),
                  pl.BlockSpec((None, D, tn), lambda n, m: (layer, 0, nf + n), pipeline_mode=pl.Buffered(1)),
                  pl.BlockSpec((None, CONV_WIDTH, tn), lambda n, m: (layer, 0, n)),
                  pl.BlockSpec((None, CONV_WIDTH, tn), lambda n, m: (layer, 0, nf + n)),
                  pl.BlockSpec((None, 1, tn), lambda n, m: (layer, 0, n)),
                  pl.BlockSpec((None, 1, tn), lambda n, m: (layer, 0, nf + n))],
        out_specs=pl.BlockSpec((tm, tn), lambda n, m: (m, n)),
        scratch_shapes=[pltpu.VMEM((D, tn), BF16), pltpu.VMEM((D, tn), BF16),
                        pltpu.VMEM((2, 2, sub, tn), F32),
                        pltpu.VMEM((FFN_SUBTILES + 1, 2, SUBLANES, tn), F32)],
        compiler_params=_params(("parallel", "arbitrary"), vmem),
        name="ffn_up_conv_gate",
    )(a, w_up, w_up, conv_w, conv_w, conv_b, conv_b)


def _ple_body(a_ref, wg_ref, p_ref, wp_ref, h_ref, g_ref, o_ref, on_ref, wg_bf_ref, wp_bf_ref):
    @pl.when(pl.program_id(0) == 0)
    def _():
        wg_bf_ref[...] = wg_ref[...].astype(BF16)
        wp_bf_ref[...] = wp_ref[...].astype(BF16)

    logits = jnp.dot(a_ref[...], wg_bf_ref[...], preferred_element_type=F32)
    emb = jnp.dot(p_ref[...].astype(BF16), wp_bf_ref[...], preferred_element_type=F32)
    out = h_ref[...] + emb / (1.0 + jnp.exp(-logits))
    o_ref[...] = out
    on_ref[...] = _rms_scaled(out, g_ref[...]).astype(on_ref.dtype)


def _ple(h, a, w_gate, p, w_ple, layer, norm_gain, norm_dtype):
    T, D = h.shape
    P = p.shape[2]
    tm = _tile(T, FULL_ROW_TM)
    row = pl.BlockSpec((tm, D), lambda m: (m, 0))
    vmem = (2 * tm * D * 2 + D * D * 6 + 2 * tm * P * 4 + P * D * 6 + 4 * tm * D * 4
            + 2 * tm * D * jnp.dtype(norm_dtype).itemsize + 8 * tm * D * 4)
    return pl.pallas_call(
        _ple_body,
        out_shape=(jax.ShapeDtypeStruct((T, D), F32), jax.ShapeDtypeStruct((T, D), norm_dtype)),
        grid=(T // tm,),
        in_specs=[row,
                  pl.BlockSpec((None, D, D), lambda m: (layer, 0, 0), pipeline_mode=pl.Buffered(1)),
                  pl.BlockSpec((None, tm, P), lambda m: (layer, m, 0)),
                  pl.BlockSpec((None, P, D), lambda m: (layer, 0, 0), pipeline_mode=pl.Buffered(1)),
                  row,
                  pl.BlockSpec((1, D), lambda m: (0, 0))],
        out_specs=(row, row),
        scratch_shapes=[pltpu.VMEM((D, D), BF16), pltpu.VMEM((P, D), BF16)],
        compiler_params=_params(("arbitrary",), vmem),
        name="gated_layer_embedding",
    )(a, w_gate, p, w_ple, h, norm_gain.reshape(1, D))


def kernel(x, p, positions, attn_norm, w_in_even, w_out_even, w_in_odd, b_forget, w_out_odd, ffn_norm,
           w_up, conv_w, conv_b, w_down, ple_norm, w_ple_gate, w_ple, final_norm):
    B, S, D = x.shape
    T = B * S
    depth = p.shape[0]
    H = D // HEAD_DIM
    n_sb = H // 2
    n_dil = H - n_sb
    h = x.reshape(T, D)
    p = p.reshape(depth, T, p.shape[-1])
    w_in_odd_t = jnp.swapaxes(w_in_odd, 1, 2)
    rope_tabs = _rope_tables(positions)
    a = _rmsnorm(h, attn_norm[0], BF16)
    for i in range(depth):
        if i % 2 == 0:
            scale = HEAD_DIM ** -0.5
            qkv_sb = _qkv_proj(a, w_in_even, i // 2, LOG2_E * scale, 0, n_sb, BF16)
            qkv_dil = _qkv_proj(a, w_in_even, i // 2, scale, n_sb, n_dil, F32, rope_tabs)
            mixed = [_sb_attention(qkv_sb, B, S, n_sb), _dilated_attention(qkv_dil, B, S, n_dil)]
            w_out = w_out_even
        else:
            qkv = _qkv_proj(a, w_in_odd_t, i // 2, LOG2_E * HEAD_DIM ** -0.5, 0, H, BF16, w_transposed=True)
            c = _forget_cumsum(a, w_in_odd_t[i // 2, 3 * D:, :], b_forget[i // 2], B, S)
            mixed = [_fox_attention(qkv, c, B, S, D)]
            w_out = w_out_odd
        h, a = _residual_matmul(h, mixed, w_out, i // 2, norm_gain=ffn_norm[i])
        act = _ffn_up(a, w_up, conv_w, conv_b, i, S)
        h = _residual_matmul(h, [act], w_down, i, tm_pref=256)
        last = i == depth - 1
        h, a = _ple(h, _rmsnorm(h, ple_norm[i], BF16), w_ple_gate, p, w_ple, i,
                    final_norm if last else attn_norm[i + 1], F32 if last else BF16)
    return a.reshape(B, S, D)
```

```python
import functools

import jax
import jax.numpy as jnp
from jax import lax
from jax.experimental import pallas as pl
from jax.experimental.pallas import tpu as pltpu

HEAD_DIM = 128
ROPE_DIM = HEAD_DIM // 4
ROPE_THETA = 500000.0
DILATED_CONFIGS = ((128, 1), (512, 4), (2048, 16))
BAND_BLOCK = 128
DILATED_SPAN = 2048
RMS_EPS = 1e-6
CONV_WIDTH = 3
MASK_VALUE = -1e30
ATTN_HEADS_PER_STEP = 4
LOG2_E = 1.4426950408889634
GATE_PARTS = 3
SB_SUFFIX_PARTS = 1
MATMUL_TN = 1024
FULL_ROW_TM = 256
FFN_TOKEN_TILE = 2048
FFN_SUBTILES = 4
SUBLANES = 8
VMEM_CAP_BYTES = 60 * 1024 * 1024

F32 = jnp.float32
BF16 = jnp.bfloat16


def _params(semantics, vmem_bytes):
    return pltpu.CompilerParams(dimension_semantics=semantics,
                                vmem_limit_bytes=int(min(vmem_bytes, VMEM_CAP_BYTES)))


def _tile(dim, pref):
    t = min(pref, dim)
    while dim % t:
        t //= 2
    return t


def _nt_dot(a, b):
    return lax.dot_general(a, b, (((1,), (1,)), ((), ())), preferred_element_type=F32)


def _tn_dot(a, b):
    return lax.dot_general(a, b, (((0,), (0,)), ((), ())), preferred_element_type=F32)


def _softplus_neg_abs(z):
    return jnp.log1p(jnp.exp(-jnp.abs(z)))


def _split_bf16(x, parts):
    out = []
    for _ in range(parts - 1):
        hi = x.astype(BF16)
        out.append(hi)
        x = x - hi.astype(F32)
    out.append(x.astype(BF16))
    return out


def _rmsnorm_body(h_ref, g_ref, o_ref):
    x = h_ref[...]
    ms = jnp.mean(x * x, axis=-1, keepdims=True)
    o_ref[...] = (x * lax.rsqrt(ms + RMS_EPS) * g_ref[...]).astype(o_ref.dtype)


def _rmsnorm(h, g, out_dtype):
    T, D = h.shape
    tm = _tile(T, 512)
    blk = tm * D * 4
    return pl.pallas_call(
        _rmsnorm_body,
        out_shape=jax.ShapeDtypeStruct((T, D), out_dtype),
        grid=(T // tm,),
        in_specs=[pl.BlockSpec((tm, D), lambda m: (m, 0)),
                  pl.BlockSpec((1, D), lambda m: (0, 0))],
        out_specs=pl.BlockSpec((tm, D), lambda m: (m, 0)),
        compiler_params=_params(("parallel",), 6 * blk),
        name="rmsnorm",
    )(h, g.reshape(1, D))


def _rope_table_body(pos_ref, invf_ref, cos_ref, sa_ref, sb_ref):
    ang = pos_ref[...].astype(F32) * invf_ref[...]
    lane = lax.broadcasted_iota(jnp.int32, ang.shape, 1)
    half = ROPE_DIM // 2
    s = jnp.sin(ang)
    cos_ref[...] = jnp.cos(ang)
    sa_ref[...] = jnp.where((lane >= half) & (lane < ROPE_DIM), s, 0.0)
    sb_ref[...] = jnp.where(lane < half, -s, 0.0)


def _rope_tables(positions):
    T = positions.size
    tm = _tile(T, 512)
    half = ROPE_DIM // 2
    inv_freq = ROPE_THETA ** (-jnp.arange(0, ROPE_DIM, 2, dtype=F32) / ROPE_DIM)
    invf = jnp.concatenate([inv_freq, inv_freq, jnp.zeros((HEAD_DIM - 2 * half,), F32)]).reshape(1, HEAD_DIM)
    tab = jax.ShapeDtypeStruct((T, HEAD_DIM), F32)
    spec = pl.BlockSpec((tm, HEAD_DIM), lambda m: (m, 0))
    return pl.pallas_call(
        _rope_table_body,
        out_shape=(tab, tab, tab),
        grid=(T // tm,),
        in_specs=[pl.BlockSpec((tm, 1), lambda m: (m, 0)),
                  pl.BlockSpec((1, HEAD_DIM), lambda m: (0, 0))],
        out_specs=(spec, spec, spec),
        compiler_params=_params(("parallel",), 32 * tm * HEAD_DIM * 4),
        name="rope_tables",
    )(positions.reshape(T, 1), invf)


def _qkv_body(*refs, rope, scale, tiles_per_section, heads_per_tile, w_transposed):
    if rope:
        a_ref, w_ref, cos_ref, sa_ref, sb_ref, o_ref, wbf_ref = refs
    else:
        a_ref, w_ref, o_ref, wbf_ref = refs
    n = pl.program_id(0)

    @pl.when(pl.program_id(1) == 0)
    def _():
        wbf_ref[...] = w_ref[...].astype(BF16)

    if w_transposed:
        acc = _nt_dot(a_ref[...], wbf_ref[...])
    else:
        acc = jnp.dot(a_ref[...], wbf_ref[...], preferred_element_type=F32)
    acc = acc * jnp.where(n < tiles_per_section, scale, 1.0).astype(F32)
    if not rope:
        o_ref[...] = acc.astype(o_ref.dtype)
        return

    @pl.when(n < 2 * tiles_per_section)
    def _():
        c, sa, sb = cos_ref[...], sa_ref[...], sb_ref[...]
        half = ROPE_DIM // 2
        for j in range(heads_per_tile):
            x = acc[:, j * HEAD_DIM:(j + 1) * HEAD_DIM]
            y = x * c + pltpu.roll(x, half, 1) * sa + pltpu.roll(x, HEAD_DIM - half, 1) * sb
            o_ref[:, j * HEAD_DIM:(j + 1) * HEAD_DIM] = y.astype(o_ref.dtype)

    @pl.when(n >= 2 * tiles_per_section)
    def _():
        o_ref[...] = acc.astype(o_ref.dtype)


def _qkv_proj(a, w, layer, q_scale, first_head, n_heads, out_dtype, rope_tabs=None, w_transposed=False):
    T, D = a.shape
    rope = rope_tabs is not None
    width = n_heads * HEAD_DIM
    tm = _tile(T, 512)
    tn = _tile(width, MATMUL_TN)
    tiles = width // tn
    per_section, first = D // tn, first_head * HEAD_DIM // tn
    body = functools.partial(_qkv_body, rope=rope, scale=q_scale, tiles_per_section=tiles,
                             heads_per_tile=tn // HEAD_DIM, w_transposed=w_transposed)

    def w_tile(n):
        return (n // tiles) * per_section + first + n % tiles

    if w_transposed:
        w_spec = pl.BlockSpec((None, tn, D), lambda n, m: (layer, w_tile(n), 0))
    else:
        w_spec = pl.BlockSpec((None, D, tn), lambda n, m: (layer, 0, w_tile(n)))
    in_specs = [pl.BlockSpec((tm, D), lambda n, m: (m, 0)), w_spec]
    args = [a, w]
    if rope:
        in_specs += [pl.BlockSpec((tm, HEAD_DIM), lambda n, m: (m, 0))] * 3
        args += list(rope_tabs)
    vmem = 2 * tm * D * 2 + 2 * D * tn * 4 + D * tn * 2 + 8 * tm * tn * 4 + 6 * tm * HEAD_DIM * 4
    return pl.pallas_call(
        body,
        out_shape=jax.ShapeDtypeStruct((T, 3 * width), out_dtype),
        grid=(3 * tiles, T // tm),
        in_specs=in_specs,
        out_specs=pl.BlockSpec((tm, tn), lambda n, m: (m, n)),
        scratch_shapes=[pltpu.VMEM((tn, D) if w_transposed else (D, tn), BF16)],
        compiler_params=_params(("parallel", "arbitrary"), vmem),
        name="qkv_proj",
    )(*args)


def _sb_body(q_ref, k_ref, v_ref, o_ref, acc_ref, carry_ref, *, tq, tk, heads):
    qi = pl.program_id(2)
    ratio = tq // tk
    q = q_ref[...]
    acc_ref[...] = jnp.zeros_like(acc_ref)
    carry_ref[...] = jnp.zeros_like(carry_ref)
    strict_upper = (lax.broadcasted_iota(jnp.int32, (tk, tk), 1)
                    > lax.broadcasted_iota(jnp.int32, (tk, tk), 0)).astype(BF16)
    strict_upper_n = jnp.concatenate([strict_upper] * SB_SUFFIX_PARTS, axis=1)

    def block(j, masked):
        start = pl.multiple_of(j * tk, tk)
        if masked:
            kpos = j * tk + lax.broadcasted_iota(jnp.int32, (tk, tq), 0)
            qpos = qi * tq + lax.broadcasted_iota(jnp.int32, (tk, tq), 1)
            mask = kpos < qpos
        heads_sl = [slice(hh * HEAD_DIM, (hh + 1) * HEAD_DIM) for hh in range(heads)]
        scores = [_nt_dot(k_ref[pl.ds(start, tk), sl], q[:, sl]) for sl in heads_sl]
        costs = [jnp.maximum(z, 0.0) + jnp.log2(1.0 + jnp.exp2(-jnp.abs(z))) for z in scores]
        if masked:
            costs = [jnp.where(mask, cost, 0.0) for cost in costs]
        suffixes = [jnp.dot(strict_upper_n, jnp.concatenate(_split_bf16(cost, SB_SUFFIX_PARTS), axis=0),
                            preferred_element_type=F32) for cost in costs]
        for hh, sl in enumerate(heads_sl):
            carry = carry_ref[hh]
            w = jnp.exp2(scores[hh] - costs[hh] - suffixes[hh] - carry)
            if masked:
                w = jnp.where(mask, w, 0.0)
            acc_ref[hh] += _tn_dot(v_ref[pl.ds(start, tk), sl], w.astype(BF16))
            carry_ref[hh] = carry + suffixes[hh][0:1, :] + costs[hh][0:1, :]

    for jj in reversed(range(ratio)):
        block(qi * ratio + jj, True)

    def body(i, c):
        block(qi * ratio - 1 - i, False)
        return c

    lax.fori_loop(0, qi * ratio, body, 0)
    for hh in range(heads):
        o_ref[:, hh * HEAD_DIM:(hh + 1) * HEAD_DIM] = acc_ref[hh].T.astype(o_ref.dtype)


def _sb_attention(qkv, B, S, n_heads):
    T = B * S
    tq = _tile(S, 512)
    tk = _tile(S, 256)
    nq = S // tq
    heads = _tile(n_heads, ATTN_HEADS_PER_STEP)
    width = heads * HEAD_DIM
    hb = n_heads // heads
    body = functools.partial(_sb_body, tq=tq, tk=tk, heads=heads)
    vmem = heads * (4 * S * HEAD_DIM * 2 + 4 * tq * HEAD_DIM * 2 + tq * HEAD_DIM * 4 + SUBLANES * tq * 4
                    + 12 * tq * tk * 4)
    return pl.pallas_call(
        body,
        out_shape=jax.ShapeDtypeStruct((T, n_heads * HEAD_DIM), BF16),
        grid=(B, n_heads // heads, nq),
        in_specs=[pl.BlockSpec((tq, width), lambda b, h, i: (b * nq + i, h)),
                  pl.BlockSpec((S, width), lambda b, h, i: (b, hb + h)),
                  pl.BlockSpec((S, width), lambda b, h, i: (b, 2 * hb + h))],
        out_specs=pl.BlockSpec((tq, width), lambda b, h, i: (b * nq + i, h)),
        scratch_shapes=[pltpu.VMEM((heads, HEAD_DIM, tq), F32), pltpu.VMEM((heads, 1, tq), F32)],
        compiler_params=_params(("parallel", "parallel", "parallel"), vmem),
        name="stick_breaking_attention",
    )(qkv, qkv, qkv)


def _dilated_body(q_ref, k_ref, v_ref, *refs, configs, blk):
    n_cfg = len(configs)
    halos, o_ref = refs[:2 * n_cfg], refs[2 * n_cfg]
    part_o, part_lse = refs[2 * n_cfg + 1:3 * n_cfg + 1], refs[3 * n_cfg + 1:]
    span = q_ref.shape[0]
    span_idx = pl.program_id(2)
    row = lax.broadcasted_iota(jnp.int32, (blk, 2 * blk), 0)
    col = lax.broadcasted_iota(jnp.int32, (blk, 2 * blk), 1)
    dist = row + blk - col

    for c, (window, dil) in enumerate(configs):
        kp_ref, vp_ref = halos[2 * c], halos[2 * c + 1]
        band = (dist >= 0) & (dist <= window // dil)
        band_first = band & ((span_idx > 0) | (col >= blk))

        def rows(ref, start, dil=dil):
            return ref[pl.ds(start, blk, stride=dil), :] if dil > 1 else ref[pl.ds(start, blk), :]

        for r in range(dil):
            k_prev, v_prev = rows(kp_ref, r).astype(BF16), rows(vp_ref, r).astype(BF16)
            for j in range(span // (dil * blk)):
                start = r + dil * blk * j
                k_run, v_run = rows(k_ref, start).astype(BF16), rows(v_ref, start).astype(BF16)
                kk = jnp.concatenate([k_prev, k_run], axis=0)
                vv = jnp.concatenate([v_prev, v_run], axis=0)
                k_prev, v_prev = k_run, v_run
                s = _nt_dot(rows(q_ref, start).astype(BF16), kk)
                s = jnp.where(band if j else band_first, s, MASK_VALUE)
                m = jnp.max(s, axis=1, keepdims=True)
                p = jnp.exp(s - m)
                l = jnp.sum(p, axis=1, keepdims=True)
                out = jnp.dot(p.astype(BF16), vv, preferred_element_type=F32) / l
                lse = jnp.broadcast_to(m + jnp.log(l), (blk, HEAD_DIM))
                if dil > 1:
                    part_o[c][pl.ds(start, blk, stride=dil), :] = out
                    part_lse[c][pl.ds(start, blk, stride=dil), :] = lse
                else:
                    part_o[c][pl.ds(start, blk), :] = out
                    part_lse[c][pl.ds(start, blk), :] = lse

    top = part_lse[0][...]
    for lse_ref in part_lse[1:]:
        top = jnp.maximum(top, lse_ref[...])
    num = jnp.zeros(o_ref.shape, F32)
    den = jnp.zeros(o_ref.shape, F32)
    for out_ref, lse_ref in zip(part_o, part_lse):
        e = jnp.exp(lse_ref[...] - top)
        num = num + e * out_ref[...]
        den = den + e
    o_ref[...] = (num / den).astype(o_ref.dtype)


def _dilated_attention(qkv, B, S, n_heads):
    T = B * S
    blk = BAND_BLOCK
    span = _tile(S, DILATED_SPAN)
    spans = S // span

    def cur(section):
        return pl.BlockSpec((span, HEAD_DIM), lambda b, h, i: (b * spans + i, section * n_heads + h))

    def halo(section, dil):
        rows = dil * blk
        return pl.BlockSpec((rows, HEAD_DIM), lambda b, h, i: (
            jnp.maximum((b * spans + i) * (span // rows) - 1, 0), section * n_heads + h))

    halo_specs = [halo(section, dil) for _, dil in DILATED_CONFIGS for section in (1, 2)]
    n_cfg = len(DILATED_CONFIGS)
    body = functools.partial(_dilated_body, configs=DILATED_CONFIGS, blk=blk)
    halo_rows = sum(dil * blk for _, dil in DILATED_CONFIGS)
    vmem = (2 * (3 * span + 2 * halo_rows) * HEAD_DIM * 4 + 2 * n_cfg * span * HEAD_DIM * 4
            + 10 * span * HEAD_DIM * 4 + 24 * blk * 2 * blk * 4)
    return pl.pallas_call(
        body,
        out_shape=jax.ShapeDtypeStruct((T, n_heads * HEAD_DIM), BF16),
        grid=(B, n_heads, spans),
        in_specs=[cur(0), cur(1), cur(2)] + halo_specs,
        out_specs=pl.BlockSpec((span, HEAD_DIM), lambda b, h, i: (b * spans + i, h)),
        scratch_shapes=[pltpu.VMEM((span, HEAD_DIM), F32) for _ in range(2 * n_cfg)],
        compiler_params=_params(("parallel", "parallel", "parallel"), vmem),
        name="dilated_attention",
    )(*([qkv] * (3 + 2 * n_cfg)))


def _forget_body(a_ref, w_ref, b_ref, c_ref, carry_ref, *, ts):
    @pl.when(pl.program_id(1) == 0)
    def _():
        carry_ref[...] = jnp.zeros_like(carry_ref)

    x = _nt_dot(a_ref[...], w_ref[...].astype(BF16)) + b_ref[...]
    log_f = jnp.minimum(x, 0.0) - _softplus_neg_abs(x)
    lower = (lax.broadcasted_iota(jnp.int32, (ts, ts), 0)
             >= lax.broadcasted_iota(jnp.int32, (ts, ts), 1)).astype(BF16)
    cum = carry_ref[...]
    for part in _split_bf16(log_f, 3):
        cum = cum + jnp.dot(lower, part, preferred_element_type=F32)
    c_ref[...] = cum
    carry_ref[...] = cum[ts - 1:ts, :]


def _forget_cumsum(a, w_f, b_f, B, S):
    T, D = a.shape
    H = w_f.shape[0]
    ts = _tile(S, 512)
    ns = S // ts
    body = functools.partial(_forget_body, ts=ts)
    vmem = 4 * ts * D * 2 + 4 * D * 128 * 4 + 8 * ts * ts * 4
    return pl.pallas_call(
        body,
        out_shape=jax.ShapeDtypeStruct((T, H), F32),
        grid=(B, ns),
        in_specs=[pl.BlockSpec((ts, D), lambda b, s: (b * ns + s, 0)),
                  pl.BlockSpec((H, D), lambda b, s: (0, 0)),
                  pl.BlockSpec((1, H), lambda b, s: (0, 0))],
        out_specs=pl.BlockSpec((ts, H), lambda b, s: (b * ns + s, 0)),
        scratch_shapes=[pltpu.VMEM((1, H), F32)],
        compiler_params=_params(("parallel", "arbitrary"), vmem),
        name="forget_cumsum",
    )(a, w_f, b_f.reshape(1, H))


def _gate_lanes(parts, head, first_lane, sign):
    n_heads = parts[0].shape[1]
    row = lax.broadcasted_iota(jnp.int32, (n_heads, HEAD_DIM), 0)
    lane = lax.broadcasted_iota(jnp.int32, (n_heads, HEAD_DIM), 1)
    lane_row = lax.broadcasted_iota(jnp.int32, (1, HEAD_DIM), 1)
    gate_lane = (lane_row >= first_lane) & (lane_row < first_lane + GATE_PARTS)
    out = jnp.where((lane_row < 2 * GATE_PARTS) & jnp.logical_not(gate_lane), 1.0, 0.0)
    for i, part in enumerate(parts):
        place = jnp.where((row == head) & (lane == first_lane + i), sign, 0.0).astype(BF16)
        out = out + jnp.dot(part, place, preferred_element_type=F32)
    return out.astype(BF16)


def _fox_body(q_ref, k_ref, v_ref, c_ref, o_ref, acc_ref, m_ref, l_ref, kx_ref, *, tq, tk, heads):
    hg = pl.program_id(1)
    qi = pl.program_id(2)
    heads_sl = [slice(hh * HEAD_DIM, (hh + 1) * HEAD_DIM) for hh in range(heads)]

    @pl.when(qi == 0)
    def _():
        parts = _split_bf16(c_ref[...] * LOG2_E, GATE_PARTS)
        for hh, sl in enumerate(heads_sl):
            kx_ref[hh, :, :HEAD_DIM] = k_ref[:, sl]
            kx_ref[hh, :, HEAD_DIM:] = _gate_lanes(parts, hg * heads + hh, GATE_PARTS, -1.0)

    q_parts = _split_bf16(c_ref[pl.ds(pl.multiple_of(qi * tq, tq), tq), :] * LOG2_E, GATE_PARTS)
    qx = [jnp.concatenate([q_ref[:, sl], _gate_lanes(q_parts, hg * heads + hh, 0, 1.0)], axis=1)
          for hh, sl in enumerate(heads_sl)]
    acc_ref[...] = jnp.zeros_like(acc_ref)
    l_ref[...] = jnp.zeros_like(l_ref)
    m_ref[...] = jnp.full_like(m_ref, MASK_VALUE)

    def block(j, masked):
        start = pl.multiple_of(j * tk, tk)
        if masked:
            kpos = j * tk + lax.broadcasted_iota(jnp.int32, (tk, tq), 0)
            qpos = qi * tq + lax.broadcasted_iota(jnp.int32, (tk, tq), 1)
            mask = kpos <= qpos
        scores = [_nt_dot(kx_ref[hh, pl.ds(start, tk), :], qx[hh]) for hh in range(heads)]
        for hh, sl in enumerate(heads_sl):
            s = scores[hh]
            if masked:
                s = jnp.where(mask, s, MASK_VALUE)
            m_old = m_ref[hh]
            m_new = jnp.maximum(m_old, jnp.max(s, axis=0, keepdims=True))
            p = jnp.exp2(s - m_new)
            alpha = jnp.exp2(m_old - m_new)
            l_ref[hh] = alpha * l_ref[hh] + jnp.sum(p, axis=0, keepdims=True)
            acc_ref[hh] = alpha * acc_ref[hh] + _tn_dot(v_ref[pl.ds(start, tk), sl], p.astype(BF16))
            m_ref[hh] = m_new

    def body(j, c):
        block(j, False)
        return c

    lax.fori_loop(0, qi, body, 0)
    block(qi, True)
    for hh in range(heads):
        o_ref[:, hh * HEAD_DIM:(hh + 1) * HEAD_DIM] = (acc_ref[hh] / l_ref[hh]).T.astype(o_ref.dtype)


def _fox_attention(qkv, c, B, S, D):
    T = B * S
    H = D // HEAD_DIM
    tq = tk = _tile(S, 512)
    nq = S // tq
    heads = _tile(H, ATTN_HEADS_PER_STEP)
    width = heads * HEAD_DIM
    hb = D // width
    body = functools.partial(_fox_body, tq=tq, tk=tk, heads=heads)
    vmem = (2 * S * 128 * 4 + heads * (4 * S * HEAD_DIM * 2 + 4 * tq * HEAD_DIM * 2 + S * 2 * HEAD_DIM * 2
                                       + tq * HEAD_DIM * 4 + 2 * SUBLANES * tq * 4 + 12 * tq * tk * 4))
    return pl.pallas_call(
        body,
        out_shape=jax.ShapeDtypeStruct((T, D), BF16),
        grid=(B, hb, nq),
        in_specs=[pl.BlockSpec((tq, width), lambda b, h, i: (b * nq + i, h)),
                  pl.BlockSpec((S, width), lambda b, h, i: (b, hb + h)),
                  pl.BlockSpec((S, width), lambda b, h, i: (b, 2 * hb + h)),
                  pl.BlockSpec((S, H), lambda b, h, i: (b, 0))],
        out_specs=pl.BlockSpec((tq, width), lambda b, h, i: (b * nq + i, h)),
        scratch_shapes=[pltpu.VMEM((heads, HEAD_DIM, tq), F32), pltpu.VMEM((heads, 1, tq), F32),
                        pltpu.VMEM((heads, 1, tq), F32), pltpu.VMEM((heads, S, 2 * HEAD_DIM), BF16)],
        compiler_params=_params(("parallel", "parallel", "arbitrary"), vmem),
        name="forgetting_attention",
    )(qkv, qkv, qkv, c)


def _rms_scaled(x, g):
    return x * lax.rsqrt(jnp.mean(x * x, axis=-1, keepdims=True) + RMS_EPS) * g


def _residual_mm_body(*refs, n_parts, norm):
    a_refs = refs[:n_parts]
    w_refs = refs[n_parts:2 * n_parts]
    rest = refs[2 * n_parts:]
    if norm:
        h_ref, g_ref, o_ref, on_ref = rest[:4]
        wbf_refs = rest[4:]
    else:
        h_ref, o_ref = rest[:2]
        wbf_refs = rest[2:]

    @pl.when(pl.program_id(1) == 0)
    def _():
        for w_ref, wbf_ref in zip(w_refs, wbf_refs):
            wbf_ref[...] = w_ref[...].astype(BF16)

    acc = h_ref[...]
    for a_ref, wbf_ref in zip(a_refs, wbf_refs):
        acc = acc + jnp.dot(a_ref[...], wbf_ref[...], preferred_element_type=F32)
    o_ref[...] = acc
    if norm:
        on_ref[...] = _rms_scaled(acc, g_ref[...]).astype(on_ref.dtype)


def _residual_matmul(h, parts, w, layer, tm_pref=512, norm_gain=None):
    T, N = h.shape
    n_parts = len(parts)
    kp = parts[0].shape[1]
    norm = norm_gain is not None
    tm = _tile(T, FULL_ROW_TM if norm else tm_pref)
    tn = N if norm else _tile(N, MATMUL_TN)
    body = functools.partial(_residual_mm_body, n_parts=n_parts, norm=norm)
    a_specs = [pl.BlockSpec((tm, kp), lambda n, m: (m, 0)) for _ in parts]
    w_specs = [pl.BlockSpec((None, kp, tn), lambda n, m, i=i: (layer, i, n), pipeline_mode=pl.Buffered(1))
               for i in range(n_parts)]
    hspec = pl.BlockSpec((tm, tn), lambda n, m: (m, n))
    vmem = n_parts * (2 * tm * kp * 2 + kp * tn * 4 + kp * tn * 2) + 8 * tm * tn * 4
    extra_in, extra_args, out_shape, out_specs = [], [], jax.ShapeDtypeStruct((T, N), F32), hspec
    if norm:
        extra_in, extra_args = [pl.BlockSpec((1, N), lambda n, m: (0, 0))], [norm_gain.reshape(1, N)]
        out_shape, out_specs = (out_shape, jax.ShapeDtypeStruct((T, N), BF16)), (hspec, hspec)
        vmem += 4 * tm * tn * 4
    return pl.pallas_call(
        body,
        out_shape=out_shape,
        grid=(N // tn, T // tm),
        in_specs=a_specs + w_specs + [hspec] + extra_in,
        out_specs=out_specs,
        scratch_shapes=[pltpu.VMEM((kp, tn), BF16) for _ in parts],
        compiler_params=_params(("parallel", "arbitrary"), vmem),
        name="residual_matmul",
    )(*parts, *([w] * n_parts), h, *extra_args)


def _ffn_up_body(a_ref, wg_ref, wv_ref, cwg_ref, cwv_ref, cbg_ref, cbv_ref, o_ref,
                 wg_bf_ref, wv_bf_ref, u_ref, tails_ref, *, tm, tiles_per_seq):
    m = pl.program_id(1)
    sub = tm // FFN_SUBTILES
    wbf_refs, cw_refs, cb_refs = (wg_bf_ref, wv_bf_ref), (cwg_ref, cwv_ref), (cbg_ref, cbv_ref)

    @pl.when(m == 0)
    def _():
        wg_bf_ref[...] = wg_ref[...].astype(BF16)
        wv_bf_ref[...] = wv_ref[...].astype(BF16)

    @pl.when(m % tiles_per_seq == 0)
    def _():
        tails_ref[0] = jnp.zeros(tails_ref.shape[1:], F32)

    def products(i):
        a = a_ref[i * sub:(i + 1) * sub, :]
        for g in range(2):
            u = jnp.dot(a, wbf_refs[g][...], preferred_element_type=F32)
            u_ref[i % 2, g] = u
            tails_ref[i + 1, g] = u[sub - SUBLANES:, :]

    def conv(i, g):
        u = u_ref[i % 2, g]
        row = lax.broadcasted_iota(jnp.int32, (SUBLANES, u.shape[1]), 0)
        cw = cw_refs[g][...]
        out = u * cw[CONV_WIDTH - 1:CONV_WIDTH, :] + cb_refs[g][...]
        for back in range(1, CONV_WIDTH):
            shifted = pltpu.roll(u, back, 0)
            top = shifted[:SUBLANES]
            for k in range(back):
                prev_row = tails_ref[i, g, SUBLANES - back + k:SUBLANES - back + k + 1, :]
                top = jnp.where(row == k, prev_row, top)
            shifted = jnp.concatenate([top, shifted[SUBLANES:]], axis=0)
            out = out + shifted * cw[CONV_WIDTH - 1 - back:CONV_WIDTH - back, :]
        return out

    def gated(i):
        gate, val = conv(i, 0), conv(i, 1)
        silu = gate / (1.0 + jnp.exp2(gate * -LOG2_E))
        o_ref[i * sub:(i + 1) * sub, :] = (silu * val).astype(o_ref.dtype)

    products(0)
    for i in range(1, FFN_SUBTILES):
        products(i)
        gated(i - 1)
    gated(FFN_SUBTILES - 1)
    tails_ref[0] = tails_ref[FFN_SUBTILES]


def _ffn_up(a, w_up, conv_w, conv_b, layer, S):
    T, D = a.shape
    F = w_up.shape[2] // 2
    tm = _tile(S, FFN_TOKEN_TILE)
    tn = _tile(F, 512)
    nf = F // tn
    body = functools.partial(_ffn_up_body, tm=tm, tiles_per_seq=S // tm)
    conv_b = conv_b.reshape(conv_b.shape[0], 1, 2 * F)
    sub = tm // FFN_SUBTILES
    vmem = (2 * tm * D * 2 + 2 * (2 * D * tn * 4 + D * tn * 2) + 2 * tm * tn * 2 + 4 * sub * tn * 4
            + 16 * sub * tn * 4)
    return pl.pallas_call(
        body,
        out_shape=jax.ShapeDtypeStruct((T, F), BF16),
        grid=(nf, T // tm),
        in_specs=[pl.BlockSpec((tm, D), lambda n, m: (m, 0)),
                  pl.BlockSpec((None, D, tn), lambda n, m: (layer, 0, n)),
                  pl.BlockSpec((None, D, tn), lambda n, m: (layer, 0, nf + n)),
                  pl.BlockSpec((None, CONV_WIDTH, tn), lambda n, m: (layer, 0, n)),
                  pl.BlockSpec((None, CONV_WIDTH, tn), lambda n, m: (layer, 0, nf + n)),
                  pl.BlockSpec((None, 1, tn), lambda n, m: (layer, 0, n)),
                  pl.BlockSpec((None, 1, tn), lambda n, m: (layer, 0, nf + n))],
        out_specs=pl.BlockSpec((tm, tn), lambda n, m: (m, n)),
        scratch_shapes=[pltpu.VMEM((D, tn), BF16), pltpu.VMEM((D, tn), BF16),
                        pltpu.VMEM((2, 2, sub, tn), F32),
                        pltpu.VMEM((FFN_SUBTILES + 1, 2, SUBLANES, tn), F32)],
        compiler_params=_params(("parallel", "arbitrary"), vmem),
        name="ffn_up_conv_gate",
    )(a, w_up, w_up, conv_w, conv_w, conv_b, conv_b)


def _ple_body(*refs, want_h):
    h_ref, gate_gain_ref, wg_ref, p_ref, wp_ref, g_ref = refs[:6]
    o_ref = refs[6] if want_h else None
    on_ref, wg_bf_ref, wp_bf_ref = refs[-3:]

    @pl.when(pl.program_id(0) == 0)
    def _():
        wg_bf_ref[...] = wg_ref[...].astype(BF16)
        wp_bf_ref[...] = wp_ref[...].astype(BF16)

    h = h_ref[...]
    a = _rms_scaled(h, gate_gain_ref[...]).astype(BF16)
    logits = jnp.dot(a, wg_bf_ref[...], preferred_element_type=F32)
    emb = jnp.dot(p_ref[...].astype(BF16), wp_bf_ref[...], preferred_element_type=F32)
    out = h + emb / (1.0 + jnp.exp(-logits))
    if want_h:
        o_ref[...] = out
    on_ref[...] = _rms_scaled(out, g_ref[...]).astype(on_ref.dtype)


def _ple(h, gate_gain, w_gate, p, w_ple, layer, norm_gain, norm_dtype, want_h):
    T, D = h.shape
    P = p.shape[2]
    tm = _tile(T, FULL_ROW_TM)
    row = pl.BlockSpec((tm, D), lambda m: (m, 0))
    gain = pl.BlockSpec((1, D), lambda m: (0, 0))
    vmem = (D * D * 6 + 2 * tm * P * 4 + P * D * 6 + 4 * tm * D * 4
            + 2 * tm * D * jnp.dtype(norm_dtype).itemsize + 10 * tm * D * 4)
    normed = jax.ShapeDtypeStruct((T, D), norm_dtype)
    outs = pl.pallas_call(
        functools.partial(_ple_body, want_h=want_h),
        out_shape=(jax.ShapeDtypeStruct((T, D), F32), normed) if want_h else (normed,),
        grid=(T // tm,),
        in_specs=[row, gain,
                  pl.BlockSpec((None, D, D), lambda m: (layer, 0, 0), pipeline_mode=pl.Buffered(1)),
                  pl.BlockSpec((None, tm, P), lambda m: (layer, m, 0)),
                  pl.BlockSpec((None, P, D), lambda m: (layer, 0, 0), pipeline_mode=pl.Buffered(1)),
                  gain],
        out_specs=(row, row) if want_h else (row,),
        scratch_shapes=[pltpu.VMEM((D, D), BF16), pltpu.VMEM((P, D), BF16)],
        compiler_params=_params(("arbitrary",), vmem),
        name="gated_layer_embedding",
    )(h, gate_gain.reshape(1, D), w_gate, p, w_ple, norm_gain.reshape(1, D))
    return outs if want_h else (None, outs[0])


def kernel(x, p, positions, attn_norm, w_in_even, w_out_even, w_in_odd, b_forget, w_out_odd, ffn_norm,
           w_up, conv_w, conv_b, w_down, ple_norm, w_ple_gate, w_ple, final_norm):
    B, S, D = x.shape
    T = B * S
    depth = p.shape[0]
    H = D // HEAD_DIM
    n_sb = H // 2
    n_dil = H - n_sb
    h = x.reshape(T, D)
    p = p.reshape(depth, T, p.shape[-1])
    w_in_odd_t = jnp.swapaxes(w_in_odd, 1, 2)
    rope_tabs = _rope_tables(positions)
    a = _rmsnorm(h, attn_norm[0], BF16)
    for i in range(depth):
        if i % 2 == 0:
            scale = HEAD_DIM ** -0.5
            qkv_sb = _qkv_proj(a, w_in_even, i // 2, LOG2_E * scale, 0, n_sb, BF16)
            qkv_dil = _qkv_proj(a, w_in_even, i // 2, scale, n_sb, n_dil, F32, rope_tabs)
            mixed = [_sb_attention(qkv_sb, B, S, n_sb), _dilated_attention(qkv_dil, B, S, n_dil)]
            w_out = w_out_even
        else:
            qkv = _qkv_proj(a, w_in_odd_t, i // 2, LOG2_E * HEAD_DIM ** -0.5, 0, H, BF16, w_transposed=True)
            c = _forget_cumsum(a, w_in_odd_t[i // 2, 3 * D:, :], b_forget[i // 2], B, S)
            mixed = [_fox_attention(qkv, c, B, S, D)]
            w_out = w_out_odd
        h, a = _residual_matmul(h, mixed, w_out, i // 2, norm_gain=ffn_norm[i])
        act = _ffn_up(a, w_up, conv_w, conv_b, i, S)
        h = _residual_matmul(h, [act], w_down, i, tm_pref=256)
        last = i == depth - 1
        h, a = _ple(h, ple_norm[i], w_ple_gate, p, w_ple, i,
                    final_norm if last else attn_norm[i + 1], F32 if last else BF16, want_h=not last)
    return a.reshape(B, S, D)
```

```python
import functools

import jax
import jax.numpy as jnp
from jax import lax
from jax.experimental import pallas as pl
from jax.experimental.pallas import tpu as pltpu

HEAD_DIM = 128
ROPE_DIM = HEAD_DIM // 4
ROPE_THETA = 500000.0
DILATED_CONFIGS = ((128, 1), (512, 4), (2048, 16))
BAND_BLOCK = 128
DILATED_SPAN = 2048
RMS_EPS = 1e-6
CONV_WIDTH = 3
MASK_VALUE = -1e30
ATTN_HEADS_PER_STEP = 4
LOG2_E = 1.4426950408889634
GATE_PARTS = 3
SB_SUFFIX_PARTS = 1
MATMUL_TN = 1024
FULL_ROW_TM = 256
FFN_TOKEN_TILE = 2048
FFN_SUBTILES = 4
SUBLANES = 8
VMEM_CAP_BYTES = 60 * 1024 * 1024

F32 = jnp.float32
BF16 = jnp.bfloat16


def _params(semantics, vmem_bytes):
    return pltpu.CompilerParams(dimension_semantics=semantics,
                                vmem_limit_bytes=int(min(vmem_bytes, VMEM_CAP_BYTES)))


def _tile(dim, pref):
    t = min(pref, dim)
    while dim % t:
        t //= 2
    return t


def _nt_dot(a, b):
    return lax.dot_general(a, b, (((1,), (1,)), ((), ())), preferred_element_type=F32)


def _tn_dot(a, b):
    return lax.dot_general(a, b, (((0,), (0,)), ((), ())), preferred_element_type=F32)


def _softplus_neg_abs(z):
    return jnp.log1p(jnp.exp(-jnp.abs(z)))


def _split_bf16(x, parts):
    out = []
    for _ in range(parts - 1):
        hi = x.astype(BF16)
        out.append(hi)
        x = x - hi.astype(F32)
    out.append(x.astype(BF16))
    return out


def _rmsnorm_body(h_ref, g_ref, o_ref):
    x = h_ref[...]
    ms = jnp.mean(x * x, axis=-1, keepdims=True)
    o_ref[...] = (x * lax.rsqrt(ms + RMS_EPS) * g_ref[...]).astype(o_ref.dtype)


def _rmsnorm(h, g, out_dtype):
    T, D = h.shape
    tm = _tile(T, 512)
    blk = tm * D * 4
    return pl.pallas_call(
        _rmsnorm_body,
        out_shape=jax.ShapeDtypeStruct((T, D), out_dtype),
        grid=(T // tm,),
        in_specs=[pl.BlockSpec((tm, D), lambda m: (m, 0)),
                  pl.BlockSpec((1, D), lambda m: (0, 0))],
        out_specs=pl.BlockSpec((tm, D), lambda m: (m, 0)),
        compiler_params=_params(("parallel",), 6 * blk),
        name="rmsnorm",
    )(h, g.reshape(1, D))


def _rope_table_body(pos_ref, invf_ref, cos_ref, sa_ref, sb_ref):
    ang = pos_ref[...].astype(F32) * invf_ref[...]
    lane = lax.broadcasted_iota(jnp.int32, ang.shape, 1)
    half = ROPE_DIM // 2
    s = jnp.sin(ang)
    cos_ref[...] = jnp.cos(ang)
    sa_ref[...] = jnp.where((lane >= half) & (lane < ROPE_DIM), s, 0.0)
    sb_ref[...] = jnp.where(lane < half, -s, 0.0)


def _rope_tables(positions):
    T = positions.size
    tm = _tile(T, 512)
    half = ROPE_DIM // 2
    inv_freq = ROPE_THETA ** (-jnp.arange(0, ROPE_DIM, 2, dtype=F32) / ROPE_DIM)
    invf = jnp.concatenate([inv_freq, inv_freq, jnp.zeros((HEAD_DIM - 2 * half,), F32)]).reshape(1, HEAD_DIM)
    tab = jax.ShapeDtypeStruct((T, HEAD_DIM), F32)
    spec = pl.BlockSpec((tm, HEAD_DIM), lambda m: (m, 0))
    return pl.pallas_call(
        _rope_table_body,
        out_shape=(tab, tab, tab),
        grid=(T // tm,),
        in_specs=[pl.BlockSpec((tm, 1), lambda m: (m, 0)),
                  pl.BlockSpec((1, HEAD_DIM), lambda m: (0, 0))],
        out_specs=(spec, spec, spec),
        compiler_params=_params(("parallel",), 32 * tm * HEAD_DIM * 4),
        name="rope_tables",
    )(positions.reshape(T, 1), invf)


def _qkv_body(*refs, rope, scale, tiles_per_section, heads_per_tile, w_transposed):
    if rope:
        a_ref, w_ref, cos_ref, sa_ref, sb_ref, o_ref, wbf_ref = refs
    else:
        a_ref, w_ref, o_ref, wbf_ref = refs
    n = pl.program_id(0)

    @pl.when(pl.program_id(1) == 0)
    def _():
        wbf_ref[...] = w_ref[...].astype(BF16)

    if w_transposed:
        acc = _nt_dot(a_ref[...], wbf_ref[...])
    else:
        acc = jnp.dot(a_ref[...], wbf_ref[...], preferred_element_type=F32)
    acc = acc * jnp.where(n < tiles_per_section, scale, 1.0).astype(F32)
    if not rope:
        o_ref[...] = acc.astype(o_ref.dtype)
        return

    @pl.when(n < 2 * tiles_per_section)
    def _():
        c, sa, sb = cos_ref[...], sa_ref[...], sb_ref[...]
        half = ROPE_DIM // 2
        for j in range(heads_per_tile):
            x = acc[:, j * HEAD_DIM:(j + 1) * HEAD_DIM]
            y = x * c + pltpu.roll(x, half, 1) * sa + pltpu.roll(x, HEAD_DIM - half, 1) * sb
            o_ref[:, j * HEAD_DIM:(j + 1) * HEAD_DIM] = y.astype(o_ref.dtype)

    @pl.when(n >= 2 * tiles_per_section)
    def _():
        o_ref[...] = acc.astype(o_ref.dtype)


def _qkv_proj(a, w, layer, q_scale, first_head, n_heads, out_dtype, rope_tabs=None, w_transposed=False):
    T, D = a.shape
    rope = rope_tabs is not None
    width = n_heads * HEAD_DIM
    tm = _tile(T, 512)
    tn = _tile(width, MATMUL_TN)
    tiles = width // tn
    per_section, first = D // tn, first_head * HEAD_DIM // tn
    body = functools.partial(_qkv_body, rope=rope, scale=q_scale, tiles_per_section=tiles,
                             heads_per_tile=tn // HEAD_DIM, w_transposed=w_transposed)

    def w_tile(n):
        return (n // tiles) * per_section + first + n % tiles

    if w_transposed:
        w_spec = pl.BlockSpec((None, tn, D), lambda n, m: (layer, w_tile(n), 0))
    else:
        w_spec = pl.BlockSpec((None, D, tn), lambda n, m: (layer, 0, w_tile(n)))
    in_specs = [pl.BlockSpec((tm, D), lambda n, m: (m, 0)), w_spec]
    args = [a, w]
    if rope:
        in_specs += [pl.BlockSpec((tm, HEAD_DIM), lambda n, m: (m, 0))] * 3
        args += list(rope_tabs)
    vmem = 2 * tm * D * 2 + 2 * D * tn * 4 + D * tn * 2 + 8 * tm * tn * 4 + 6 * tm * HEAD_DIM * 4
    return pl.pallas_call(
        body,
        out_shape=jax.ShapeDtypeStruct((T, 3 * width), out_dtype),
        grid=(3 * tiles, T // tm),
        in_specs=in_specs,
        out_specs=pl.BlockSpec((tm, tn), lambda n, m: (m, n)),
        scratch_shapes=[pltpu.VMEM((tn, D) if w_transposed else (D, tn), BF16)],
        compiler_params=_params(("parallel", "arbitrary"), vmem),
        name="qkv_proj",
    )(*args)


def _sb_body(q_ref, k_ref, v_ref, o_ref, acc_ref, carry_ref, sa_ref, sb_ref, *, tq, tk, heads):
    qi = pl.program_id(2)
    q = q_ref[...]
    acc_ref[...] = jnp.zeros_like(acc_ref)
    carry_ref[...] = jnp.zeros_like(carry_ref)
    strict_upper = (lax.broadcasted_iota(jnp.int32, (tk, tk), 1)
                    > lax.broadcasted_iota(jnp.int32, (tk, tk), 0)).astype(BF16)
    strict_upper_n = jnp.concatenate([strict_upper] * SB_SUFFIX_PARTS, axis=1)

    heads_sl = [slice(hh * HEAD_DIM, (hh + 1) * HEAD_DIM) for hh in range(heads)]

    def score_products(j, s_ref):
        start = pl.multiple_of(j * tk, tk)
        for hh, sl in enumerate(heads_sl):
            s_ref[hh] = _nt_dot(k_ref[pl.ds(start, tk), sl], q[:, sl])

    def block(j, s_ref, masked):
        start = pl.multiple_of(j * tk, tk)
        if masked:
            kpos = j * tk + lax.broadcasted_iota(jnp.int32, (tk, tq), 0)
            qpos = qi * tq + lax.broadcasted_iota(jnp.int32, (tk, tq), 1)
            mask = kpos < qpos
        scores = [s_ref[hh] for hh in range(heads)]
        costs =[jnp.maximum(z, 0.0) + jnp.log2(1.0 + jnp.exp2(-jnp.abs(z))) for z in scores]
        if masked:
            costs = [jnp.where(mask, cost, 0.0) for cost in costs]
        suffixes = [jnp.dot(strict_upper_n, jnp.concatenate(_split_bf16(cost, SB_SUFFIX_PARTS), axis=0),
                            preferred_element_type=F32) for cost in costs]
        for hh, sl in enumerate(heads_sl):
            carry = carry_ref[hh]
            w = jnp.exp2(scores[hh] - costs[hh] - suffixes[hh] - carry)
            if masked:
                w = jnp.where(mask, w, 0.0)
            acc_ref[hh] += _tn_dot(v_ref[pl.ds(start, tk), sl], w.astype(BF16))
            carry_ref[hh] = carry + suffixes[hh][0:1, :] + costs[hh][0:1, :]

    first = 2 * qi + 1
    score_products(first, sa_ref)
    score_products(first - 1, sb_ref)
    block(first, sa_ref, True)
    score_products(jnp.maximum(first - 2, 0), sa_ref)
    block(first - 1, sb_ref, True)

    def pair(i, c):
        j = first - 2 - 2 * i
        score_products(j - 1, sb_ref)
        block(j, sa_ref, False)
        score_products(jnp.maximum(j - 2, 0), sa_ref)
        block(j - 1, sb_ref, False)
        return c

    lax.fori_loop(0, qi, pair, 0)
    for hh in range(heads):
        o_ref[:, hh * HEAD_DIM:(hh + 1) * HEAD_DIM] = acc_ref[hh].T.astype(o_ref.dtype)


def _sb_attention(qkv, B, S, n_heads):
    T = B * S
    tq = _tile(S, 512)
    tk = tq // 2
    nq = S // tq
    heads = _tile(n_heads, ATTN_HEADS_PER_STEP)
    width = heads * HEAD_DIM
    hb = n_heads // heads
    body = functools.partial(_sb_body, tq=tq, tk=tk, heads=heads)
    vmem = heads * (4 * S * HEAD_DIM * 2 + 4 * tq * HEAD_DIM * 2 + tq * HEAD_DIM * 4 + SUBLANES * tq * 4
                    + 14 * tq * tk * 4)
    return pl.pallas_call(
        body,
        out_shape=jax.ShapeDtypeStruct((T, n_heads * HEAD_DIM), BF16),
        grid=(B, n_heads // heads, nq),
        in_specs=[pl.BlockSpec((tq, width), lambda b, h, i: (b * nq + i, h)),
                  pl.BlockSpec((S, width), lambda b, h, i: (b, hb + h)),
                  pl.BlockSpec((S, width), lambda b, h, i: (b, 2 * hb + h))],
        out_specs=pl.BlockSpec((tq, width), lambda b, h, i: (b * nq + i, h)),
        scratch_shapes=[pltpu.VMEM((heads, HEAD_DIM, tq), F32), pltpu.VMEM((heads, 1, tq), F32),
                        pltpu.VMEM((heads, tk, tq), F32), pltpu.VMEM((heads, tk, tq), F32)],
        compiler_params=_params(("parallel", "parallel", "parallel"), vmem),
        name="stick_breaking_attention",
    )(qkv, qkv, qkv)


def _dilated_body(q_ref, k_ref, v_ref, *refs, configs, blk):
    n_cfg = len(configs)
    halos, o_ref = refs[:2 * n_cfg], refs[2 * n_cfg]
    part_o, part_lse = refs[2 * n_cfg + 1:3 * n_cfg + 1], refs[3 * n_cfg + 1:]
    span = q_ref.shape[0]
    span_idx = pl.program_id(2)
    row = lax.broadcasted_iota(jnp.int32, (blk, 2 * blk), 0)
    col = lax.broadcasted_iota(jnp.int32, (blk, 2 * blk), 1)
    dist = row + blk - col

    for c, (window, dil) in enumerate(configs):
        kp_ref, vp_ref = halos[2 * c], halos[2 * c + 1]
        band = (dist >= 0) & (dist <= window // dil)
        band_first = band & ((span_idx > 0) | (col >= blk))

        def rows(ref, start, dil=dil):
            return ref[pl.ds(start, blk, stride=dil), :] if dil > 1 else ref[pl.ds(start, blk), :]

        for r in range(dil):
            k_prev, v_prev = rows(kp_ref, r).astype(BF16), rows(vp_ref, r).astype(BF16)
            for j in range(span // (dil * blk)):
                start = r + dil * blk * j
                k_run, v_run = rows(k_ref, start).astype(BF16), rows(v_ref, start).astype(BF16)
                kk = jnp.concatenate([k_prev, k_run], axis=0)
                vv = jnp.concatenate([v_prev, v_run], axis=0)
                k_prev, v_prev = k_run, v_run
                s = _nt_dot(rows(q_ref, start).astype(BF16), kk)
                s = jnp.where(band if j else band_first, s, MASK_VALUE)
                m = jnp.max(s, axis=1, keepdims=True)
                p = jnp.exp(s - m)
                l = jnp.sum(p, axis=1, keepdims=True)
                out = jnp.dot(p.astype(BF16), vv, preferred_element_type=F32) / l
                lse = jnp.broadcast_to(m + jnp.log(l), (blk, HEAD_DIM))
                if dil > 1:
                    part_o[c][pl.ds(start, blk, stride=dil), :] = out
                    part_lse[c][pl.ds(start, blk, stride=dil), :] = lse
                else:
                    part_o[c][pl.ds(start, blk), :] = out
                    part_lse[c][pl.ds(start, blk), :] = lse

    top = part_lse[0][...]
    for lse_ref in part_lse[1:]:
        top = jnp.maximum(top, lse_ref[...])
    num = jnp.zeros(o_ref.shape, F32)
    den = jnp.zeros(o_ref.shape, F32)
    for out_ref, lse_ref in zip(part_o, part_lse):
        e = jnp.exp(lse_ref[...] - top)
        num = num + e * out_ref[...]
        den = den + e
    o_ref[...] = (num / den).astype(o_ref.dtype)


def _dilated_attention(qkv, B, S, n_heads):
    T = B * S
    blk = BAND_BLOCK
    span = _tile(S, DILATED_SPAN)
    spans = S // span

    def cur(section):
        return pl.BlockSpec((span, HEAD_DIM), lambda b, h, i: (b * spans + i, section * n_heads + h))

    def halo(section, dil):
        rows = dil * blk
        return pl.BlockSpec((rows, HEAD_DIM), lambda b, h, i: (
            jnp.maximum((b * spans + i) * (span // rows) - 1, 0), section * n_heads + h))

    halo_specs = [halo(section, dil) for _, dil in DILATED_CONFIGS for section in (1, 2)]
    n_cfg = len(DILATED_CONFIGS)
    body = functools.partial(_dilated_body, configs=DILATED_CONFIGS, blk=blk)
    halo_rows = sum(dil * blk for _, dil in DILATED_CONFIGS)
    vmem = (2 * (3 * span + 2 * halo_rows) * HEAD_DIM * 4 + 2 * n_cfg * span * HEAD_DIM * 4
            + 10 * span * HEAD_DIM * 4 + 24 * blk * 2 * blk * 4)
    return pl.pallas_call(
        body,
        out_shape=jax.ShapeDtypeStruct((T, n_heads * HEAD_DIM), BF16),
        grid=(B, n_heads, spans),
        in_specs=[cur(0), cur(1), cur(2)] + halo_specs,
        out_specs=pl.BlockSpec((span, HEAD_DIM), lambda b, h, i: (b * spans + i, h)),
        scratch_shapes=[pltpu.VMEM((span, HEAD_DIM), F32) for _ in range(2 * n_cfg)],
        compiler_params=_params(("parallel", "parallel", "parallel"), vmem),
        name="dilated_attention",
    )(*([qkv] * (3 + 2 * n_cfg)))


def _forget_body(a_ref, w_ref, b_ref, c_ref, carry_ref, *, ts):
    @pl.when(pl.program_id(1) == 0)
    def _():
        carry_ref[...] = jnp.zeros_like(carry_ref)

    x = _nt_dot(a_ref[...], w_ref[...].astype(BF16)) + b_ref[...]
    log_f = jnp.minimum(x, 0.0) - _softplus_neg_abs(x)
    lower = (lax.broadcasted_iota(jnp.int32, (ts, ts), 0)
             >= lax.broadcasted_iota(jnp.int32, (ts, ts), 1)).astype(BF16)
    cum = carry_ref[...]
    for part in _split_bf16(log_f, 3):
        cum = cum + jnp.dot(lower, part, preferred_element_type=F32)
    c_ref[...] = cum
    carry_ref[...] = cum[ts - 1:ts, :]


def _forget_cumsum(a, w_f, b_f, B, S):
    T, D = a.shape
    H = w_f.shape[0]
    ts = _tile(S, 512)
    ns = S // ts
    body = functools.partial(_forget_body, ts=ts)
    vmem = 4 * ts * D * 2 + 4 * D * 128 * 4 + 8 * ts * ts * 4
    return pl.pallas_call(
        body,
        out_shape=jax.ShapeDtypeStruct((T, H), F32),
        grid=(B, ns),
        in_specs=[pl.BlockSpec((ts, D), lambda b, s: (b * ns + s, 0)),
                  pl.BlockSpec((H, D), lambda b, s: (0, 0)),
                  pl.BlockSpec((1, H), lambda b, s: (0, 0))],
        out_specs=pl.BlockSpec((ts, H), lambda b, s: (b * ns + s, 0)),
        scratch_shapes=[pltpu.VMEM((1, H), F32)],
        compiler_params=_params(("parallel", "arbitrary"), vmem),
        name="forget_cumsum",
    )(a, w_f, b_f.reshape(1, H))


def _gate_lanes(parts, head, first_lane, sign):
    n_heads = parts[0].shape[1]
    row = lax.broadcasted_iota(jnp.int32, (n_heads, HEAD_DIM), 0)
    lane = lax.broadcasted_iota(jnp.int32, (n_heads, HEAD_DIM), 1)
    lane_row = lax.broadcasted_iota(jnp.int32, (1, HEAD_DIM), 1)
    gate_lane = (lane_row >= first_lane) & (lane_row < first_lane + GATE_PARTS)
    out = jnp.where((lane_row < 2 * GATE_PARTS) & jnp.logical_not(gate_lane), 1.0, 0.0)
    for i, part in enumerate(parts):
        place = jnp.where((row == head) & (lane == first_lane + i), sign, 0.0).astype(BF16)
        out = out + jnp.dot(part, place, preferred_element_type=F32)
    return out.astype(BF16)


def _fox_body(q_ref, k_ref, v_ref, c_ref, o_ref, acc_ref, m_ref, l_ref, kx_ref, sa_ref, sb_ref, *, tq, tk, heads):
    hg = pl.program_id(1)
    qi = pl.program_id(2)
    heads_sl = [slice(hh * HEAD_DIM, (hh + 1) * HEAD_DIM) for hh in range(heads)]

    @pl.when(qi == 0)
    def _():
        parts = _split_bf16(c_ref[...] * LOG2_E, GATE_PARTS)
        for hh, sl in enumerate(heads_sl):
            kx_ref[hh, :, :HEAD_DIM] = k_ref[:, sl]
            kx_ref[hh, :, HEAD_DIM:] = _gate_lanes(parts, hg * heads + hh, GATE_PARTS, -1.0)

    q_parts = _split_bf16(c_ref[pl.ds(pl.multiple_of(qi * tq, tq), tq), :] * LOG2_E, GATE_PARTS)
    qx = [jnp.concatenate([q_ref[:, sl], _gate_lanes(q_parts, hg * heads + hh, 0, 1.0)], axis=1)
          for hh, sl in enumerate(heads_sl)]
    acc_ref[...] = jnp.zeros_like(acc_ref)
    l_ref[...] = jnp.zeros_like(l_ref)
    m_ref[...] = jnp.full_like(m_ref, MASK_VALUE)

    def score_products(j, s_ref):
        start = pl.multiple_of(j * tk, tk)
        for hh in range(heads):
            s_ref[hh] = _nt_dot(kx_ref[hh, pl.ds(start, tk), :], qx[hh])

    def block(j, s_ref, masked):
        start = pl.multiple_of(j * tk, tk)
        if masked:
            kpos = j * tk + lax.broadcasted_iota(jnp.int32, (tk, tq), 0)
            qpos = qi * tq + lax.broadcasted_iota(jnp.int32, (tk, tq), 1)
            mask = kpos <= qpos
        for hh, sl in enumerate(heads_sl):
            s = s_ref[hh]
            if masked:
                s = jnp.where(mask, s, MASK_VALUE)
            m_old = m_ref[hh]
            m_new = jnp.maximum(m_old, jnp.max(s, axis=0, keepdims=True))
            p = jnp.exp2(s - m_new)
            alpha = jnp.exp2(m_old - m_new)
            l_ref[hh] = alpha * l_ref[hh] + jnp.sum(p, axis=0, keepdims=True)
            acc_ref[hh] = alpha * acc_ref[hh] + _tn_dot(v_ref[pl.ds(start, tk), sl], p.astype(BF16))
            m_ref[hh] = m_new

    def pair(i, c):
        j = 2 * i
        score_products(j + 1, sb_ref)
        block(j, sa_ref, False)
        score_products(j + 2, sa_ref)
        block(j + 1, sb_ref, False)
        return c

    score_products(0, sa_ref)
    lax.fori_loop(0, qi // 2, pair, 0)

    @pl.when(qi % 2 == 0)
    def _():
        block(qi, sa_ref, True)

    @pl.when(qi % 2 == 1)
    def _():
        score_products(qi, sb_ref)
        block(qi - 1, sa_ref, False)
        block(qi, sb_ref, True)

    for hh in range(heads):
        o_ref[:, hh * HEAD_DIM:(hh + 1) * HEAD_DIM] = (acc_ref[hh] / l_ref[hh]).T.astype(o_ref.dtype)


def _fox_attention(qkv, c, B, S, D):
    T = B * S
    H = D // HEAD_DIM
    tq = tk = _tile(S, 512)
    nq = S // tq
    heads = _tile(H, ATTN_HEADS_PER_STEP)
    width = heads * HEAD_DIM
    hb = D // width
    body = functools.partial(_fox_body, tq=tq, tk=tk, heads=heads)
    vmem = (2 * S * 128 * 4 + heads * (4 * S * HEAD_DIM * 2 + 4 * tq * HEAD_DIM * 2 + S * 2 * HEAD_DIM * 2
                                       + tq * HEAD_DIM * 4 + 2 * SUBLANES * tq * 4 + 12 * tq * tk * 4))
    return pl.pallas_call(
        body,
        out_shape=jax.ShapeDtypeStruct((T, D), BF16),
        grid=(B, hb, nq),
        in_specs=[pl.BlockSpec((tq, width), lambda b, h, i: (b * nq + i, h)),
                  pl.BlockSpec((S, width), lambda b, h, i: (b, hb + h)),
                  pl.BlockSpec((S, width), lambda b, h, i: (b, 2 * hb + h)),
                  pl.BlockSpec((S, H), lambda b, h, i: (b, 0))],
        out_specs=pl.BlockSpec((tq, width), lambda b, h, i: (b * nq + i, h)),
        scratch_shapes=[pltpu.VMEM((heads, HEAD_DIM, tq), F32), pltpu.VMEM((heads, 1, tq), F32),
                        pltpu.VMEM((heads, 1, tq), F32), pltpu.VMEM((heads, S, 2 * HEAD_DIM), BF16),
                        pltpu.VMEM((heads, tk, tq), F32), pltpu.VMEM((heads, tk, tq), F32)],
        compiler_params=_params(("parallel", "parallel", "arbitrary"), vmem),
        name="forgetting_attention",
    )(qkv, qkv, qkv, c)


def _rms_scaled(x, g):
    return x * lax.rsqrt(jnp.mean(x * x, axis=-1, keepdims=True) + RMS_EPS) * g


def _residual_mm_body(*refs, n_parts, norm):
    a_refs = refs[:n_parts]
    w_refs = refs[n_parts:2 * n_parts]
    rest = refs[2 * n_parts:]
    if norm:
        h_ref, g_ref, o_ref, on_ref = rest[:4]
        wbf_refs = rest[4:]
    else:
        h_ref, o_ref = rest[:2]
        wbf_refs = rest[2:]

    @pl.when(pl.program_id(1) == 0)
    def _():
        for w_ref, wbf_ref in zip(w_refs, wbf_refs):
            wbf_ref[...] = w_ref[...].astype(BF16)

    acc = h_ref[...]
    for a_ref, wbf_ref in zip(a_refs, wbf_refs):
        acc = acc + jnp.dot(a_ref[...], wbf_ref[...], preferred_element_type=F32)
    o_ref[...] = acc
    if norm:
        on_ref[...] = _rms_scaled(acc, g_ref[...]).astype(on_ref.dtype)


def _residual_matmul(h, parts, w, layer, tm_pref=512, norm_gain=None):
    T, N = h.shape
    n_parts = len(parts)
    kp = parts[0].shape[1]
    norm = norm_gain is not None
    tm = _tile(T, FULL_ROW_TM if norm else tm_pref)
    tn = N if norm else _tile(N, MATMUL_TN)
    body = functools.partial(_residual_mm_body, n_parts=n_parts, norm=norm)
    a_specs = [pl.BlockSpec((tm, kp), lambda n, m: (m, 0)) for _ in parts]
    w_specs = [pl.BlockSpec((None, kp, tn), lambda n, m, i=i: (layer, i, n), pipeline_mode=pl.Buffered(1))
               for i in range(n_parts)]
    hspec = pl.BlockSpec((tm, tn), lambda n, m: (m, n))
    vmem = n_parts * (2 * tm * kp * 2 + kp * tn * 4 + kp * tn * 2) + 8 * tm * tn * 4
    extra_in, extra_args, out_shape, out_specs = [], [], jax.ShapeDtypeStruct((T, N), F32), hspec
    if norm:
        extra_in, extra_args = [pl.BlockSpec((1, N), lambda n, m: (0, 0))], [norm_gain.reshape(1, N)]
        out_shape, out_specs = (out_shape, jax.ShapeDtypeStruct((T, N), BF16)), (hspec, hspec)
        vmem += 4 * tm * tn * 4
    return pl.pallas_call(
        body,
        out_shape=out_shape,
        grid=(N // tn, T // tm),
        in_specs=a_specs + w_specs + [hspec] + extra_in,
        out_specs=out_specs,
        scratch_shapes=[pltpu.VMEM((kp, tn), BF16) for _ in parts],
        compiler_params=_params(("parallel", "arbitrary"), vmem),
        name="residual_matmul",
    )(*parts, *([w] * n_parts), h, *extra_args)


def _ffn_up_body(a_ref, wg_ref, wv_ref, cwg_ref, cwv_ref, cbg_ref, cbv_ref, o_ref,
                 wg_bf_ref, wv_bf_ref, u_ref, tails_ref, *, tm, tiles_per_seq):
    m = pl.program_id(1)
    sub = tm // FFN_SUBTILES
    wbf_refs, cw_refs, cb_refs = (wg_bf_ref, wv_bf_ref), (cwg_ref, cwv_ref), (cbg_ref, cbv_ref)

    @pl.when(m == 0)
    def _():
        wg_bf_ref[...] = wg_ref[...].astype(BF16)
        wv_bf_ref[...] = wv_ref[...].astype(BF16)

    @pl.when(m % tiles_per_seq == 0)
    def _():
        tails_ref[0] = jnp.zeros(tails_ref.shape[1:], F32)

    def products(i):
        a = a_ref[i * sub:(i + 1) * sub, :]
        for g in range(2):
            u = jnp.dot(a, wbf_refs[g][...], preferred_element_type=F32)
            u_ref[i % 2, g] = u
            tails_ref[i + 1, g] = u[sub - SUBLANES:, :]

    def conv(i, g):
        u = u_ref[i % 2, g]
        row = lax.broadcasted_iota(jnp.int32, (SUBLANES, u.shape[1]), 0)
        cw = cw_refs[g][...]
        out = u * cw[CONV_WIDTH - 1:CONV_WIDTH, :] + cb_refs[g][...]
        for back in range(1, CONV_WIDTH):
            shifted = pltpu.roll(u, back, 0)
            top = shifted[:SUBLANES]
            for k in range(back):
                prev_row = tails_ref[i, g, SUBLANES - back + k:SUBLANES - back + k + 1, :]
                top = jnp.where(row == k, prev_row, top)
            shifted = jnp.concatenate([top, shifted[SUBLANES:]], axis=0)
            out = out + shifted * cw[CONV_WIDTH - 1 - back:CONV_WIDTH - back, :]
        return out

    def gated(i):
        gate, val = conv(i, 0), conv(i, 1)
        silu = gate / (1.0 + jnp.exp2(gate * -LOG2_E))
        o_ref[i * sub:(i + 1) * sub, :] = (silu * val).astype(o_ref.dtype)

    products(0)
    for i in range(1, FFN_SUBTILES):
        products(i)
        gated(i - 1)
    gated(FFN_SUBTILES - 1)
    tails_ref[0] = tails_ref[FFN_SUBTILES]


def _ffn_up(a, w_up, conv_w, conv_b, layer, S):
    T, D = a.shape
    F = w_up.shape[2] // 2
    tm = _tile(S, FFN_TOKEN_TILE)
    tn = _tile(F, 512)
    nf = F // tn
    body = functools.partial(_ffn_up_body, tm=tm, tiles_per_seq=S // tm)
    conv_b = conv_b.reshape(conv_b.shape[0], 1, 2 * F)
    sub = tm // FFN_SUBTILES
    vmem = (2 * tm * D * 2 + 2 * (2 * D * tn * 4 + D * tn * 2) + 2 * tm * tn * 2 + 4 * sub * tn * 4
            + 16 * sub * tn * 4)
    return pl.pallas_call(
        body,
        out_shape=jax.ShapeDtypeStruct((T, F), BF16),
        grid=(nf, T // tm),
        in_specs=[pl.BlockSpec((tm, D), lambda n, m: (m, 0)),
                  pl.BlockSpec((None, D, tn), lambda n, m: (layer, 0, n)),
                  pl.BlockSpec((None, D, tn), lambda n, m: (layer, 0, nf + n)),
                  pl.BlockSpec((None, CONV_WIDTH, tn), lambda n, m: (layer, 0, n)),
                  pl.BlockSpec((None, CONV_WIDTH, tn), lambda n, m: (layer, 0, nf + n)),
                  pl.BlockSpec((None, 1, tn), lambda n, m: (layer, 0, n)),
                  pl.BlockSpec((None, 1, tn), lambda n, m: (layer, 0, nf + n))],
        out_specs=pl.BlockSpec((tm, tn), lambda n, m: (m, n)),
        scratch_shapes=[pltpu.VMEM((D, tn), BF16), pltpu.VMEM((D, tn), BF16),
                        pltpu.VMEM((2, 2, sub, tn), F32),
                        pltpu.VMEM((FFN_SUBTILES + 1, 2, SUBLANES, tn), F32)],
        compiler_params=_params(("parallel", "arbitrary"), vmem),
        name="ffn_up_conv_gate",
    )(a, w_up, w_up, conv_w, conv_w, conv_b, conv_b)


def _ple_body(*refs, want_h):
    h_ref, gate_gain_ref, wg_ref, p_ref, wp_ref, g_ref = refs[:6]
    o_ref = refs[6] if want_h else None
    on_ref, wg_bf_ref, wp_bf_ref = refs[-3:]

    @pl.when(pl.program_id(0) == 0)
    def _():
        wg_bf_ref[...] = wg_ref[...].astype(BF16)
        wp_bf_ref[...] = wp_ref[...].astype(BF16)

    h = h_ref[...]
    a = _rms_scaled(h, gate_gain_ref[...]).astype(BF16)
    logits = jnp.dot(a, wg_bf_ref[...], preferred_element_type=F32)
    emb = jnp.dot(p_ref[...].astype(BF16), wp_bf_ref[...], preferred_element_type=F32)
    out = h + emb / (1.0 + jnp.exp(-logits))
    if want_h:
        o_ref[...] = out
    on_ref[...] = _rms_scaled(out, g_ref[...]).astype(on_ref.dtype)


def _ple(h, gate_gain, w_gate, p, w_ple, layer, norm_gain, norm_dtype, want_h):
    T, D = h.shape
    P = p.shape[2]
    tm = _tile(T, FULL_ROW_TM)
    row = pl.BlockSpec((tm, D), lambda m: (m, 0))
    gain = pl.BlockSpec((1, D), lambda m: (0, 0))
    vmem = (D * D * 6 + 2 * tm * P * 4 + P * D * 6 + 4 * tm * D * 4
            + 2 * tm * D * jnp.dtype(norm_dtype).itemsize + 10 * tm * D * 4)
    normed = jax.ShapeDtypeStruct((T, D), norm_dtype)
    outs = pl.pallas_call(
        functools.partial(_ple_body, want_h=want_h),
        out_shape=(jax.ShapeDtypeStruct((T, D), F32), normed) if want_h else (normed,),
        grid=(T // tm,),
        in_specs=[row, gain,
                  pl.BlockSpec((None, D, D), lambda m: (layer, 0, 0), pipeline_mode=pl.Buffered(1)),
                  pl.BlockSpec((None, tm, P), lambda m: (layer, m, 0)),
                  pl.BlockSpec((None, P, D), lambda m: (layer, 0, 0), pipeline_mode=pl.Buffered(1)),
                  gain],
        out_specs=(row, row) if want_h else (row,),
        scratch_shapes=[pltpu.VMEM((D, D), BF16), pltpu.VMEM((P, D), BF16)],
        compiler_params=_params(("arbitrary",), vmem),
        name="gated_layer_embedding",
    )(h, gate_gain.reshape(1, D), w_gate, p, w_ple, norm_gain.reshape(1, D))
    return outs if want_h else (None, outs[0])


def kernel(x, p, positions, attn_norm, w_in_even, w_out_even, w_in_odd, b_forget, w_out_odd, ffn_norm,
           w_up, conv_w, conv_b, w_down, ple_norm, w_ple_gate, w_ple, final_norm):
    B, S, D = x.shape
    T = B * S
    depth = p.shape[0]
    H = D // HEAD_DIM
    n_sb = H // 2
    n_dil = H - n_sb
    h = x.reshape(T, D)
    p = p.reshape(depth, T, p.shape[-1])
    w_in_odd_t = jnp.swapaxes(w_in_odd, 1, 2)
    rope_tabs = _rope_tables(positions)
    a = _rmsnorm(h, attn_norm[0], BF16)
    for i in range(depth):
        if i % 2 == 0:
            scale = HEAD_DIM ** -0.5
            qkv_sb = _qkv_proj(a, w_in_even, i // 2, LOG2_E * scale, 0, n_sb, BF16)
            qkv_dil = _qkv_proj(a, w_in_even, i // 2, scale, n_sb, n_dil, F32, rope_tabs)
            mixed = [_sb_attention(qkv_sb, B, S, n_sb), _dilated_attention(qkv_dil, B, S, n_dil)]
            w_out = w_out_even
        else:
            qkv = _qkv_proj(a, w_in_odd_t, i // 2, LOG2_E * HEAD_DIM ** -0.5, 0, H, BF16, w_transposed=True)
            c = _forget_cumsum(a, w_in_odd_t[i // 2, 3 * D:, :], b_forget[i // 2], B, S)
            mixed = [_fox_attention(qkv, c, B, S, D)]
            w_out = w_out_odd
        h, a = _residual_matmul(h, mixed, w_out, i // 2, norm_gain=ffn_norm[i])
        act = _ffn_up(a, w_up, conv_w, conv_b, i, S)
        h = _residual_matmul(h, [act], w_down, i, tm_pref=256)
        last = i == depth - 1
        h, a = _ple(h, ple_norm[i], w_ple_gate, p, w_ple, i,
                    final_norm if last else attn_norm[i + 1], F32 if last else BF16, want_h=not last)
    return a.reshape(B, S, D)
```

```python
import functools

import jax
import jax.numpy as jnp
from jax import lax
from jax.experimental import pallas as pl
from jax.experimental.pallas import tpu as pltpu

HEAD_DIM = 128
ROPE_DIM = HEAD_DIM // 4
ROPE_THETA = 500000.0
DILATED_CONFIGS = ((128, 1), (512, 4), (2048, 16))
BAND_BLOCK = 128
DILATED_SPAN = 2048
RMS_EPS = 1e-6
CONV_WIDTH = 3
MASK_VALUE = -1e30
ATTN_HEADS_PER_STEP = 4
LOG2_E = 1.4426950408889634
GATE_PARTS = 3
SB_SUFFIX_PARTS = 1
MATMUL_TN = 1024
FULL_ROW_TM = 256
FFN_TOKEN_TILE = 2048
FFN_SUBTILES = 4
SUBLANES = 8
VMEM_CAP_BYTES = 60 * 1024 * 1024

F32 = jnp.float32
BF16 = jnp.bfloat16


def _params(semantics, vmem_bytes):
    return pltpu.CompilerParams(dimension_semantics=semantics,
                                vmem_limit_bytes=int(min(vmem_bytes, VMEM_CAP_BYTES)))


def _tile(dim, pref):
    t = min(pref, dim)
    while dim % t:
        t //= 2
    return t


def _nt_dot(a, b):
    return lax.dot_general(a, b, (((1,), (1,)), ((), ())), preferred_element_type=F32)


def _tn_dot(a, b):
    return lax.dot_general(a, b, (((0,), (0,)), ((), ())), preferred_element_type=F32)


def _softplus_neg_abs(z):
    return jnp.log1p(jnp.exp(-jnp.abs(z)))


def _split_bf16(x, parts):
    out = []
    for _ in range(parts - 1):
        hi = x.astype(BF16)
        out.append(hi)
        x = x - hi.astype(F32)
    out.append(x.astype(BF16))
    return out


def _rmsnorm_body(h_ref, g_ref, o_ref):
    x = h_ref[...]
    ms = jnp.mean(x * x, axis=-1, keepdims=True)
    o_ref[...] = (x * lax.rsqrt(ms + RMS_EPS) * g_ref[...]).astype(o_ref.dtype)


def _rmsnorm(h, g, out_dtype):
    T, D = h.shape
    tm = _tile(T, 512)
    blk = tm * D * 4
    return pl.pallas_call(
        _rmsnorm_body,
        out_shape=jax.ShapeDtypeStruct((T, D), out_dtype),
        grid=(T // tm,),
        in_specs=[pl.BlockSpec((tm, D), lambda m: (m, 0)),
                  pl.BlockSpec((1, D), lambda m: (0, 0))],
        out_specs=pl.BlockSpec((tm, D), lambda m: (m, 0)),
        compiler_params=_params(("parallel",), 6 * blk),
        name="rmsnorm",
    )(h, g.reshape(1, D))


def _rope_table_body(pos_ref, invf_ref, cos_ref, sa_ref, sb_ref):
    ang = pos_ref[...].astype(F32) * invf_ref[...]
    lane = lax.broadcasted_iota(jnp.int32, ang.shape, 1)
    half = ROPE_DIM // 2
    s = jnp.sin(ang)
    cos_ref[...] = jnp.cos(ang)
    sa_ref[...] = jnp.where((lane >= half) & (lane < ROPE_DIM), s, 0.0)
    sb_ref[...] = jnp.where(lane < half, -s, 0.0)


def _rope_tables(positions):
    T = positions.size
    tm = _tile(T, 512)
    half = ROPE_DIM // 2
    inv_freq = ROPE_THETA ** (-jnp.arange(0, ROPE_DIM, 2, dtype=F32) / ROPE_DIM)
    invf = jnp.concatenate([inv_freq, inv_freq, jnp.zeros((HEAD_DIM - 2 * half,), F32)]).reshape(1, HEAD_DIM)
    tab = jax.ShapeDtypeStruct((T, HEAD_DIM), F32)
    spec = pl.BlockSpec((tm, HEAD_DIM), lambda m: (m, 0))
    return pl.pallas_call(
        _rope_table_body,
        out_shape=(tab, tab, tab),
        grid=(T // tm,),
        in_specs=[pl.BlockSpec((tm, 1), lambda m: (m, 0)),
                  pl.BlockSpec((1, HEAD_DIM), lambda m: (0, 0))],
        out_specs=(spec, spec, spec),
        compiler_params=_params(("parallel",), 32 * tm * HEAD_DIM * 4),
        name="rope_tables",
    )(positions.reshape(T, 1), invf)


def _qkv_body(*refs, rope, scale, tiles_per_section, heads_per_tile, w_transposed):
    if rope:
        a_ref, w_ref, cos_ref, sa_ref, sb_ref, o_ref, wbf_ref = refs
    else:
        a_ref, w_ref, o_ref, wbf_ref = refs
    n = pl.program_id(0)

    @pl.when(pl.program_id(1) == 0)
    def _():
        wbf_ref[...] = w_ref[...].astype(BF16)

    if w_transposed:
        acc = _nt_dot(a_ref[...], wbf_ref[...])
    else:
        acc = jnp.dot(a_ref[...], wbf_ref[...], preferred_element_type=F32)
    acc = acc * jnp.where(n < tiles_per_section, scale, 1.0).astype(F32)
    if not rope:
        o_ref[...] = acc.astype(o_ref.dtype)
        return

    @pl.when(n < 2 * tiles_per_section)
    def _():
        c, sa, sb = cos_ref[...], sa_ref[...], sb_ref[...]
        half = ROPE_DIM // 2
        for j in range(heads_per_tile):
            x = acc[:, j * HEAD_DIM:(j + 1) * HEAD_DIM]
            y = x * c + pltpu.roll(x, half, 1) * sa + pltpu.roll(x, HEAD_DIM - half, 1) * sb
            o_ref[:, j * HEAD_DIM:(j + 1) * HEAD_DIM] = y.astype(o_ref.dtype)

    @pl.when(n >= 2 * tiles_per_section)
    def _():
        o_ref[...] = acc.astype(o_ref.dtype)


def _qkv_proj(a, w, layer, q_scale, first_head, n_heads, out_dtype, rope_tabs=None, w_transposed=False):
    T, D = a.shape
    rope = rope_tabs is not None
    width = n_heads * HEAD_DIM
    tm = _tile(T, 512)
    tn = _tile(width, MATMUL_TN)
    tiles = width // tn
    per_section, first = D // tn, first_head * HEAD_DIM // tn
    body = functools.partial(_qkv_body, rope=rope, scale=q_scale, tiles_per_section=tiles,
                             heads_per_tile=tn // HEAD_DIM, w_transposed=w_transposed)

    def w_tile(n):
        return (n // tiles) * per_section + first + n % tiles

    if w_transposed:
        w_spec = pl.BlockSpec((None, tn, D), lambda n, m: (layer, w_tile(n), 0))
    else:
        w_spec = pl.BlockSpec((None, D, tn), lambda n, m: (layer, 0, w_tile(n)))
    in_specs = [pl.BlockSpec((tm, D), lambda n, m: (m, 0)), w_spec]
    args = [a, w]
    if rope:
        in_specs += [pl.BlockSpec((tm, HEAD_DIM), lambda n, m: (m, 0))] * 3
        args += list(rope_tabs)
    vmem = 2 * tm * D * 2 + 2 * D * tn * 4 + D * tn * 2 + 8 * tm * tn * 4 + 6 * tm * HEAD_DIM * 4
    return pl.pallas_call(
        body,
        out_shape=jax.ShapeDtypeStruct((T, 3 * width), out_dtype),
        grid=(3 * tiles, T // tm),
        in_specs=in_specs,
        out_specs=pl.BlockSpec((tm, tn), lambda n, m: (m, n)),
        scratch_shapes=[pltpu.VMEM((tn, D) if w_transposed else (D, tn), BF16)],
        compiler_params=_params(("parallel", "arbitrary"), vmem),
        name="qkv_proj",
    )(*args)


def _sb_body(q_ref, k_ref, v_ref, o_ref, acc_ref, carry_ref, sa_ref, sb_ref, *, tq, tk, heads):
    qi = pl.program_id(2)
    q = q_ref[...]
    acc_ref[...] = jnp.zeros_like(acc_ref)
    carry_ref[...] = jnp.zeros_like(carry_ref)
    strict_upper = (lax.broadcasted_iota(jnp.int32, (tk, tk), 1)
                    > lax.broadcasted_iota(jnp.int32, (tk, tk), 0)).astype(BF16)
    strict_upper_n = jnp.concatenate([strict_upper] * SB_SUFFIX_PARTS, axis=1)

    heads_sl = [slice(hh * HEAD_DIM, (hh + 1) * HEAD_DIM) for hh in range(heads)]

    def score_products(j, s_ref):
        start = pl.multiple_of(j * tk, tk)
        for hh, sl in enumerate(heads_sl):
            s_ref[hh] = _nt_dot(k_ref[pl.ds(start, tk), sl], q[:, sl])

    def block(j, s_ref, masked):
        start = pl.multiple_of(j * tk, tk)
        if masked:
            kpos = j * tk + lax.broadcasted_iota(jnp.int32, (tk, tq), 0)
            qpos = qi * tq + lax.broadcasted_iota(jnp.int32, (tk, tq), 1)
            mask = kpos < qpos
        scores = [s_ref[hh] for hh in range(heads)]
        costs =[jnp.maximum(z, 0.0) + jnp.log2(1.0 + jnp.exp2(-jnp.abs(z))) for z in scores]
        if masked:
            costs = [jnp.where(mask, cost, 0.0) for cost in costs]
        suffixes = [jnp.dot(strict_upper_n, jnp.concatenate(_split_bf16(cost, SB_SUFFIX_PARTS), axis=0),
                            preferred_element_type=F32) for cost in costs]
        for hh, sl in enumerate(heads_sl):
            carry = carry_ref[hh]
            w = jnp.exp2(scores[hh] - costs[hh] - suffixes[hh] - carry)
            if masked:
                w = jnp.where(mask, w, 0.0)
            acc_ref[hh] += _tn_dot(v_ref[pl.ds(start, tk), sl], w.astype(BF16))
            carry_ref[hh] = carry + suffixes[hh][0:1, :] + costs[hh][0:1, :]

    first = 2 * qi + 1
    score_products(first, sa_ref)
    score_products(first - 1, sb_ref)
    block(first, sa_ref, True)
    score_products(jnp.maximum(first - 2, 0), sa_ref)
    block(first - 1, sb_ref, True)

    def pair(i, c):
        j = first - 2 - 2 * i
        score_products(j - 1, sb_ref)
        block(j, sa_ref, False)
        score_products(jnp.maximum(j - 2, 0), sa_ref)
        block(j - 1, sb_ref, False)
        return c

    lax.fori_loop(0, qi, pair, 0)
    for hh in range(heads):
        o_ref[:, hh * HEAD_DIM:(hh + 1) * HEAD_DIM] = acc_ref[hh].T.astype(o_ref.dtype)


def _sb_attention(qkv, B, S, n_heads):
    T = B * S
    tq = _tile(S, 512)
    tk = tq // 2
    nq = S // tq
    heads = _tile(n_heads, ATTN_HEADS_PER_STEP)
    width = heads * HEAD_DIM
    hb = n_heads // heads
    body = functools.partial(_sb_body, tq=tq, tk=tk, heads=heads)
    vmem = heads * (4 * S * HEAD_DIM * 2 + 4 * tq * HEAD_DIM * 2 + tq * HEAD_DIM * 4 + SUBLANES * tq * 4
                    + 14 * tq * tk * 4)
    return pl.pallas_call(
        body,
        out_shape=jax.ShapeDtypeStruct((T, n_heads * HEAD_DIM), BF16),
        grid=(B, n_heads // heads, nq),
        in_specs=[pl.BlockSpec((tq, width), lambda b, h, i: (b * nq + i, h)),
                  pl.BlockSpec((S, width), lambda b, h, i: (b, hb + h)),
                  pl.BlockSpec((S, width), lambda b, h, i: (b, 2 * hb + h))],
        out_specs=pl.BlockSpec((tq, width), lambda b, h, i: (b * nq + i, h)),
        scratch_shapes=[pltpu.VMEM((heads, HEAD_DIM, tq), F32), pltpu.VMEM((heads, 1, tq), F32),
                        pltpu.VMEM((heads, tk, tq), F32), pltpu.VMEM((heads, tk, tq), F32)],
        compiler_params=_params(("parallel", "parallel", "parallel"), vmem),
        name="stick_breaking_attention",
    )(qkv, qkv, qkv)


def _dilated_body(q_ref, k_ref, v_ref, *refs, configs, blk):
    n_cfg = len(configs)
    halos, o_ref = refs[:2 * n_cfg], refs[2 * n_cfg]
    part_o, part_lse = refs[2 * n_cfg + 1:3 * n_cfg + 1], refs[3 * n_cfg + 1:]
    span = q_ref.shape[0]
    span_idx = pl.program_id(2)
    row = lax.broadcasted_iota(jnp.int32, (blk, 2 * blk), 0)
    col = lax.broadcasted_iota(jnp.int32, (blk, 2 * blk), 1)
    dist = row + blk - col

    for c, (window, dil) in enumerate(configs):
        kp_ref, vp_ref = halos[2 * c], halos[2 * c + 1]
        band = (dist >= 0) & (dist <= window // dil)
        band_first = band & ((span_idx > 0) | (col >= blk))

        def rows(ref, start, dil=dil):
            return ref[pl.ds(start, blk, stride=dil), :] if dil > 1 else ref[pl.ds(start, blk), :]

        for r in range(dil):
            k_prev, v_prev = rows(kp_ref, r).astype(BF16), rows(vp_ref, r).astype(BF16)
            for j in range(span // (dil * blk)):
                start = r + dil * blk * j
                k_run, v_run = rows(k_ref, start).astype(BF16), rows(v_ref, start).astype(BF16)
                kk = jnp.concatenate([k_prev, k_run], axis=0)
                vv = jnp.concatenate([v_prev, v_run], axis=0)
                k_prev, v_prev = k_run, v_run
                s = _nt_dot(rows(q_ref, start).astype(BF16), kk)
                s = jnp.where(band if j else band_first, s, MASK_VALUE)
                m = jnp.max(s, axis=1, keepdims=True)
                p = jnp.exp(s - m)
                l = jnp.sum(p, axis=1, keepdims=True)
                out = jnp.dot(p.astype(BF16), vv, preferred_element_type=F32) / l
                lse = jnp.broadcast_to(m + jnp.log(l), (blk, HEAD_DIM))
                if dil > 1:
                    part_o[c][pl.ds(start, blk, stride=dil), :] = out
                    part_lse[c][pl.ds(start, blk, stride=dil), :] = lse
                else:
                    part_o[c][pl.ds(start, blk), :] = out
                    part_lse[c][pl.ds(start, blk), :] = lse

    top = part_lse[0][...]
    for lse_ref in part_lse[1:]:
        top = jnp.maximum(top, lse_ref[...])
    num = jnp.zeros(o_ref.shape, F32)
    den = jnp.zeros(o_ref.shape, F32)
    for out_ref, lse_ref in zip(part_o, part_lse):
        e = jnp.exp(lse_ref[...] - top)
        num = num + e * out_ref[...]
        den = den + e
    o_ref[...] = (num / den).astype(o_ref.dtype)


def _dilated_attention(qkv, B, S, n_heads):
    T = B * S
    blk = BAND_BLOCK
    span = _tile(S, DILATED_SPAN)
    spans = S // span

    def cur(section):
        return pl.BlockSpec((span, HEAD_DIM), lambda b, h, i: (b * spans + i, section * n_heads + h))

    def halo(section, dil):
        rows = dil * blk
        return pl.BlockSpec((rows, HEAD_DIM), lambda b, h, i: (
            jnp.maximum((b * spans + i) * (span // rows) - 1, 0), section * n_heads + h))

    halo_specs = [halo(section, dil) for _, dil in DILATED_CONFIGS for section in (1, 2)]
    n_cfg = len(DILATED_CONFIGS)
    body = functools.partial(_dilated_body, configs=DILATED_CONFIGS, blk=blk)
    halo_rows = sum(dil * blk for _, dil in DILATED_CONFIGS)
    vmem = (2 * (3 * span + 2 * halo_rows) * HEAD_DIM * 4 + 2 * n_cfg * span * HEAD_DIM * 4
            + 10 * span * HEAD_DIM * 4 + 24 * blk * 2 * blk * 4)
    return pl.pallas_call(
        body,
        out_shape=jax.ShapeDtypeStruct((T, n_heads * HEAD_DIM), BF16),
        grid=(B, n_heads, spans),
        in_specs=[cur(0), cur(1), cur(2)] + halo_specs,
        out_specs=pl.BlockSpec((span, HEAD_DIM), lambda b, h, i: (b * spans + i, h)),
        scratch_shapes=[pltpu.VMEM((span, HEAD_DIM), F32) for _ in range(2 * n_cfg)],
        compiler_params=_params(("parallel", "parallel", "parallel"), vmem),
        name="dilated_attention",
    )(*([qkv] * (3 + 2 * n_cfg)))


def _forget_body(a_ref, w_ref, b_ref, c_ref, carry_ref, *, ts):
    @pl.when(pl.program_id(1) == 0)
    def _():
        carry_ref[...] = jnp.zeros_like(carry_ref)

    x = _nt_dot(a_ref[...], w_ref[...].astype(BF16)) + b_ref[...]
    log_f = jnp.minimum(x, 0.0) - _softplus_neg_abs(x)
    lower = (lax.broadcasted_iota(jnp.int32, (ts, ts), 0)
             >= lax.broadcasted_iota(jnp.int32, (ts, ts), 1)).astype(BF16)
    cum = carry_ref[...]
    for part in _split_bf16(log_f, 3):
        cum = cum + jnp.dot(lower, part, preferred_element_type=F32)
    c_ref[...] = cum
    carry_ref[...] = cum[ts - 1:ts, :]


def _forget_cumsum(a, w_f, b_f, B, S):
    T, D = a.shape
    H = w_f.shape[0]
    ts = _tile(S, 512)
    ns = S // ts
    body = functools.partial(_forget_body, ts=ts)
    vmem = 4 * ts * D * 2 + 4 * D * 128 * 4 + 8 * ts * ts * 4
    return pl.pallas_call(
        body,
        out_shape=jax.ShapeDtypeStruct((T, H), F32),
        grid=(B, ns),
        in_specs=[pl.BlockSpec((ts, D), lambda b, s: (b * ns + s, 0)),
                  pl.BlockSpec((H, D), lambda b, s: (0, 0)),
                  pl.BlockSpec((1, H), lambda b, s: (0, 0))],
        out_specs=pl.BlockSpec((ts, H), lambda b, s: (b * ns + s, 0)),
        scratch_shapes=[pltpu.VMEM((1, H), F32)],
        compiler_params=_params(("parallel", "arbitrary"), vmem),
        name="forget_cumsum",
    )(a, w_f, b_f.reshape(1, H))


def _gate_lanes(parts, first_lane, sign):
    n_heads = parts[0].shape[1]
    row = lax.broadcasted_iota(jnp.int32, (n_heads, HEAD_DIM), 0)
    lane = lax.broadcasted_iota(jnp.int32, (n_heads, HEAD_DIM), 1)
    out = jnp.zeros((parts[0].shape[0], HEAD_DIM), F32)
    for i, part in enumerate(parts):
        place = jnp.where(lane == first_lane + GATE_PARTS * row + i, sign, 0.0).astype(BF16)
        out = out + jnp.dot(part, place, preferred_element_type=F32)
    return out


def _fox_body(q_ref, k_ref, v_ref, c_ref, o_ref, acc_ref, m_ref, l_ref, kx_ref, sa_ref, sb_ref, *, tq, tk, heads):
    hg = pl.program_id(1)
    qi = pl.program_id(2)
    heads_sl = [slice(hh * HEAD_DIM, (hh + 1) * HEAD_DIM) for hh in range(heads)]
    n_heads = c_ref.shape[1]
    lane = lax.broadcasted_iota(jnp.int32, (1, HEAD_DIM), 1)
    upper = HEAD_DIM // 2
    in_upper = (lane >= upper) & (lane < upper + GATE_PARTS * n_heads)

    @pl.when(qi == 0)
    def _():
        parts = _split_bf16(c_ref[...] * LOG2_E, GATE_PARTS)
        kx_ref[...] = (_gate_lanes(parts, 0, -1.0) + jnp.where(in_upper, 1.0, 0.0)).astype(BF16)

    q_parts = _split_bf16(c_ref[pl.ds(pl.multiple_of(qi * tq, tq), tq), :] * LOG2_E, GATE_PARTS)
    q_gates = _gate_lanes(q_parts, upper, 1.0) + jnp.where(lane < upper, 1.0, 0.0)
    qx = []
    for hh, sl in enumerate(heads_sl):
        first = GATE_PARTS * (hg * heads + hh)
        own = ((lane >= first) & (lane < first + GATE_PARTS)) | (
            (lane >= upper + first) & (lane < upper + first + GATE_PARTS))
        qx.append(jnp.concatenate([q_ref[:, sl], jnp.where(own, q_gates, 0.0).astype(BF16)], axis=1))
    acc_ref[...] = jnp.zeros_like(acc_ref)
    l_ref[...] = jnp.zeros_like(l_ref)
    m_ref[...] = jnp.full_like(m_ref, MASK_VALUE)

    def score_products(j, s_ref):
        start = pl.multiple_of(j * tk, tk)
        gates = kx_ref[pl.ds(start, tk), :]
        for hh, sl in enumerate(heads_sl):
            s_ref[hh] = _nt_dot(jnp.concatenate([k_ref[pl.ds(start, tk), sl], gates], axis=1), qx[hh])

    def block(j, s_ref, masked):
        start = pl.multiple_of(j * tk, tk)
        if masked:
            kpos = j * tk + lax.broadcasted_iota(jnp.int32, (tk, tq), 0)
            qpos = qi * tq + lax.broadcasted_iota(jnp.int32, (tk, tq), 1)
            mask = kpos <= qpos
        for hh, sl in enumerate(heads_sl):
            s = s_ref[hh]
            if masked:
                s = jnp.where(mask, s, MASK_VALUE)
            m_old = m_ref[hh]
            m_new = jnp.maximum(m_old, jnp.max(s, axis=0, keepdims=True))
            p = jnp.exp2(s - m_new)
            alpha = jnp.exp2(m_old - m_new)
            l_ref[hh] = alpha * l_ref[hh] + jnp.sum(p, axis=0, keepdims=True)
            acc_ref[hh] = alpha * acc_ref[hh] + _tn_dot(v_ref[pl.ds(start, tk), sl], p.astype(BF16))
            m_ref[hh] = m_new

    def pair(i, c):
        j = 2 * i
        score_products(j + 1, sb_ref)
        block(j, sa_ref, False)
        score_products(j + 2, sa_ref)
        block(j + 1, sb_ref, False)
        return c

    score_products(0, sa_ref)
    lax.fori_loop(0, qi // 2, pair, 0)

    @pl.when(qi % 2 == 0)
    def _():
        block(qi, sa_ref, True)

    @pl.when(qi % 2 == 1)
    def _():
        score_products(qi, sb_ref)
        block(qi - 1, sa_ref, False)
        block(qi, sb_ref, True)

    for hh in range(heads):
        o_ref[:, hh * HEAD_DIM:(hh + 1) * HEAD_DIM] = (acc_ref[hh] / l_ref[hh]).T.astype(o_ref.dtype)


def _fox_attention(qkv, c, B, S, D):
    T = B * S
    H = D // HEAD_DIM
    tq = tk = _tile(S, 512)
    nq = S // tq
    heads = _tile(H, ATTN_HEADS_PER_STEP)
    width = heads * HEAD_DIM
    hb = D // width
    body = functools.partial(_fox_body, tq=tq, tk=tk, heads=heads)
    vmem = (2 * S * 128 * 4 + heads * (4 * S * HEAD_DIM * 2 + 4 * tq * HEAD_DIM * 2 + S * 2 * HEAD_DIM * 2
                                       + tq * HEAD_DIM * 4 + 2 * SUBLANES * tq * 4 + 12 * tq * tk * 4))
    return pl.pallas_call(
        body,
        out_shape=jax.ShapeDtypeStruct((T, D), BF16),
        grid=(B, hb, nq),
        in_specs=[pl.BlockSpec((tq, width), lambda b, h, i: (b * nq + i, h)),
                  pl.BlockSpec((S, width), lambda b, h, i: (b, hb + h)),
                  pl.BlockSpec((S, width), lambda b, h, i: (b, 2 * hb + h)),
                  pl.BlockSpec((S, H), lambda b, h, i: (b, 0))],
        out_specs=pl.BlockSpec((tq, width), lambda b, h, i: (b * nq + i, h)),
        scratch_shapes=[pltpu.VMEM((heads, HEAD_DIM, tq), F32), pltpu.VMEM((heads, 1, tq), F32),
                        pltpu.VMEM((heads, 1, tq), F32), pltpu.VMEM((S, HEAD_DIM), BF16),
                        pltpu.VMEM((heads, tk, tq), F32), pltpu.VMEM((heads, tk, tq), F32)],
        compiler_params=_params(("parallel", "parallel", "arbitrary"), vmem),
        name="forgetting_attention",
    )(qkv, qkv, qkv, c)


def _rms_scaled(x, g):
    return x * lax.rsqrt(jnp.mean(x * x, axis=-1, keepdims=True) + RMS_EPS) * g


def _residual_mm_body(*refs, n_parts, norm):
    a_refs = refs[:n_parts]
    w_refs = refs[n_parts:2 * n_parts]
    rest = refs[2 * n_parts:]
    if norm:
        h_ref, g_ref, o_ref, on_ref = rest[:4]
        wbf_refs = rest[4:]
    else:
        h_ref, o_ref = rest[:2]
        wbf_refs = rest[2:]

    @pl.when(pl.program_id(1) == 0)
    def _():
        for w_ref, wbf_ref in zip(w_refs, wbf_refs):
            wbf_ref[...] = w_ref[...].astype(BF16)

    acc = h_ref[...]
    for a_ref, wbf_ref in zip(a_refs, wbf_refs):
        acc = acc + jnp.dot(a_ref[...], wbf_ref[...], preferred_element_type=F32)
    o_ref[...] = acc
    if norm:
        on_ref[...] = _rms_scaled(acc, g_ref[...]).astype(on_ref.dtype)


def _residual_matmul(h, parts, w, layer, tm_pref=512, norm_gain=None):
    T, N = h.shape
    n_parts = len(parts)
    kp = parts[0].shape[1]
    norm = norm_gain is not None
    tm = _tile(T, FULL_ROW_TM if norm else tm_pref)
    tn = N if norm else _tile(N, MATMUL_TN)
    body = functools.partial(_residual_mm_body, n_parts=n_parts, norm=norm)
    a_specs = [pl.BlockSpec((tm, kp), lambda n, m: (m, 0)) for _ in parts]
    w_specs = [pl.BlockSpec((None, kp, tn), lambda n, m, i=i: (layer, i, n), pipeline_mode=pl.Buffered(1))
               for i in range(n_parts)]
    hspec = pl.BlockSpec((tm, tn), lambda n, m: (m, n))
    vmem = n_parts * (2 * tm * kp * 2 + kp * tn * 4 + kp * tn * 2) + 8 * tm * tn * 4
    extra_in, extra_args, out_shape, out_specs = [], [], jax.ShapeDtypeStruct((T, N), F32), hspec
    if norm:
        extra_in, extra_args = [pl.BlockSpec((1, N), lambda n, m: (0, 0))], [norm_gain.reshape(1, N)]
        out_shape, out_specs = (out_shape, jax.ShapeDtypeStruct((T, N), BF16)), (hspec, hspec)
        vmem += 4 * tm * tn * 4
    return pl.pallas_call(
        body,
        out_shape=out_shape,
        grid=(N // tn, T // tm),
        in_specs=a_specs + w_specs + [hspec] + extra_in,
        out_specs=out_specs,
        scratch_shapes=[pltpu.VMEM((kp, tn), BF16) for _ in parts],
        compiler_params=_params(("parallel", "arbitrary"), vmem),
        name="residual_matmul",
    )(*parts, *([w] * n_parts), h, *extra_args)


def _ffn_up_body(a_ref, wg_ref, wv_ref, cwg_ref, cwv_ref, cbg_ref, cbv_ref, o_ref,
                 wg_bf_ref, wv_bf_ref, u_ref, tails_ref, *, tm, tiles_per_seq):
    m = pl.program_id(1)
    sub = tm // FFN_SUBTILES
    wbf_refs, cw_refs, cb_refs = (wg_bf_ref, wv_bf_ref), (cwg_ref, cwv_ref), (cbg_ref, cbv_ref)

    @pl.when(m == 0)
    def _():
        wg_bf_ref[...] = wg_ref[...].astype(BF16)
        wv_bf_ref[...] = wv_ref[...].astype(BF16)

    @pl.when(m % tiles_per_seq == 0)
    def _():
        tails_ref[0] = jnp.zeros(tails_ref.shape[1:], F32)

    def products(i):
        a = a_ref[i * sub:(i + 1) * sub, :]
        for g in range(2):
            u = jnp.dot(a, wbf_refs[g][...], preferred_element_type=F32)
            u_ref[i % 2, g] = u
            tails_ref[i + 1, g] = u[sub - SUBLANES:, :]

    def conv(i, g):
        u = u_ref[i % 2, g]
        row = lax.broadcasted_iota(jnp.int32, (SUBLANES, u.shape[1]), 0)
        cw = cw_refs[g][...]
        out = u * cw[CONV_WIDTH - 1:CONV_WIDTH, :] + cb_refs[g][...]
        for back in range(1, CONV_WIDTH):
            shifted = pltpu.roll(u, back, 0)
            top = shifted[:SUBLANES]
            for k in range(back):
                prev_row = tails_ref[i, g, SUBLANES - back + k:SUBLANES - back + k + 1, :]
                top = jnp.where(row == k, prev_row, top)
            shifted = jnp.concatenate([top, shifted[SUBLANES:]], axis=0)
            out = out + shifted * cw[CONV_WIDTH - 1 - back:CONV_WIDTH - back, :]
        return out

    def gated(i):
        gate, val = conv(i, 0), conv(i, 1)
        silu = gate / (1.0 + jnp.exp2(gate * -LOG2_E))
        o_ref[i * sub:(i + 1) * sub, :] = (silu * val).astype(o_ref.dtype)

    products(0)
    for i in range(1, FFN_SUBTILES):
        products(i)
        gated(i - 1)
    gated(FFN_SUBTILES - 1)
    tails_ref[0] = tails_ref[FFN_SUBTILES]


def _ffn_up(a, w_up, conv_w, conv_b, layer, S):
    T, D = a.shape
    F = w_up.shape[2] // 2
    tm = _tile(S, FFN_TOKEN_TILE)
    tn = _tile(F, 512)
    nf = F // tn
    body = functools.partial(_ffn_up_body, tm=tm, tiles_per_seq=S // tm)
    conv_b = conv_b.reshape(conv_b.shape[0], 1, 2 * F)
    sub = tm // FFN_SUBTILES
    vmem = (2 * tm * D * 2 + 2 * (2 * D * tn * 4 + D * tn * 2) + 2 * tm * tn * 2 + 4 * sub * tn * 4
            + 16 * sub * tn * 4)
    return pl.pallas_call(
        body,
        out_shape=jax.ShapeDtypeStruct((T, F), BF16),
        grid=(nf, T // tm),
        in_specs=[pl.BlockSpec((tm, D), lambda n, m: (m, 0)),
                  pl.BlockSpec((None, D, tn), lambda n, m: (layer, 0, n)),
                  pl.BlockSpec((None, D, tn), lambda n, m: (layer, 0, nf + n)),
                  pl.BlockSpec((None, CONV_WIDTH, tn), lambda n, m: (layer, 0, n)),
                  pl.BlockSpec((None, CONV_WIDTH, tn), lambda n, m: (layer, 0, nf + n)),
                  pl.BlockSpec((None, 1, tn), lambda n, m: (layer, 0, n)),
                  pl.BlockSpec((None, 1, tn), lambda n, m: (layer, 0, nf + n))],
        out_specs=pl.BlockSpec((tm, tn), lambda n, m: (m, n)),
        scratch_shapes=[pltpu.VMEM((D, tn), BF16), pltpu.VMEM((D, tn), BF16),
                        pltpu.VMEM((2, 2, sub, tn), F32),
                        pltpu.VMEM((FFN_SUBTILES + 1, 2, SUBLANES, tn), F32)],
        compiler_params=_params(("parallel", "arbitrary"), vmem),
        name="ffn_up_conv_gate",
    )(a, w_up, w_up, conv_w, conv_w, conv_b, conv_b)


def _ple_body(*refs, want_h):
    h_ref, gate_gain_ref, wg_ref, p_ref, wp_ref, g_ref = refs[:6]
    o_ref = refs[6] if want_h else None
    on_ref, wg_bf_ref, wp_bf_ref = refs[-3:]

    @pl.when(pl.program_id(0) == 0)
    def _():
        wg_bf_ref[...] = wg_ref[...].astype(BF16)
        wp_bf_ref[...] = wp_ref[...].astype(BF16)

    h = h_ref[...]
    a = _rms_scaled(h, gate_gain_ref[...]).astype(BF16)
    logits = jnp.dot(a, wg_bf_ref[...], preferred_element_type=F32)
    emb = jnp.dot(p_ref[...].astype(BF16), wp_bf_ref[...], preferred_element_type=F32)
    out = h + emb / (1.0 + jnp.exp(-logits))
    if want_h:
        o_ref[...] = out
    on_ref[...] = _rms_scaled(out, g_ref[...]).astype(on_ref.dtype)


def _ple(h, gate_gain, w_gate, p, w_ple, layer, norm_gain, norm_dtype, want_h):
    T, D = h.shape
    P = p.shape[2]
    tm = _tile(T, FULL_ROW_TM)
    row = pl.BlockSpec((tm, D), lambda m: (m, 0))
    gain = pl.BlockSpec((1, D), lambda m: (0, 0))
    vmem = (D * D * 6 + 2 * tm * P * 4 + P * D * 6 + 4 * tm * D * 4
            + 2 * tm * D * jnp.dtype(norm_dtype).itemsize + 10 * tm * D * 4)
    normed = jax.ShapeDtypeStruct((T, D), norm_dtype)
    outs = pl.pallas_call(
        functools.partial(_ple_body, want_h=want_h),
        out_shape=(jax.ShapeDtypeStruct((T, D), F32), normed) if want_h else (normed,),
        grid=(T // tm,),
        in_specs=[row, gain,
                  pl.BlockSpec((None, D, D), lambda m: (layer, 0, 0), pipeline_mode=pl.Buffered(1)),
                  pl.BlockSpec((None, tm, P), lambda m: (layer, m, 0)),
                  pl.BlockSpec((None, P, D), lambda m: (layer, 0, 0), pipeline_mode=pl.Buffered(1)),
                  gain],
        out_specs=(row, row) if want_h else (row,),
        scratch_shapes=[pltpu.VMEM((D, D), BF16), pltpu.VMEM((P, D), BF16)],
        compiler_params=_params(("arbitrary",), vmem),
        name="gated_layer_embedding",
    )(h, gate_gain.reshape(1, D), w_gate, p, w_ple, norm_gain.reshape(1, D))
    return outs if want_h else (None, outs[0])


def kernel(x, p, positions, attn_norm, w_in_even, w_out_even, w_in_odd, b_forget, w_out_odd, ffn_norm,
           w_up, conv_w, conv_b, w_down, ple_norm, w_ple_gate, w_ple, final_norm):
    B, S, D = x.shape
    T = B * S
    depth = p.shape[0]
    H = D // HEAD_DIM
    n_sb = H // 2
    n_dil = H - n_sb
    h = x.reshape(T, D)
    p = p.reshape(depth, T, p.shape[-1])
    w_in_odd_t = jnp.swapaxes(w_in_odd, 1, 2)
    rope_tabs = _rope_tables(positions)
    a = _rmsnorm(h, attn_norm[0], BF16)
    for i in range(depth):
        if i % 2 == 0:
            scale = HEAD_DIM ** -0.5
            qkv_sb = _qkv_proj(a, w_in_even, i // 2, LOG2_E * scale, 0, n_sb, BF16)
            qkv_dil = _qkv_proj(a, w_in_even, i // 2, scale, n_sb, n_dil, F32, rope_tabs)
            mixed = [_sb_attention(qkv_sb, B, S, n_sb), _dilated_attention(qkv_dil, B, S, n_dil)]
            w_out = w_out_even
        else:
            qkv = _qkv_proj(a, w_in_odd_t, i // 2, LOG2_E * HEAD_DIM ** -0.5, 0, H, BF16, w_transposed=True)
            c = _forget_cumsum(a, w_in_odd_t[i // 2, 3 * D:, :], b_forget[i // 2], B, S)
            mixed = [_fox_attention(qkv, c, B, S, D)]
            w_out = w_out_odd
        h, a = _residual_matmul(h, mixed, w_out, i // 2, norm_gain=ffn_norm[i])
        act = _ffn_up(a, w_up, conv_w, conv_b, i, S)
        h = _residual_matmul(h, [act], w_down, i, tm_pref=256)
        last = i == depth - 1
        h, a = _ple(h, ple_norm[i], w_ple_gate, p, w_ple, i,
                    final_norm if last else attn_norm[i + 1], F32 if last else BF16, want_h=not last)
    return a.reshape(B, S, D)
```

```python
import functools

import jax
import jax.numpy as jnp
from jax import lax
from jax.experimental import pallas as pl
from jax.experimental.pallas import tpu as pltpu

HEAD_DIM = 128
ROPE_DIM = HEAD_DIM // 4
ROPE_THETA = 500000.0
DILATED_CONFIGS = ((128, 1), (512, 4), (2048, 16))
BAND_BLOCK = 128
DILATED_SPAN = 2048
RMS_EPS = 1e-6
CONV_WIDTH = 3
MASK_VALUE = -1e30
ATTN_HEADS_PER_STEP = 4
LOG2_E = 1.4426950408889634
GATE_PARTS = 3
SB_SUFFIX_PARTS = 1
MATMUL_TN = 1024
FULL_ROW_TM = 256
FFN_TOKEN_TILE = 2048
FFN_SUBTILES = 4
SUBLANES = 8
VMEM_CAP_BYTES = 60 * 1024 * 1024

F32 = jnp.float32
BF16 = jnp.bfloat16


def _params(semantics, vmem_bytes):
    return pltpu.CompilerParams(dimension_semantics=semantics,
                                vmem_limit_bytes=int(min(vmem_bytes, VMEM_CAP_BYTES)))


def _tile(dim, pref):
    t = min(pref, dim)
    while dim % t:
        t //= 2
    return t


def _nt_dot(a, b):
    return lax.dot_general(a, b, (((1,), (1,)), ((), ())), preferred_element_type=F32)


def _tn_dot(a, b):
    return lax.dot_general(a, b, (((0,), (0,)), ((), ())), preferred_element_type=F32)


def _softplus_neg_abs(z):
    return jnp.log1p(jnp.exp(-jnp.abs(z)))


def _split_bf16(x, parts):
    out = []
    for _ in range(parts - 1):
        hi = x.astype(BF16)
        out.append(hi)
        x = x - hi.astype(F32)
    out.append(x.astype(BF16))
    return out


def _rmsnorm_body(h_ref, g_ref, o_ref):
    x = h_ref[...]
    ms = jnp.mean(x * x, axis=-1, keepdims=True)
    o_ref[...] = (x * lax.rsqrt(ms + RMS_EPS) * g_ref[...]).astype(o_ref.dtype)


def _rmsnorm(h, g, out_dtype):
    T, D = h.shape
    tm = _tile(T, 512)
    blk = tm * D * 4
    return pl.pallas_call(
        _rmsnorm_body,
        out_shape=jax.ShapeDtypeStruct((T, D), out_dtype),
        grid=(T // tm,),
        in_specs=[pl.BlockSpec((tm, D), lambda m: (m, 0)),
                  pl.BlockSpec((1, D), lambda m: (0, 0))],
        out_specs=pl.BlockSpec((tm, D), lambda m: (m, 0)),
        compiler_params=_params(("parallel",), 6 * blk),
        name="rmsnorm",
    )(h, g.reshape(1, D))


def _rope_table_body(pos_ref, invf_ref, cos_ref, sa_ref, sb_ref):
    ang = pos_ref[...].astype(F32) * invf_ref[...]
    lane = lax.broadcasted_iota(jnp.int32, ang.shape, 1)
    half = ROPE_DIM // 2
    s = jnp.sin(ang)
    cos_ref[...] = jnp.cos(ang)
    sa_ref[...] = jnp.where((lane >= half) & (lane < ROPE_DIM), s, 0.0)
    sb_ref[...] = jnp.where(lane < half, -s, 0.0)


def _rope_tables(positions):
    T = positions.size
    tm = _tile(T, 512)
    half = ROPE_DIM // 2
    inv_freq = ROPE_THETA ** (-jnp.arange(0, ROPE_DIM, 2, dtype=F32) / ROPE_DIM)
    invf = jnp.concatenate([inv_freq, inv_freq, jnp.zeros((HEAD_DIM - 2 * half,), F32)]).reshape(1, HEAD_DIM)
    tab = jax.ShapeDtypeStruct((T, HEAD_DIM), F32)
    spec = pl.BlockSpec((tm, HEAD_DIM), lambda m: (m, 0))
    return pl.pallas_call(
        _rope_table_body,
        out_shape=(tab, tab, tab),
        grid=(T // tm,),
        in_specs=[pl.BlockSpec((tm, 1), lambda m: (m, 0)),
                  pl.BlockSpec((1, HEAD_DIM), lambda m: (0, 0))],
        out_specs=(spec, spec, spec),
        compiler_params=_params(("parallel",), 32 * tm * HEAD_DIM * 4),
        name="rope_tables",
    )(positions.reshape(T, 1), invf)


def _qkv_body(*refs, rope, scale, tiles_per_section, heads_per_tile, w_transposed):
    if rope:
        a_ref, w_ref, cos_ref, sa_ref, sb_ref, o_ref, wbf_ref = refs
    else:
        a_ref, w_ref, o_ref, wbf_ref = refs
    n = pl.program_id(0)

    @pl.when(pl.program_id(1) == 0)
    def _():
        wbf_ref[...] = w_ref[...].astype(BF16)

    if w_transposed:
        acc = _nt_dot(a_ref[...], wbf_ref[...])
    else:
        acc = jnp.dot(a_ref[...], wbf_ref[...], preferred_element_type=F32)
    acc = acc * jnp.where(n < tiles_per_section, scale, 1.0).astype(F32)
    if not rope:
        o_ref[...] = acc.astype(o_ref.dtype)
        return

    @pl.when(n < 2 * tiles_per_section)
    def _():
        c, sa, sb = cos_ref[...], sa_ref[...], sb_ref[...]
        half = ROPE_DIM // 2
        for j in range(heads_per_tile):
            x = acc[:, j * HEAD_DIM:(j + 1) * HEAD_DIM]
            y = x * c + pltpu.roll(x, half, 1) * sa + pltpu.roll(x, HEAD_DIM - half, 1) * sb
            o_ref[:, j * HEAD_DIM:(j + 1) * HEAD_DIM] = y.astype(o_ref.dtype)

    @pl.when(n >= 2 * tiles_per_section)
    def _():
        o_ref[...] = acc.astype(o_ref.dtype)


def _qkv_proj(a, w, layer, q_scale, first_head, n_heads, out_dtype, rope_tabs=None, w_transposed=False):
    T, D = a.shape
    rope = rope_tabs is not None
    width = n_heads * HEAD_DIM
    tm = _tile(T, 512 if rope else 1024)
    tn = _tile(width, MATMUL_TN)
    tiles = width // tn
    per_section, first = D // tn, first_head * HEAD_DIM // tn
    body = functools.partial(_qkv_body, rope=rope, scale=q_scale, tiles_per_section=tiles,
                             heads_per_tile=tn // HEAD_DIM, w_transposed=w_transposed)

    def w_tile(n):
        return (n // tiles) * per_section + first + n % tiles

    if w_transposed:
        w_spec = pl.BlockSpec((None, tn, D), lambda n, m: (layer, w_tile(n), 0))
    else:
        w_spec = pl.BlockSpec((None, D, tn), lambda n, m: (layer, 0, w_tile(n)))
    in_specs = [pl.BlockSpec((tm, D), lambda n, m: (m, 0)), w_spec]
    args = [a, w]
    if rope:
        in_specs += [pl.BlockSpec((tm, HEAD_DIM), lambda n, m: (m, 0))] * 3
        args += list(rope_tabs)
    vmem = 2 * tm * D * 2 + 2 * D * tn * 4 + D * tn * 2 + 8 * tm * tn * 4 + 6 * tm * HEAD_DIM * 4
    return pl.pallas_call(
        body,
        out_shape=jax.ShapeDtypeStruct((T, 3 * width), out_dtype),
        grid=(3 * tiles, T // tm),
        in_specs=in_specs,
        out_specs=pl.BlockSpec((tm, tn), lambda n, m: (m, n)),
        scratch_shapes=[pltpu.VMEM((tn, D) if w_transposed else (D, tn), BF16)],
        compiler_params=_params(("parallel", "arbitrary"), vmem),
        name="qkv_proj",
    )(*args)


def _sb_body(q_ref, k_ref, v_ref, o_ref, acc_ref, carry_ref, sa_ref, sb_ref, *, tq, tk, heads):
    qi = pl.program_id(2)
    q = q_ref[...]
    acc_ref[...] = jnp.zeros_like(acc_ref)
    carry_ref[...] = jnp.zeros_like(carry_ref)
    strict_upper = (lax.broadcasted_iota(jnp.int32, (tk, tk), 1)
                    > lax.broadcasted_iota(jnp.int32, (tk, tk), 0)).astype(BF16)
    strict_upper_n = jnp.concatenate([strict_upper] * SB_SUFFIX_PARTS, axis=1)

    heads_sl = [slice(hh * HEAD_DIM, (hh + 1) * HEAD_DIM) for hh in range(heads)]

    def score_products(j, s_ref):
        start = pl.multiple_of(j * tk, tk)
        for hh, sl in enumerate(heads_sl):
            s_ref[hh] = _nt_dot(k_ref[pl.ds(start, tk), sl], q[:, sl])

    def block(j, s_ref, masked):
        start = pl.multiple_of(j * tk, tk)
        if masked:
            kpos = j * tk + lax.broadcasted_iota(jnp.int32, (tk, tq), 0)
            qpos = qi * tq + lax.broadcasted_iota(jnp.int32, (tk, tq), 1)
            mask = kpos < qpos
        scores = [s_ref[hh] for hh in range(heads)]
        costs =[jnp.maximum(z, 0.0) + jnp.log2(1.0 + jnp.exp2(-jnp.abs(z))) for z in scores]
        if masked:
            costs = [jnp.where(mask, cost, 0.0) for cost in costs]
        suffixes = [jnp.dot(strict_upper_n, jnp.concatenate(_split_bf16(cost, SB_SUFFIX_PARTS), axis=0),
                            preferred_element_type=F32) for cost in costs]
        for hh, sl in enumerate(heads_sl):
            carry = carry_ref[hh]
            w = jnp.exp2(scores[hh] - costs[hh] - suffixes[hh] - carry)
            if masked:
                w = jnp.where(mask, w, 0.0)
            acc_ref[hh] += _tn_dot(v_ref[pl.ds(start, tk), sl], w.astype(BF16))
            carry_ref[hh] = carry + suffixes[hh][0:1, :] + costs[hh][0:1, :]

    first = 2 * qi + 1
    score_products(first, sa_ref)
    score_products(first - 1, sb_ref)
    block(first, sa_ref, True)
    score_products(jnp.maximum(first - 2, 0), sa_ref)
    block(first - 1, sb_ref, True)

    def pair(i, c):
        j = first - 2 - 2 * i
        score_products(j - 1, sb_ref)
        block(j, sa_ref, False)
        score_products(jnp.maximum(j - 2, 0), sa_ref)
        block(j - 1, sb_ref, False)
        return c

    lax.fori_loop(0, qi, pair, 0)
    for hh in range(heads):
        o_ref[:, hh * HEAD_DIM:(hh + 1) * HEAD_DIM] = acc_ref[hh].T.astype(o_ref.dtype)


def _sb_attention(qkv, B, S, n_heads):
    T = B * S
    tq = _tile(S, 512)
    tk = tq // 2
    nq = S // tq
    heads = _tile(n_heads, ATTN_HEADS_PER_STEP)
    width = heads * HEAD_DIM
    hb = n_heads // heads
    body = functools.partial(_sb_body, tq=tq, tk=tk, heads=heads)
    vmem = heads * (4 * S * HEAD_DIM * 2 + 4 * tq * HEAD_DIM * 2 + tq * HEAD_DIM * 4 + SUBLANES * tq * 4
                    + 14 * tq * tk * 4)
    return pl.pallas_call(
        body,
        out_shape=jax.ShapeDtypeStruct((T, n_heads * HEAD_DIM), BF16),
        grid=(B, n_heads // heads, nq),
        in_specs=[pl.BlockSpec((tq, width), lambda b, h, i: (b * nq + i, h)),
                  pl.BlockSpec((S, width), lambda b, h, i: (b, hb + h)),
                  pl.BlockSpec((S, width), lambda b, h, i: (b, 2 * hb + h))],
        out_specs=pl.BlockSpec((tq, width), lambda b, h, i: (b * nq + i, h)),
        scratch_shapes=[pltpu.VMEM((heads, HEAD_DIM, tq), F32), pltpu.VMEM((heads, 1, tq), F32),
                        pltpu.VMEM((heads, tk, tq), F32), pltpu.VMEM((heads, tk, tq), F32)],
        compiler_params=_params(("parallel", "parallel", "parallel"), vmem),
        name="stick_breaking_attention",
    )(qkv, qkv, qkv)


def _dilated_body(q_ref, k_ref, v_ref, *refs, configs, blk):
    n_cfg = len(configs)
    halos, o_ref = refs[:2 * n_cfg], refs[2 * n_cfg]
    part_o, part_lse = refs[2 * n_cfg + 1:3 * n_cfg + 1], refs[3 * n_cfg + 1:]
    span = q_ref.shape[0]
    span_idx = pl.program_id(2)
    row = lax.broadcasted_iota(jnp.int32, (blk, 2 * blk), 0)
    col = lax.broadcasted_iota(jnp.int32, (blk, 2 * blk), 1)
    dist = row + blk - col

    for c, (window, dil) in enumerate(configs):
        kp_ref, vp_ref = halos[2 * c], halos[2 * c + 1]
        band = (dist >= 0) & (dist <= window // dil)
        band_first = band & ((span_idx > 0) | (col >= blk))

        def rows(ref, start, dil=dil):
            return ref[pl.ds(start, blk, stride=dil), :] if dil > 1 else ref[pl.ds(start, blk), :]

        for r in range(dil):
            k_prev, v_prev = rows(kp_ref, r).astype(BF16), rows(vp_ref, r).astype(BF16)
            for j in range(span // (dil * blk)):
                start = r + dil * blk * j
                k_run, v_run = rows(k_ref, start).astype(BF16), rows(v_ref, start).astype(BF16)
                kk = jnp.concatenate([k_prev, k_run], axis=0)
                vv = jnp.concatenate([v_prev, v_run], axis=0)
                k_prev, v_prev = k_run, v_run
                s = _nt_dot(rows(q_ref, start).astype(BF16), kk)
                s = jnp.where(band if j else band_first, s, MASK_VALUE)
                m = jnp.max(s, axis=1, keepdims=True)
                p = jnp.exp(s - m)
                l = jnp.sum(p, axis=1, keepdims=True)
                out = jnp.dot(p.astype(BF16), vv, preferred_element_type=F32) / l
                lse = jnp.broadcast_to(m + jnp.log(l), (blk, HEAD_DIM))
                if dil > 1:
                    part_o[c][pl.ds(start, blk, stride=dil), :] = out
                    part_lse[c][pl.ds(start, blk, stride=dil), :] = lse
                else:
                    part_o[c][pl.ds(start, blk), :] = out
                    part_lse[c][pl.ds(start, blk), :] = lse

    top = part_lse[0][...]
    for lse_ref in part_lse[1:]:
        top = jnp.maximum(top, lse_ref[...])
    num = jnp.zeros(o_ref.shape, F32)
    den = jnp.zeros(o_ref.shape, F32)
    for out_ref, lse_ref in zip(part_o, part_lse):
        e = jnp.exp(lse_ref[...] - top)
        num = num + e * out_ref[...]
        den = den + e
    o_ref[...] = (num / den).astype(o_ref.dtype)


def _dilated_attention(qkv, B, S, n_heads):
    T = B * S
    blk = BAND_BLOCK
    span = _tile(S, DILATED_SPAN)
    spans = S // span

    def cur(section):
        return pl.BlockSpec((span, HEAD_DIM), lambda b, h, i: (b * spans + i, section * n_heads + h))

    def halo(section, dil):
        rows = dil * blk
        return pl.BlockSpec((rows, HEAD_DIM), lambda b, h, i: (
            jnp.maximum((b * spans + i) * (span // rows) - 1, 0), section * n_heads + h))

    halo_specs = [halo(section, dil) for _, dil in DILATED_CONFIGS for section in (1, 2)]
    n_cfg = len(DILATED_CONFIGS)
    body = functools.partial(_dilated_body, configs=DILATED_CONFIGS, blk=blk)
    halo_rows = sum(dil * blk for _, dil in DILATED_CONFIGS)
    vmem = (2 * (3 * span + 2 * halo_rows) * HEAD_DIM * 4 + 2 * n_cfg * span * HEAD_DIM * 4
            + 10 * span * HEAD_DIM * 4 + 24 * blk * 2 * blk * 4)
    return pl.pallas_call(
        body,
        out_shape=jax.ShapeDtypeStruct((T, n_heads * HEAD_DIM), BF16),
        grid=(B, n_heads, spans),
        in_specs=[cur(0), cur(1), cur(2)] + halo_specs,
        out_specs=pl.BlockSpec((span, HEAD_DIM), lambda b, h, i: (b * spans + i, h)),
        scratch_shapes=[pltpu.VMEM((span, HEAD_DIM), F32) for _ in range(2 * n_cfg)],
        compiler_params=_params(("parallel", "parallel", "parallel"), vmem),
        name="dilated_attention",
    )(*([qkv] * (3 + 2 * n_cfg)))


def _forget_body(a_ref, w_ref, b_ref, c_ref, carry_ref, *, ts):
    @pl.when(pl.program_id(1) == 0)
    def _():
        carry_ref[...] = jnp.zeros_like(carry_ref)

    x = _nt_dot(a_ref[...], w_ref[...].astype(BF16)) + b_ref[...]
    log_f = jnp.minimum(x, 0.0) - _softplus_neg_abs(x)
    lower = (lax.broadcasted_iota(jnp.int32, (ts, ts), 0)
             >= lax.broadcasted_iota(jnp.int32, (ts, ts), 1)).astype(BF16)
    cum = carry_ref[...]
    for part in _split_bf16(log_f, 3):
        cum = cum + jnp.dot(lower, part, preferred_element_type=F32)
    c_ref[...] = cum
    carry_ref[...] = cum[ts - 1:ts, :]


def _forget_cumsum(a, w_f, b_f, B, S):
    T, D = a.shape
    H = w_f.shape[0]
    ts = _tile(S, 512)
    ns = S // ts
    body = functools.partial(_forget_body, ts=ts)
    vmem = 4 * ts * D * 2 + 4 * D * 128 * 4 + 8 * ts * ts * 4
    return pl.pallas_call(
        body,
        out_shape=jax.ShapeDtypeStruct((T, H), F32),
        grid=(B, ns),
        in_specs=[pl.BlockSpec((ts, D), lambda b, s: (b * ns + s, 0)),
                  pl.BlockSpec((H, D), lambda b, s: (0, 0)),
                  pl.BlockSpec((1, H), lambda b, s: (0, 0))],
        out_specs=pl.BlockSpec((ts, H), lambda b, s: (b * ns + s, 0)),
        scratch_shapes=[pltpu.VMEM((1, H), F32)],
        compiler_params=_params(("parallel", "arbitrary"), vmem),
        name="forget_cumsum",
    )(a, w_f, b_f.reshape(1, H))


def _gate_lanes(parts, first_lane, sign):
    n_heads = parts[0].shape[1]
    row = lax.broadcasted_iota(jnp.int32, (n_heads, HEAD_DIM), 0)
    lane = lax.broadcasted_iota(jnp.int32, (n_heads, HEAD_DIM), 1)
    out = jnp.zeros((parts[0].shape[0], HEAD_DIM), F32)
    for i, part in enumerate(parts):
        place = jnp.where(lane == first_lane + GATE_PARTS * row + i, sign, 0.0).astype(BF16)
        out = out + jnp.dot(part, place, preferred_element_type=F32)
    return out


def _fox_body(q_ref, k_ref, v_ref, c_ref, o_ref, acc_ref, m_ref, l_ref, kx_ref, sa_ref, sb_ref, *, tq, tk, heads):
    hg = pl.program_id(1)
    qi = pl.program_id(2)
    heads_sl = [slice(hh * HEAD_DIM, (hh + 1) * HEAD_DIM) for hh in range(heads)]
    n_heads = c_ref.shape[1]
    lane = lax.broadcasted_iota(jnp.int32, (1, HEAD_DIM), 1)
    upper = HEAD_DIM // 2
    in_upper = (lane >= upper) & (lane < upper + GATE_PARTS * n_heads)

    @pl.when(qi == 0)
    def _():
        parts = _split_bf16(c_ref[...] * LOG2_E, GATE_PARTS)
        kx_ref[...] = (_gate_lanes(parts, 0, -1.0) + jnp.where(in_upper, 1.0, 0.0)).astype(BF16)

    q_parts = _split_bf16(c_ref[pl.ds(pl.multiple_of(qi * tq, tq), tq), :] * LOG2_E, GATE_PARTS)
    q_gates = _gate_lanes(q_parts, upper, 1.0) + jnp.where(lane < upper, 1.0, 0.0)
    qx = []
    for hh, sl in enumerate(heads_sl):
        first = GATE_PARTS * (hg * heads + hh)
        own = ((lane >= first) & (lane < first + GATE_PARTS)) | (
            (lane >= upper + first) & (lane < upper + first + GATE_PARTS))
        qx.append(jnp.concatenate([q_ref[:, sl], jnp.where(own, q_gates, 0.0).astype(BF16)], axis=1))
    acc_ref[...] = jnp.zeros_like(acc_ref)
    l_ref[...] = jnp.zeros_like(l_ref)
    m_ref[...] = jnp.full_like(m_ref, MASK_VALUE)

    def score_products(j, s_ref):
        start = pl.multiple_of(j * tk, tk)
        gates = kx_ref[pl.ds(start, tk), :]
        for hh, sl in enumerate(heads_sl):
            s_ref[hh] = _nt_dot(jnp.concatenate([k_ref[pl.ds(start, tk), sl], gates], axis=1), qx[hh])

    def block(j, s_ref, masked):
        start = pl.multiple_of(j * tk, tk)
        if masked:
            kpos = j * tk + lax.broadcasted_iota(jnp.int32, (tk, tq), 0)
            qpos = qi * tq + lax.broadcasted_iota(jnp.int32, (tk, tq), 1)
            mask = kpos <= qpos
        for hh, sl in enumerate(heads_sl):
            s = s_ref[hh]
            if masked:
                s = jnp.where(mask, s, MASK_VALUE)
            m_old = m_ref[hh]
            m_new = jnp.maximum(m_old, jnp.max(s, axis=0, keepdims=True))
            p = jnp.exp2(s - m_new)
            alpha = jnp.exp2(m_old - m_new)
            l_ref[hh] = alpha * l_ref[hh] + jnp.sum(p, axis=0, keepdims=True)
            acc_ref[hh] = alpha * acc_ref[hh] + _tn_dot(v_ref[pl.ds(start, tk), sl], p.astype(BF16))
            m_ref[hh] = m_new

    def pair(i, c):
        j = 2 * i
        score_products(j + 1, sb_ref)
        block(j, sa_ref, False)
        score_products(j + 2, sa_ref)
        block(j + 1, sb_ref, False)
        return c

    score_products(0, sa_ref)
    lax.fori_loop(0, qi // 2, pair, 0)

    @pl.when(qi % 2 == 0)
    def _():
        block(qi, sa_ref, True)

    @pl.when(qi % 2 == 1)
    def _():
        score_products(qi, sb_ref)
        block(qi - 1, sa_ref, False)
        block(qi, sb_ref, True)

    for hh in range(heads):
        o_ref[:, hh * HEAD_DIM:(hh + 1) * HEAD_DIM] = (acc_ref[hh] / l_ref[hh]).T.astype(o_ref.dtype)


def _fox_attention(qkv, c, B, S, D):
    T = B * S
    H = D // HEAD_DIM
    tq = tk = _tile(S, 512)
    nq = S // tq
    heads = _tile(H, ATTN_HEADS_PER_STEP)
    width = heads * HEAD_DIM
    hb = D // width
    body = functools.partial(_fox_body, tq=tq, tk=tk, heads=heads)
    vmem = (2 * S * 128 * 4 + heads * (4 * S * HEAD_DIM * 2 + 4 * tq * HEAD_DIM * 2 + S * 2 * HEAD_DIM * 2
                                       + tq * HEAD_DIM * 4 + 2 * SUBLANES * tq * 4 + 12 * tq * tk * 4))
    return pl.pallas_call(
        body,
        out_shape=jax.ShapeDtypeStruct((T, D), BF16),
        grid=(B, hb, nq),
        in_specs=[pl.BlockSpec((tq, width), lambda b, h, i: (b * nq + i, h)),
                  pl.BlockSpec((S, width), lambda b, h, i: (b, hb + h)),
                  pl.BlockSpec((S, width), lambda b, h, i: (b, 2 * hb + h)),
                  pl.BlockSpec((S, H), lambda b, h, i: (b, 0))],
        out_specs=pl.BlockSpec((tq, width), lambda b, h, i: (b * nq + i, h)),
        scratch_shapes=[pltpu.VMEM((heads, HEAD_DIM, tq), F32), pltpu.VMEM((heads, 1, tq), F32),
                        pltpu.VMEM((heads, 1, tq), F32), pltpu.VMEM((S, HEAD_DIM), BF16),
                        pltpu.VMEM((heads, tk, tq), F32), pltpu.VMEM((heads, tk, tq), F32)],
        compiler_params=_params(("parallel", "parallel", "arbitrary"), vmem),
        name="forgetting_attention",
    )(qkv, qkv, qkv, c)


def _rms_scaled(x, g):
    return x * lax.rsqrt(jnp.mean(x * x, axis=-1, keepdims=True) + RMS_EPS) * g


def _residual_mm_body(*refs, n_parts, norm):
    a_refs = refs[:n_parts]
    w_refs = refs[n_parts:2 * n_parts]
    rest = refs[2 * n_parts:]
    if norm:
        h_ref, g_ref, o_ref, on_ref = rest[:4]
        wbf_refs = rest[4:]
    else:
        h_ref, o_ref = rest[:2]
        wbf_refs = rest[2:]

    @pl.when(pl.program_id(1) == 0)
    def _():
        for w_ref, wbf_ref in zip(w_refs, wbf_refs):
            wbf_ref[...] = w_ref[...].astype(BF16)

    acc = h_ref[...]
    for a_ref, wbf_ref in zip(a_refs, wbf_refs):
        acc = acc + jnp.dot(a_ref[...], wbf_ref[...], preferred_element_type=F32)
    o_ref[...] = acc
    if norm:
        on_ref[...] = _rms_scaled(acc, g_ref[...]).astype(on_ref.dtype)


def _residual_matmul(h, parts, w, layer, tm_pref=512, norm_gain=None):
    T, N = h.shape
    n_parts = len(parts)
    kp = parts[0].shape[1]
    norm = norm_gain is not None
    tm = _tile(T, FULL_ROW_TM if norm else tm_pref)
    tn = N if norm else _tile(N, MATMUL_TN)
    body = functools.partial(_residual_mm_body, n_parts=n_parts, norm=norm)
    a_specs = [pl.BlockSpec((tm, kp), lambda n, m: (m, 0)) for _ in parts]
    w_specs = [pl.BlockSpec((None, kp, tn), lambda n, m, i=i: (layer, i, n), pipeline_mode=pl.Buffered(1))
               for i in range(n_parts)]
    hspec = pl.BlockSpec((tm, tn), lambda n, m: (m, n))
    vmem = n_parts * (2 * tm * kp * 2 + kp * tn * 4 + kp * tn * 2) + 8 * tm * tn * 4
    extra_in, extra_args, out_shape, out_specs = [], [], jax.ShapeDtypeStruct((T, N), F32), hspec
    if norm:
        extra_in, extra_args = [pl.BlockSpec((1, N), lambda n, m: (0, 0))], [norm_gain.reshape(1, N)]
        out_shape, out_specs = (out_shape, jax.ShapeDtypeStruct((T, N), BF16)), (hspec, hspec)
        vmem += 4 * tm * tn * 4
    return pl.pallas_call(
        body,
        out_shape=out_shape,
        grid=(N // tn, T // tm),
        in_specs=a_specs + w_specs + [hspec] + extra_in,
        out_specs=out_specs,
        scratch_shapes=[pltpu.VMEM((kp, tn), BF16) for _ in parts],
        compiler_params=_params(("parallel", "arbitrary"), vmem),
        name="residual_matmul",
    )(*parts, *([w] * n_parts), h, *extra_args)


def _ffn_up_body(a_ref, wg_ref, wv_ref, cwg_ref, cwv_ref, cbg_ref, cbv_ref, o_ref,
                 wg_bf_ref, wv_bf_ref, u_ref, tails_ref, *, tm, tiles_per_seq):
    m = pl.program_id(1)
    sub = tm // FFN_SUBTILES
    wbf_refs, cw_refs, cb_refs = (wg_bf_ref, wv_bf_ref), (cwg_ref, cwv_ref), (cbg_ref, cbv_ref)

    @pl.when(m == 0)
    def _():
        wg_bf_ref[...] = wg_ref[...].astype(BF16)
        wv_bf_ref[...] = wv_ref[...].astype(BF16)

    @pl.when(m % tiles_per_seq == 0)
    def _():
        tails_ref[0] = jnp.zeros(tails_ref.shape[1:], F32)

    def products(i):
        a = a_ref[i * sub:(i + 1) * sub, :]
        for g in range(2):
            u = jnp.dot(a, wbf_refs[g][...], preferred_element_type=F32)
            u_ref[i % 2, g] = u
            tails_ref[i + 1, g] = u[sub - SUBLANES:, :]

    def conv(i, g):
        u = u_ref[i % 2, g]
        row = lax.broadcasted_iota(jnp.int32, (SUBLANES, u.shape[1]), 0)
        cw = cw_refs[g][...]
        out = u * cw[CONV_WIDTH - 1:CONV_WIDTH, :] + cb_refs[g][...]
        for back in range(1, CONV_WIDTH):
            shifted = pltpu.roll(u, back, 0)
            top = shifted[:SUBLANES]
            for k in range(back):
                prev_row = tails_ref[i, g, SUBLANES - back + k:SUBLANES - back + k + 1, :]
                top = jnp.where(row == k, prev_row, top)
            shifted = jnp.concatenate([top, shifted[SUBLANES:]], axis=0)
            out = out + shifted * cw[CONV_WIDTH - 1 - back:CONV_WIDTH - back, :]
        return out

    def gated(i):
        gate, val = conv(i, 0), conv(i, 1)
        silu = gate / (1.0 + jnp.exp2(gate * -LOG2_E))
        o_ref[i * sub:(i + 1) * sub, :] = (silu * val).astype(o_ref.dtype)

    products(0)
    for i in range(1, FFN_SUBTILES):
        products(i)
        gated(i - 1)
    gated(FFN_SUBTILES - 1)
    tails_ref[0] = tails_ref[FFN_SUBTILES]


def _ffn_up(a, w_up, conv_w, conv_b, layer, S):
    T, D = a.shape
    F = w_up.shape[2] // 2
    tm = _tile(S, FFN_TOKEN_TILE)
    tn = _tile(F, 512)
    nf = F // tn
    body = functools.partial(_ffn_up_body, tm=tm, tiles_per_seq=S // tm)
    conv_b = conv_b.reshape(conv_b.shape[0], 1, 2 * F)
    sub = tm // FFN_SUBTILES
    vmem = (2 * tm * D * 2 + 2 * (2 * D * tn * 4 + D * tn * 2) + 2 * tm * tn * 2 + 4 * sub * tn * 4
            + 16 * sub * tn * 4)
    return pl.pallas_call(
        body,
        out_shape=jax.ShapeDtypeStruct((T, F), BF16),
        grid=(nf, T // tm),
        in_specs=[pl.BlockSpec((tm, D), lambda n, m: (m, 0)),
                  pl.BlockSpec((None, D, tn), lambda n, m: (layer, 0, n)),
                  pl.BlockSpec((None, D, tn), lambda n, m: (layer, 0, nf + n)),
                  pl.BlockSpec((None, CONV_WIDTH, tn), lambda n, m: (layer, 0, n)),
                  pl.BlockSpec((None, CONV_WIDTH, tn), lambda n, m: (layer, 0, nf + n)),
                  pl.BlockSpec((None, 1, tn), lambda n, m: (layer, 0, n)),
                  pl.BlockSpec((None, 1, tn), lambda n, m: (layer, 0, nf + n))],
        out_specs=pl.BlockSpec((tm, tn), lambda n, m: (m, n)),
        scratch_shapes=[pltpu.VMEM((D, tn), BF16), pltpu.VMEM((D, tn), BF16),
                        pltpu.VMEM((2, 2, sub, tn), F32),
                        pltpu.VMEM((FFN_SUBTILES + 1, 2, SUBLANES, tn), F32)],
        compiler_params=_params(("parallel", "arbitrary"), vmem),
        name="ffn_up_conv_gate",
    )(a, w_up, w_up, conv_w, conv_w, conv_b, conv_b)


def _ple_body(*refs, want_h):
    h_ref, gate_gain_ref, wg_ref, p_ref, wp_ref, g_ref = refs[:6]
    o_ref = refs[6] if want_h else None
    on_ref, wg_bf_ref, wp_bf_ref = refs[-3:]

    @pl.when(pl.program_id(0) == 0)
    def _():
        wg_bf_ref[...] = wg_ref[...].astype(BF16)
        wp_bf_ref[...] = wp_ref[...].astype(BF16)

    h = h_ref[...]
    a = _rms_scaled(h, gate_gain_ref[...]).astype(BF16)
    logits = jnp.dot(a, wg_bf_ref[...], preferred_element_type=F32)
    emb = jnp.dot(p_ref[...].astype(BF16), wp_bf_ref[...], preferred_element_type=F32)
    out = h + emb / (1.0 + jnp.exp(-logits))
    if want_h:
        o_ref[...] = out
    on_ref[...] = _rms_scaled(out, g_ref[...]).astype(on_ref.dtype)


def _ple(h, gate_gain, w_gate, p, w_ple, layer, norm_gain, norm_dtype, want_h):
    T, D = h.shape
    P = p.shape[2]
    tm = _tile(T, FULL_ROW_TM)
    row = pl.BlockSpec((tm, D), lambda m: (m, 0))
    gain = pl.BlockSpec((1, D), lambda m: (0, 0))
    vmem = (D * D * 6 + 2 * tm * P * 4 + P * D * 6 + 4 * tm * D * 4
            + 2 * tm * D * jnp.dtype(norm_dtype).itemsize + 10 * tm * D * 4)
    normed = jax.ShapeDtypeStruct((T, D), norm_dtype)
    outs = pl.pallas_call(
        functools.partial(_ple_body, want_h=want_h),
        out_shape=(jax.ShapeDtypeStruct((T, D), F32), normed) if want_h else (normed,),
        grid=(T // tm,),
        in_specs=[row, gain,
                  pl.BlockSpec((None, D, D), lambda m: (layer, 0, 0), pipeline_mode=pl.Buffered(1)),
                  pl.BlockSpec((None, tm, P), lambda m: (layer, m, 0)),
                  pl.BlockSpec((None, P, D), lambda m: (layer, 0, 0), pipeline_mode=pl.Buffered(1)),
                  gain],
        out_specs=(row, row) if want_h else (row,),
        scratch_shapes=[pltpu.VMEM((D, D), BF16), pltpu.VMEM((P, D), BF16)],
        compiler_params=_params(("arbitrary",), vmem),
        name="gated_layer_embedding",
    )(h, gate_gain.reshape(1, D), w_gate, p, w_ple, norm_gain.reshape(1, D))
    return outs if want_h else (None, outs[0])


def kernel(x, p, positions, attn_norm, w_in_even, w_out_even, w_in_odd, b_forget, w_out_odd, ffn_norm,
           w_up, conv_w, conv_b, w_down, ple_norm, w_ple_gate, w_ple, final_norm):
    B, S, D = x.shape
    T = B * S
    depth = p.shape[0]
    H = D // HEAD_DIM
    n_sb = H // 2
    n_dil = H - n_sb
    h = x.reshape(T, D)
    p = p.reshape(depth, T, p.shape[-1])
    w_in_odd_t = jnp.swapaxes(w_in_odd, 1, 2)
    rope_tabs = _rope_tables(positions)
    a = _rmsnorm(h, attn_norm[0], BF16)
    for i in range(depth):
        if i % 2 == 0:
            scale = HEAD_DIM ** -0.5
            qkv_sb = _qkv_proj(a, w_in_even, i // 2, LOG2_E * scale, 0, n_sb, BF16)
            qkv_dil = _qkv_proj(a, w_in_even, i // 2, scale, n_sb, n_dil, F32, rope_tabs)
            mixed = [_sb_attention(qkv_sb, B, S, n_sb), _dilated_attention(qkv_dil, B, S, n_dil)]
            w_out = w_out_even
        else:
            qkv = _qkv_proj(a, w_in_odd_t, i // 2, LOG2_E * HEAD_DIM ** -0.5, 0, H, BF16, w_transposed=True)
            c = _forget_cumsum(a, w_in_odd_t[i // 2, 3 * D:, :], b_forget[i // 2], B, S)
            mixed = [_fox_attention(qkv, c, B, S, D)]
            w_out = w_out_odd
        h, a = _residual_matmul(h, mixed, w_out, i // 2, norm_gain=ffn_norm[i])
        act = _ffn_up(a, w_up, conv_w, conv_b, i, S)
        h = _residual_matmul(h, [act], w_down, i, tm_pref=256)
        last = i == depth - 1
        h, a = _ple(h, ple_norm[i], w_ple_gate, p, w_ple, i,
                    final_norm if last else attn_norm[i + 1], F32 if last else BF16, want_h=not last)
    return a.reshape(B, S, D)
```
